```python
import math
import jax
import jax.numpy as jnp
from jax import lax
import numpy as np

D_MODEL = 1024
BATCH = 4
SEQ = 8192
DEPTH = 2

CHUNK = 64
S5_WIDTH = D_MODEL // 4
S5_GROUP_CH = 16
S5_GROUPS = S5_WIDTH // S5_GROUP_CH
S5_STATE = 64
SB_HEADS = 4
SB_HEAD_DIM = 64
SB_WIDTH = SB_HEADS * SB_HEAD_DIM
SB_BLOCK = 128
RET_HEADS = 4
RET_DK = 64
RET_DV = 128
RET_QK_WIDTH = RET_HEADS * RET_DK
RET_V_WIDTH = RET_HEADS * RET_DV
ROPE_BASE = 10000.0
N_BRANCHES = 3
IN_WIDTH = S5_WIDTH + 3 * SB_WIDTH + 2 * RET_QK_WIDTH + 2 * RET_V_WIDTH + N_BRANCHES * D_MODEL
N_EXPERTS = 16
N_EXPERT_GROUPS = 4
EXPERTS_PER_GROUP = N_EXPERTS // N_EXPERT_GROUPS
TOP_K = 2
D_EXPERT = 1024
MOE_BLOCK = 128
ALPHA = (2 * DEPTH) ** 0.25
BETA = (8 * DEPTH) ** -0.25
LN_EPS = 1e-5

kernel_name = 'hybrid_streaming_encoder'


def layer_norm(x, g, b):
    xf = x.astype(jnp.float32)
    mu = jnp.mean(xf, axis=-1, keepdims=True)
    var = jnp.mean(jnp.square(xf - mu), axis=-1, keepdims=True)
    return ((xf - mu) * lax.rsqrt(var + LN_EPS) * g + b).astype(x.dtype)


def rope(t, pos):
    half = t.shape[-1] // 2
    inv_freq = ROPE_BASE ** (-jnp.arange(half, dtype=jnp.float32) / half)
    ang = pos[:, None] * inv_freq[None, :]
    cos = jnp.cos(ang)[None, :, None, :]
    sin = jnp.sin(ang)[None, :, None, :]
    t1, t2 = t[..., :half], t[..., half:]
    return jnp.concatenate([t1 * cos - t2 * sin, t1 * sin + t2 * cos], axis=-1)


def s5_mixer(u, lam_re, lam_im, log_dt, b_re, b_im, c_re, c_im, d_skip, w_glu, b_glu):
    Bsz, L, _ = u.shape
    uf = u.astype(jnp.float32).reshape(Bsz, L, S5_GROUPS, S5_GROUP_CH)
    lam = lax.complex(lam_re.astype(jnp.float32), lam_im.astype(jnp.float32))
    dt = jnp.exp(log_dt.astype(jnp.float32))[:, None]
    lam_bar = jnp.exp(lam * dt)
    zoh = (lam_bar - 1.0) / lam
    b_bar = lax.complex(b_re.astype(jnp.float32), b_im.astype(jnp.float32)) * zoh[..., None]
    bu = lax.complex(jnp.einsum('blgc,gpc->blgp', uf, jnp.real(b_bar)),
                     jnp.einsum('blgc,gpc->blgp', uf, jnp.imag(b_bar)))
    a = jnp.broadcast_to(lam_bar, (L,) + lam_bar.shape)

    def combine(e1, e2):
        a1, s1 = e1
        a2, s2 = e2
        return a1 * a2, a2 * s1 + s2

    def scan_one(bu_b):
        return lax.associative_scan(combine, (a, bu_b))[1]

    states = jax.vmap(scan_one)(bu)
    y = (jnp.einsum('blgp,gcp->blgc', jnp.real(states), c_re.astype(jnp.float32))
         - jnp.einsum('blgp,gcp->blgc', jnp.imag(states), c_im.astype(jnp.float32))
         + d_skip.astype(jnp.float32).reshape(S5_GROUPS, S5_GROUP_CH) * uf)
    y = jax.nn.gelu(y.reshape(Bsz, L, S5_WIDTH))
    y = y * jax.nn.sigmoid(y @ w_glu.astype(jnp.float32) + b_glu.astype(jnp.float32))
    return y.astype(u.dtype)


def stick_breaking_attention(q, k, v):
    Bsz, L, H, d = q.shape
    qf = q.astype(jnp.float32) * (d ** -0.5)
    kf = k.astype(jnp.float32)
    vf = v.astype(jnp.float32)
    nqb = L // SB_BLOCK
    qb = qf.reshape(Bsz, nqb, SB_BLOCK, H, d).transpose(1, 0, 3, 2, 4)
    key_pos = jnp.arange(L)

    def block(args):
        i, qblk = args
        qpos = i * SB_BLOCK + jnp.arange(SB_BLOCK)
        z = jnp.einsum('bhqd,bkhd->bhqk', qblk, kf)
        mask = key_pos[None, :] < qpos[:, None]
        log_1m = jnp.where(mask, jax.nn.log_sigmoid(-z), 0.0)
        after = lax.cumsum(log_1m, axis=3, reverse=True) - log_1m
        w = jnp.where(mask, jnp.exp(jax.nn.log_sigmoid(z) + after), 0.0)
        return jnp.einsum('bhqk,bkhd->bqhd', w, vf)

    out = lax.map(block, (jnp.arange(nqb), qb))
    return out.transpose(1, 0, 2, 3, 4).reshape(Bsz, L, H * d)


def chunkwise_retention(q, k, v, g, gn_g):
    Bsz, L, _ = q.shape
    n_chunks = L // CHUNK
    pos = jnp.arange(L, dtype=jnp.float32)
    qf = rope(q.astype(jnp.float32).reshape(Bsz, L, RET_HEADS, RET_DK), pos)
    kf = rope(k.astype(jnp.float32).reshape(Bsz, L, RET_HEADS, RET_DK), pos) * (RET_DK ** -0.5)
    vf = v.astype(jnp.float32).reshape(Bsz, L, RET_HEADS, RET_DV)

    def to_chunks(t):
        return t.reshape(Bsz, n_chunks, CHUNK, RET_HEADS, t.shape[-1]).transpose(0, 3, 1, 2, 4)

    qc, kc, vc = to_chunks(qf), to_chunks(kf), to_chunks(vf)
    log_gamma = jnp.log(1.0 - 2.0 ** (-5.0 - jnp.arange(RET_HEADS, dtype=jnp.float32)))
    idx = jnp.arange(CHUNK, dtype=jnp.float32)
    intra_decay = jnp.exp(log_gamma[:, None, None] * jnp.abs(idx[:, None] - idx[None, :]))
    scores = jnp.einsum('bhncd,bhnsd->bhncs', qc, kc) * intra_decay[None, :, None]
    intra = jnp.einsum('bhncs,bhnse->bhnce', scores, vc)
    k_dec = kc * jnp.exp(log_gamma[:, None] * (CHUNK - 1.0 - idx))[None, :, None, :, None]
    kv = jnp.einsum('bhnsd,bhnse->nbhde', k_dec, vc)
    chunk_decay = jnp.exp(log_gamma * CHUNK)[None, :, None, None]

    def step(state, kv_i):
        return chunk_decay * state + kv_i, state

    _, state_prev = lax.scan(step, jnp.zeros_like(kv[0]), kv)
    q_dec = qc * jnp.exp(log_gamma[:, None] * (idx + 1.0))[None, :, None, :, None]
    cross = jnp.einsum('bhncd,nbhde->bhnce', q_dec, state_prev)
    o = (intra + cross).transpose(0, 2, 3, 1, 4).reshape(Bsz, L, RET_HEADS, RET_DV)
    mu = jnp.mean(o, axis=-1, keepdims=True)
    var = jnp.mean(jnp.square(o - mu), axis=-1, keepdims=True)
    o = ((o - mu) * lax.rsqrt(var + LN_EPS)).reshape(Bsz, L, RET_V_WIDTH) * gn_g.astype(jnp.float32)
    return jax.nn.silu(g.astype(jnp.float32)) * o


def hybrid_mixer(x, w_in, lam_re, lam_im, log_dt, b_re, b_im, c_re, c_im, d_skip, w_glu, b_glu,
                 gn_g, w_up_a, w_up_b, w_up_c, w_out):
    Bsz, L, _ = x.shape
    h = x @ w_in
    sizes = (S5_WIDTH, SB_WIDTH, SB_WIDTH, SB_WIDTH, RET_QK_WIDTH, RET_QK_WIDTH,
             RET_V_WIDTH, RET_V_WIDTH, D_MODEL, D_MODEL, D_MODEL)
    cuts = np.cumsum(sizes)[:-1].tolist()
    u_a, q_b, k_b, v_b, q_c, k_c, v_c, g_c, gate_a, gate_b, gate_c = jnp.split(h, cuts, axis=-1)
    y_a = s5_mixer(u_a, lam_re, lam_im, log_dt, b_re, b_im, c_re, c_im, d_skip, w_glu, b_glu)
    hd = (Bsz, L, SB_HEADS, SB_HEAD_DIM)
    y_b = stick_breaking_attention(q_b.reshape(hd), k_b.reshape(hd), v_b.reshape(hd)).astype(x.dtype)
    y_c = chunkwise_retention(q_c, k_c, v_c, g_c, gn_g).astype(x.dtype)
    merged = (jax.nn.sigmoid(gate_a) * (y_a @ w_up_a)
              + jax.nn.sigmoid(gate_b) * (y_b @ w_up_b)
              + jax.nn.sigmoid(gate_c) * (y_c @ w_up_c))
    return merged @ w_out


def grouped_moe(x, router_w, router_b, w1, w3, w2):
    Bsz, L, D = x.shape
    T = Bsz * L
    xt = x.reshape(T, D)
    logits = (xt @ router_w).astype(jnp.float32) + router_b.astype(jnp.float32)
    scores = jax.nn.softmax(logits, axis=-1)
    grp_score = lax.top_k(scores.reshape(T, N_EXPERT_GROUPS, EXPERTS_PER_GROUP), 2)[0].sum(-1)
    best = jnp.argmax(grp_score, axis=-1)
    in_group = (jnp.arange(N_EXPERTS) // EXPERTS_PER_GROUP)[None, :] == best[:, None]
    top_w, top_e = lax.top_k(jnp.where(in_group, scores, -1.0), TOP_K)
    gates = top_w / jnp.sum(top_w, axis=-1, keepdims=True)
    n_assign = T * TOP_K
    flat_e = top_e.reshape(-1)
    flat_tok = jnp.arange(n_assign) // TOP_K
    flat_g = gates.reshape(-1)
    order = jnp.argsort(flat_e)
    sorted_e = flat_e[order]
    counts = jnp.bincount(flat_e, length=N_EXPERTS)
    starts = jnp.cumsum(counts) - counts
    padded = (counts + MOE_BLOCK - 1) // MOE_BLOCK * MOE_BLOCK
    pends = jnp.cumsum(padded)
    pstarts = pends - padded
    dest = pstarts[sorted_e] + (jnp.arange(n_assign) - starts[sorted_e])
    n_rows = n_assign + N_EXPERTS * MOE_BLOCK
    buf_tok = jnp.full((n_rows,), T, dtype=jnp.int32).at[dest].set(flat_tok[order].astype(jnp.int32))
    buf_gate = jnp.zeros((n_rows,), jnp.float32).at[dest].set(flat_g[order])
    n_blocks = n_rows // MOE_BLOCK
    block_e = jnp.clip(jnp.searchsorted(pends, jnp.arange(n_blocks) * MOE_BLOCK, side='right'), 0, N_EXPERTS - 1)
    x_pad = jnp.concatenate([xt, jnp.zeros((1, D), xt.dtype)], axis=0)
    xs = x_pad[buf_tok].reshape(n_blocks, MOE_BLOCK, D)

    def expert_block(args):
        xb, e = args
        hb = jax.nn.silu(xb @ w1[e]) * (xb @ w3[e])
        return hb @ w2[e]

    ys = lax.map(expert_block, (xs, block_e)).reshape(n_rows, D)
    ys = ys * buf_gate[:, None].astype(ys.dtype)
    out = jax.ops.segment_sum(ys, buf_tok, num_segments=T + 1)[:T]
    return out.reshape(Bsz, L, D).astype(x.dtype)


def setup_inputs(seed: int = 0) -> dict:
    key = jax.random.key(seed)
    ks = jax.random.split(key, 32)
    f32 = jnp.float32

    def nrm(k, shape, scale):
        return jax.random.normal(k, shape, f32) * scale

    Lr, D = DEPTH, D_MODEL
    G, P, Cg = S5_GROUPS, S5_STATE, S5_GROUP_CH
    E, F = N_EXPERTS, D_EXPERT
    return {
        'x': nrm(ks[0], (BATCH, SEQ, D), 1.0),
        'ln0_g': 1.0 + nrm(ks[1], (D,), 0.02),
        'ln0_b': nrm(ks[2], (D,), 0.02),
        'w_in': nrm(ks[3], (Lr, D, IN_WIDTH), D ** -0.5),
        's5_lambda_re': -0.5 * (1.0 + nrm(ks[4], (Lr, G, P), 0.02)),
        's5_lambda_im': math.pi * jnp.arange(P, dtype=f32) + nrm(ks[5], (Lr, G, P), 0.02),
        's5_log_dt': jax.random.uniform(ks[6], (Lr, G), f32, math.log(1e-3), math.log(1e-1)),
        's5_b_re': nrm(ks[7], (Lr, G, P, Cg), (2 * Cg) ** -0.5),
        's5_b_im': nrm(ks[8], (Lr, G, P, Cg), (2 * Cg) ** -0.5),
        's5_c_re': nrm(ks[9], (Lr, G, Cg, P), P ** -0.5),
        's5_c_im': nrm(ks[10], (Lr, G, Cg, P), P ** -0.5),
        's5_d': nrm(ks[11], (Lr, S5_WIDTH), 1.0),
        's5_w_glu': nrm(ks[12], (Lr, S5_WIDTH, S5_WIDTH), S5_WIDTH ** -0.5),
        's5_b_glu': nrm(ks[13], (Lr, S5_WIDTH), 0.02),
        'ret_gn_g': 1.0 + nrm(ks[14], (Lr, RET_V_WIDTH), 0.02),
        'w_up_a': nrm(ks[15], (Lr, S5_WIDTH, D), S5_WIDTH ** -0.5),
        'w_up_b': nrm(ks[16], (Lr, SB_WIDTH, D), SB_WIDTH ** -0.5),
        'w_up_c': nrm(ks[17], (Lr, RET_V_WIDTH, D), RET_V_WIDTH ** -0.5),
        'w_out': nrm(ks[18], (Lr, D, D), BETA * D ** -0.5),
        'ln1_g': 1.0 + nrm(ks[19], (Lr, D), 0.02),
        'ln1_b': nrm(ks[20], (Lr, D), 0.02),
        'router_w': nrm(ks[21], (D, E), D ** -0.5),
        'router_b': nrm(ks[22], (E,), 0.01),
        'moe_w1': nrm(ks[23], (Lr, E, D, F), D ** -0.5),
        'moe_w3': nrm(ks[24], (Lr, E, D, F), D ** -0.5),
        'moe_w2': nrm(ks[25], (Lr, E, F, D), BETA * F ** -0.5),
        'ln2_g': 1.0 + nrm(ks[26], (Lr, D), 0.02),
        'ln2_b': nrm(ks[27], (Lr, D), 0.02),
    }


def reference(x, ln0_g, ln0_b, w_in, s5_lambda_re, s5_lambda_im, s5_log_dt, s5_b_re, s5_b_im,
              s5_c_re, s5_c_im, s5_d, s5_w_glu, s5_b_glu, ret_gn_g, w_up_a, w_up_b, w_up_c, w_out,
              ln1_g, ln1_b, router_w, router_b, moe_w1, moe_w3, moe_w2, ln2_g, ln2_b):
    h = layer_norm(x, ln0_g, ln0_b)
    for l in range(DEPTH):
        y = hybrid_mixer(h, w_in[l], s5_lambda_re[l], s5_lambda_im[l], s5_log_dt[l], s5_b_re[l], s5_b_im[l],
                         s5_c_re[l], s5_c_im[l], s5_d[l], s5_w_glu[l], s5_b_glu[l], ret_gn_g[l],
                         w_up_a[l], w_up_b[l], w_up_c[l], w_out[l])
        h = layer_norm(ALPHA * h + y, ln1_g[l], ln1_b[l])
        y = grouped_moe(h, router_w, router_b, moe_w1[l], moe_w3[l], moe_w2[l])
        h = layer_norm(ALPHA * h + y, ln2_g[l], ln2_b[l])
    return h
```

```python
import functools
import math

import numpy as np
import jax
import jax.numpy as jnp
from jax import lax
from jax.experimental import pallas as pl
from jax.experimental.pallas import tpu as pltpu

F32 = jnp.float32
BF16 = jnp.bfloat16

LANES = 128
SUBLANES = 8
VMEM_LIMIT = 56 * 1024 * 1024

DEPTH = 2
CHUNK = 64
S5_GROUP_CH = 16
S5_STATE = 64
SB_HEADS = 4
SB_HEAD_DIM = 64
RET_HEADS = 4
RET_DK = 64
RET_DV = 128
ROPE_BASE = 10000.0
N_EXPERTS = 16
EXPERTS_PER_GROUP = 4
ALPHA = (2 * DEPTH) ** 0.25
LN_EPS = 1e-5

GATE_OFF = 0
NEG_BIG = -1e30


def _params(*sem):
    return pltpu.CompilerParams(dimension_semantics=sem, vmem_limit_bytes=VMEM_LIMIT)


def _const_spec(shape):
    zeros = (0,) * len(shape)
    return pl.BlockSpec(shape, lambda *_: zeros, pipeline_mode=pl.Buffered(1))


def _layer_norm(x, g, b):
    mu = jnp.mean(x, axis=-1, keepdims=True)
    xc = x - mu
    var = jnp.mean(xc * xc, axis=-1, keepdims=True)
    return xc * lax.rsqrt(var + LN_EPS) * g + b


def _ln_kernel(x_ref, g_ref, b_ref, o_ref):
    o_ref[...] = _layer_norm(x_ref[...], g_ref[...], b_ref[...])


def layer_norm_rows(x, g, b, tm=512):
    T, D = x.shape
    return pl.pallas_call(
        _ln_kernel,
        grid=(T // tm,),
        in_specs=[pl.BlockSpec((tm, D), lambda i: (i, 0)), _const_spec((1, D)), _const_spec((1, D))],
        out_specs=pl.BlockSpec((tm, D), lambda i: (i, 0)),
        out_shape=jax.ShapeDtypeStruct((T, D), F32),
        compiler_params=_params("parallel"),
        name="ln0",
    )(x, g.reshape(1, D), b.reshape(1, D))


def _inproj_kernel(x_ref, w_ref, o_ref, *, col_chunk):
    xb = x_ref[...].astype(BF16)
    n = w_ref.shape[1]
    for c in range(0, n, col_chunk):
        o_ref[:, c:c + col_chunk] = jnp.dot(
            xb, w_ref[:, c:c + col_chunk], preferred_element_type=F32).astype(BF16)


def in_projection(h, w_bf16, tm=512, col_chunk=512):
    T, D = h.shape
    N = w_bf16.shape[1]
    return pl.pallas_call(
        functools.partial(_inproj_kernel, col_chunk=col_chunk),
        grid=(T // tm,),
        in_specs=[pl.BlockSpec((tm, D), lambda i: (i, 0)), _const_spec((D, N))],
        out_specs=pl.BlockSpec((tm, N), lambda i: (i, 0)),
        out_shape=jax.ShapeDtypeStruct((T, N), BF16),
        compiler_params=_params("parallel"),
        name="in_proj",
    )(h, w_bf16)


def _s5_kernel(u_ref, bblk_ref, lam_ref, lamk_ref, cblk_ref, d_ref, wglu_ref, bglu_ref,
               o_ref, s_ref, carry_ref, *, n_state):
    tm = s_ref.shape[1]
    K = tm // SUBLANES
    ncol = n_state // LANES

    @pl.when(pl.program_id(1) == 0)
    def _():
        carry_ref[...] = jnp.zeros_like(carry_ref)

    u = u_ref[...]
    bu = jnp.dot(u, bblk_ref[...], preferred_element_type=F32)
    for c in range(2 * ncol):
        s_ref[c] = bu[:, c * LANES:(c + 1) * LANES]

    def cols(j):
        return pl.ds(j * LANES, LANES), pl.ds(n_state + j * LANES, LANES)

    a_re = [jnp.broadcast_to(lam_ref[0:1, pl.ds(j * LANES, LANES)], (SUBLANES, LANES)) for j in range(ncol)]
    a_im = [jnp.broadcast_to(lam_ref[1:2, pl.ds(j * LANES, LANES)], (SUBLANES, LANES)) for j in range(ncol)]

    def rows(k):
        return pl.ds(k, SUBLANES, stride=K)

    def pass1(k, st):
        out = []
        for j in range(ncol):
            sr, si = st[2 * j], st[2 * j + 1]
            nr = a_re[j] * sr - a_im[j] * si + s_ref[j, rows(k), :]
            ni = a_re[j] * si + a_im[j] * sr + s_ref[ncol + j, rows(k), :]
            s_ref[j, rows(k), :] = nr
            s_ref[ncol + j, rows(k), :] = ni
            out += [nr, ni]
        return tuple(out)

    zero = jnp.zeros((SUBLANES, LANES), F32)
    ends = lax.fori_loop(0, K, pass1, (zero,) * (2 * ncol))

    carries = []
    for j in range(ncol):
        cr, ci = cols(j)
        kr = lamk_ref[0:1, pl.ds(j * LANES, LANES)]
        ki = lamk_ref[1:2, pl.ds(j * LANES, LANES)]
        er, ei = ends[2 * j], ends[2 * j + 1]
        c_r = [carry_ref[0:1, cr]]
        c_i = [carry_ref[0:1, ci]]
        for r in range(1, SUBLANES + 1):
            pr, pi = c_r[-1], c_i[-1]
            c_r.append(kr * pr - ki * pi + er[r - 1:r, :])
            c_i.append(kr * pi + ki * pr + ei[r - 1:r, :])
        carry_ref[0:1, cr] = c_r[SUBLANES]
        carry_ref[0:1, ci] = c_i[SUBLANES]
        carries += [jnp.concatenate(c_r[:SUBLANES], axis=0), jnp.concatenate(c_i[:SUBLANES], axis=0)]

    def pass2(k, st):
        out = []
        for j in range(ncol):
            dr, di = st[2 * j], st[2 * j + 1]
            nr = a_re[j] * dr - a_im[j] * di
            ni = a_re[j] * di + a_im[j] * dr
            s_ref[j, rows(k), :] = s_ref[j, rows(k), :] + nr
            s_ref[ncol + j, rows(k), :] = s_ref[ncol + j, rows(k), :] + ni
            out += [nr, ni]
        return tuple(out)

    lax.fori_loop(0, K, pass2, tuple(carries))

    states = jnp.concatenate([s_ref[c].astype(BF16) for c in range(2 * ncol)], axis=1)
    y = jnp.dot(states, cblk_ref[...], preferred_element_type=F32)
    y = y + d_ref[...] * u.astype(F32)
    y = jax.nn.gelu(y, approximate=True)
    gate = jnp.dot(y.astype(BF16), wglu_ref[...], preferred_element_type=F32) + bglu_ref[...]
    o_ref[...] = (y * jax.nn.sigmoid(gate)).astype(BF16)


def s5_mixer(hp, u_col_block, batch, seq, bblk, lam, lamk, cblk, d_skip, w_glu, b_glu, tm=512):
    width = bblk.shape[0]
    n_state = bblk.shape[1] // 2
    nt = seq // tm
    return pl.pallas_call(
        functools.partial(_s5_kernel, n_state=n_state),
        grid=(batch, nt),
        in_specs=[
            pl.BlockSpec((tm, width), lambda b, t: (b * nt + t, u_col_block)),
            _const_spec(bblk.shape), _const_spec(lam.shape), _const_spec(lamk.shape),
            _const_spec(cblk.shape), _const_spec((1, width)), _const_spec(w_glu.shape),
            _const_spec((1, width)),
        ],
        out_specs=pl.BlockSpec((tm, width), lambda b, t: (b * nt + t, 0)),
        out_shape=jax.ShapeDtypeStruct((batch * seq, width), BF16),
        scratch_shapes=[pltpu.VMEM((2 * n_state // LANES, tm, LANES), F32),
                        pltpu.VMEM((SUBLANES, 2 * n_state), F32)],
        compiler_params=_params("parallel", "arbitrary"),
        name="s5_mixer",
    )(hp, bblk, lam, lamk, cblk, d_skip.reshape(1, width), w_glu, b_glu.reshape(1, width))


def s5_tables(lam_re, lam_im, log_dt, b_re, b_im, c_re, c_im, sub_chunk):
    G, P = lam_re.shape
    dt = jnp.exp(log_dt.astype(F32))[:, None]

    def lam_pow(n):
        mag = jnp.exp(lam_re * dt * n)
        return mag * jnp.cos(lam_im * dt * n), mag * jnp.sin(lam_im * dt * n)

    lr, li = lam_pow(1.0)
    kr, ki = lam_pow(float(sub_chunk))
    nr, ni = lr - 1.0, li
    den = lam_re * lam_re + lam_im * lam_im
    zr = (nr * lam_re + ni * lam_im) / den
    zi = (ni * lam_re - nr * lam_im) / den
    bbr = b_re * zr[..., None] - b_im * zi[..., None]
    bbi = b_re * zi[..., None] + b_im * zr[..., None]
    eye = jnp.eye(G, dtype=F32)
    Cg = b_re.shape[-1]

    def in_blk(m):
        return jnp.einsum('gpc,gh->gchp', m, eye).reshape(G * Cg, G * P)

    def out_blk(m):
        return jnp.einsum('gcp,gh->gphc', m, eye).reshape(G * P, G * Cg)

    bblk = jnp.concatenate([in_blk(bbr), in_blk(bbi)], axis=1).astype(BF16)
    cblk = jnp.concatenate([out_blk(c_re), -out_blk(c_im)], axis=0).astype(BF16)
    lam = jnp.stack([lr.reshape(-1), li.reshape(-1)])
    lamk = jnp.stack([kr.reshape(-1), ki.reshape(-1)])
    return bblk, lam, lamk, cblk


def _log_sigmoid_pair(z):
    soft = jnp.log1p(jnp.exp(-jnp.abs(z)))
    lp = jnp.minimum(z, 0.0) - soft
    return lp, lp - z


def _sb_kernel(q_ref, k_ref, v_ref, tri_ref, o_ref, acc_ref, carry_ref, *, blk):
    i = pl.program_id(2)
    q = q_ref[...]
    lane = lax.broadcasted_iota(jnp.int32, (1, LANES), 1)
    head_lanes = [lane < SB_HEAD_DIM, lane >= SB_HEAD_DIM]
    zero = jnp.zeros((), BF16)
    qh = [jnp.where(m, q, zero) for m in head_lanes]
    row = lax.broadcasted_iota(jnp.int32, (blk, blk), 0)
    col = lax.broadcasted_iota(jnp.int32, (blk, blk), 1)
    strictly_earlier = col < row

    acc_ref[...] = jnp.zeros_like(acc_ref)
    carry_ref[...] = jnp.zeros_like(carry_ref)

    def block_step(j, diagonal):
        kj = k_ref[pl.ds(pl.multiple_of(j * blk, blk), blk), :]
        vj = v_ref[pl.ds(pl.multiple_of(j * blk, blk), blk), :]
        for h in range(2):
            z = lax.dot_general(qh[h], kj, (((1,), (1,)), ((), ())), preferred_element_type=F32)
            lp, l1m = _log_sigmoid_pair(z)
            if diagonal:
                l1m = jnp.where(strictly_earlier, l1m, 0.0)
            r = jnp.dot(l1m.astype(BF16), tri_ref[...], preferred_element_type=F32)
            w = jnp.exp(lp + r[:, :blk] + carry_ref[h])
            if diagonal:
                w = jnp.where(strictly_earlier, w, 0.0)
            vh = jnp.where(head_lanes[h], vj, zero)
            acc_ref[...] += jnp.dot(w.astype(BF16), vh, preferred_element_type=F32)
            carry_ref[h] = carry_ref[h] + r[:, blk:]

    block_step(i, True)

    def body(jj, c):
        block_step(i - 1 - jj, False)
        return c

    lax.fori_loop(0, i, body, 0)
    o_ref[...] = acc_ref[...].astype(BF16)


def stick_breaking(hp, q_blk0, k_blk0, v_blk0, batch, seq, blk=128):
    assert blk == LANES
    nq = seq // blk
    n_pairs = SB_HEADS * SB_HEAD_DIM // LANES
    r = np.arange(blk)
    tri = np.concatenate([(r[:, None] > r[None, :]), np.ones((blk, blk), bool)], axis=1)
    tri = jnp.asarray(tri, BF16)
    return pl.pallas_call(
        functools.partial(_sb_kernel, blk=blk),
        grid=(batch, n_pairs, nq),
        in_specs=[
            pl.BlockSpec((blk, LANES), lambda b, p, i: (b * nq + i, q_blk0 + p)),
            pl.BlockSpec((seq, LANES), lambda b, p, i: (b, k_blk0 + p)),
            pl.BlockSpec((seq, LANES), lambda b, p, i: (b, v_blk0 + p)),
            _const_spec(tri.shape),
        ],
        out_specs=pl.BlockSpec((blk, LANES), lambda b, p, i: (b * nq + i, p)),
        out_shape=jax.ShapeDtypeStruct((batch * seq, n_pairs * LANES), BF16),
        scratch_shapes=[pltpu.VMEM((blk, LANES), F32), pltpu.VMEM((2, blk, blk), F32)],
        compiler_params=_params("parallel", "parallel", "arbitrary"),
        name="stick_breaking",
    )(hp, hp, hp, tri)


def _ret_kernel(qk_ref, v_ref, g_ref, cos_ref, sin_ref, mask_ref, rdec_ref, cdec_ref, tdec_ref, gn_ref,
                o_ref, state_ref):
    @pl.when(pl.program_id(1) == 0)
    def _():
        state_ref[...] = jnp.zeros_like(state_ref)

    half = RET_HEADS * RET_DK // 2
    qk = qk_ref[...].astype(F32)
    cos, sin = cos_ref[...], sin_ref[...]

    def rope(t):
        t1, t2 = t[:, :half], t[:, half:]
        return jnp.concatenate([t1 * cos - t2 * sin, t1 * sin + t2 * cos], axis=1)

    qr = rope(qk[:, :2 * half])
    kr = rope(qk[:, 2 * half:])
    kb = kr.astype(BF16)
    lane = lax.broadcasted_iota(jnp.int32, (1, 2 * half), 1)
    head_of_lane = (lane % half) // (RET_DK // 2)
    for h in range(RET_HEADS):
        qh = jnp.where(head_of_lane == h, qr, 0.0)
        vh = v_ref[:, h * RET_DV:(h + 1) * RET_DV]
        scores = lax.dot_general(qh.astype(BF16), kb, (((1,), (1,)), ((), ())), preferred_element_type=F32)
        scores = scores * mask_ref[h]
        o = jnp.dot(scores.astype(BF16), vh, preferred_element_type=F32)
        o = o + jnp.dot((qh * rdec_ref[h]).astype(BF16), state_ref[h].astype(BF16), preferred_element_type=F32)
        kv = lax.dot_general((kr * cdec_ref[h]).astype(BF16), vh, (((0,), (0,)), ((), ())),
                             preferred_element_type=F32)
        state_ref[h] = tdec_ref[h] * state_ref[h] + kv
        mu = jnp.mean(o, axis=-1, keepdims=True)
        oc = o - mu
        var = jnp.mean(oc * oc, axis=-1, keepdims=True)
        on = oc * lax.rsqrt(var + LN_EPS) * gn_ref[:, h * RET_DV:(h + 1) * RET_DV]
        g = g_ref[:, h * RET_DV:(h + 1) * RET_DV].astype(F32)
        o_ref[:, h * RET_DV:(h + 1) * RET_DV] = (g * jax.nn.sigmoid(g) * on).astype(BF16)


def retention_tables(seq, tile):
    halfdim = RET_DK // 2
    inv_freq = ROPE_BASE ** (-jnp.arange(halfdim, dtype=F32) / halfdim)
    ang = jnp.arange(seq, dtype=F32)[:, None] * inv_freq[None, :]
    cos = jnp.tile(jnp.cos(ang), (1, RET_HEADS))
    sin = jnp.tile(jnp.sin(ang), (1, RET_HEADS))
    log_gamma = jnp.log(1.0 - 2.0 ** (-5.0 - jnp.arange(RET_HEADS, dtype=F32)))
    t = jnp.arange(tile, dtype=F32)
    same_or_earlier_chunk = (t[None, :] // CHUNK) <= (t[:, None] // CHUNK)
    mask = jnp.exp(log_gamma[:, None, None] * jnp.abs(t[:, None] - t[None, :]))
    mask = jnp.where(same_or_earlier_chunk[None], mask, 0.0)
    rdec = jnp.exp(log_gamma[:, None, None] * (t[None, :, None] + 1.0))
    cdec = jnp.exp(log_gamma[:, None, None] * (tile - 1.0 - t[None, :, None]))
    tdec = jnp.broadcast_to(jnp.exp(log_gamma * tile)[:, None, None], (RET_HEADS, 1, RET_DV))
    return cos, sin, mask, rdec, cdec, tdec


def retention(hp, qk_blk, v_blk, g_blk, batch, seq, gn_g, tile=256):
    cos, sin, mask, rdec, cdec, tdec = retention_tables(seq, tile)
    nt = seq // tile
    qkw = 2 * RET_HEADS * RET_DK
    vw = RET_HEADS * RET_DV
    half = RET_HEADS * RET_DK // 2
    return pl.pallas_call(
        _ret_kernel,
        grid=(batch, nt),
        in_specs=[
            pl.BlockSpec((tile, qkw), lambda b, t: (b * nt + t, qk_blk)),
            pl.BlockSpec((tile, vw), lambda b, t: (b * nt + t, v_blk)),
            pl.BlockSpec((tile, vw), lambda b, t: (b * nt + t, g_blk)),
            pl.BlockSpec((tile, half), lambda b, t: (t, 0)),
            pl.BlockSpec((tile, half), lambda b, t: (t, 0)),
            _const_spec(mask.shape), _const_spec(rdec.shape), _const_spec(cdec.shape),
            _const_spec(tdec.shape), _const_spec((1, vw)),
        ],
        out_specs=pl.BlockSpec((tile, vw), lambda b, t: (b * nt + t, 0)),
        out_shape=jax.ShapeDtypeStruct((batch * seq, vw), BF16),
        scratch_shapes=[pltpu.VMEM((RET_HEADS, 2 * half, RET_DV), F32)],
        compiler_params=_params("parallel", "arbitrary"),
        name="retention",
    )(hp, hp, hp, cos, sin, mask, rdec, cdec, tdec, gn_g.reshape(1, vw))


def _first_max(vals, lane):
    m = jnp.max(vals, axis=-1, keepdims=True)
    idx = jnp.min(jnp.where(vals == m, lane, float(LANES)), axis=-1, keepdims=True)
    return m, idx


def _merge_kernel(ga_ref, gb_ref, gc_ref, ya_ref, yb_ref, yc_ref, h_ref, wa_ref, wb_ref, wc_ref, wo_ref,
                  g1_ref, b1_ref, rw_ref, rb_ref, h1_ref, info_ref, cnt_ref, run_ref):
    @pl.when(pl.program_id(0) == 0)
    def _():
        run_ref[...] = jnp.zeros_like(run_ref)

    def branch(g_ref, y_ref, w_ref):
        up = jnp.dot(y_ref[...], w_ref[...], preferred_element_type=F32)
        return jax.nn.sigmoid(g_ref[...].astype(F32)) * up

    merged = branch(ga_ref, ya_ref, wa_ref) + branch(gb_ref, yb_ref, wb_ref) + branch(gc_ref, yc_ref, wc_ref)
    y = jnp.dot(merged.astype(BF16), wo_ref[...], preferred_element_type=F32)
    h1 = _layer_norm(ALPHA * h_ref[...] + y, g1_ref[...], b1_ref[...])
    h1_ref[...] = h1

    logits = jnp.dot(h1, rw_ref[...], preferred_element_type=F32, precision=lax.Precision.HIGHEST) + rb_ref[...]
    tm = logits.shape[0]
    lane_i = lax.broadcasted_iota(jnp.int32, (tm, LANES), 1)
    lane = lane_i.astype(F32)
    ex = jnp.exp(logits - jnp.max(logits, axis=-1, keepdims=True))
    scores = ex / jnp.sum(ex, axis=-1, keepdims=True)
    group = (lane_i // EXPERTS_PER_GROUP).astype(F32)
    best = jnp.zeros((tm, 1), F32)
    best_score = jnp.full((tm, 1), -1.0, F32)
    for gi in range(N_EXPERTS // EXPERTS_PER_GROUP):
        sg = jnp.where(group == float(gi), scores, -1.0)
        m1, i1 = _first_max(sg, lane)
        m2 = jnp.max(jnp.where(lane == i1, -1.0, sg), axis=-1, keepdims=True)
        gs = m1 + m2
        better = gs > best_score
        best = jnp.where(better, float(gi), best)
        best_score = jnp.where(better, gs, best_score)
    masked = jnp.where(group == best, scores, -1.0)
    w1, e1 = _first_max(masked, lane)
    w2, e2 = _first_max(jnp.where(lane == e1, -2.0, masked), lane)
    den = w1 + w2
    gate1, gate2 = w1 / den, w2 / den

    sel1, sel2 = lane == e1, lane == e2
    onehot = jnp.where(sel1 | sel2, 1.0, 0.0)
    r = lax.broadcasted_iota(jnp.int32, (tm, tm), 0)
    c = lax.broadcasted_iota(jnp.int32, (tm, tm), 1)
    earlier = jnp.where(c < r, 1.0, 0.0).astype(BF16)
    rank = jnp.dot(earlier, onehot.astype(BF16), preferred_element_type=F32) + run_ref[0:1, :]
    rank1 = jnp.sum(jnp.where(sel1, rank, 0.0), axis=-1, keepdims=True)
    rank2 = jnp.sum(jnp.where(sel2, rank, 0.0), axis=-1, keepdims=True)
    run_ref[0:1, :] = run_ref[0:1, :] + jnp.sum(onehot, axis=0, keepdims=True)
    cnt_ref[...] = run_ref[...]

    info = jnp.zeros((tm, LANES), F32)
    for k, val in enumerate((e1, e2, gate1, gate2, rank1, rank2)):
        info = jnp.where(lane_i == k, val, info)
    info_ref[...] = info


def merge_norm_route(hp, ya, yb, yc, h, wa, wb, wc, wo, ln_g, ln_b, rw, rb, tm=256):
    T, D = h.shape
    row = lambda i: (i, 0)
    return pl.pallas_call(
        _merge_kernel,
        grid=(T // tm,),
        in_specs=[
            pl.BlockSpec((tm, D), lambda i: (i, 0)),
            pl.BlockSpec((tm, D), lambda i: (i, 1)),
            pl.BlockSpec((tm, D), lambda i: (i, 2)),
            pl.BlockSpec((tm, ya.shape[1]), row), pl.BlockSpec((tm, yb.shape[1]), row),
            pl.BlockSpec((tm, yc.shape[1]), row), pl.BlockSpec((tm, D), row),
            _const_spec(wa.shape), _const_spec(wb.shape), _const_spec(wc.shape), _const_spec(wo.shape),
            _const_spec((1, D)), _const_spec((1, D)), _const_spec(rw.shape), _const_spec(rb.shape),
        ],
        out_specs=[pl.BlockSpec((tm, D), row), pl.BlockSpec((tm, LANES), row),
                   pl.BlockSpec((SUBLANES, LANES), lambda i: (0, 0))],
        out_shape=[jax.ShapeDtypeStruct((T, D), F32), jax.ShapeDtypeStruct((T, LANES), F32),
                   jax.ShapeDtypeStruct((SUBLANES, LANES), F32)],
        scratch_shapes=[pltpu.VMEM((SUBLANES, LANES), F32)],
        compiler_params=_params("arbitrary"),
        name="merge_norm_route",
    )(hp, hp, hp, ya, yb, yc, h, wa, wb, wc, wo, ln_g.reshape(1, D), ln_b.reshape(1, D), rw, rb)


def _row_copy(src_ref, src_row, dst_ref, dst_row, sem):
    return pltpu.make_async_copy(src_ref.at[pl.ds(src_row, 1)], dst_ref.at[pl.ds(dst_row, 1)], sem)


def _dispatch_kernel(dest_ref, x_ref, zeros_ref, xs_ref, sem, *, n_tok):
    del zeros_ref
    tm = x_ref.shape[0]
    base = pl.program_id(0) * tm

    def copies(t):
        return (_row_copy(x_ref, t, xs_ref, dest_ref[base + t], sem),
                _row_copy(x_ref, t, xs_ref, dest_ref[n_tok + base + t], sem))

    def start(t, c):
        for cp in copies(t):
            cp.start()
        return c

    def wait(t, c):
        for cp in copies(t):
            cp.wait()
        return c

    lax.fori_loop(0, tm, start, 0)
    lax.fori_loop(0, tm, wait, 0)


def moe_dispatch(x, dest, n_rows, tm=512):
    T, D = x.shape
    zeros = jnp.zeros((n_rows, D), x.dtype)
    return pl.pallas_call(
        functools.partial(_dispatch_kernel, n_tok=T),
        grid_spec=pltpu.PrefetchScalarGridSpec(
            num_scalar_prefetch=1,
            grid=(T // tm,),
            in_specs=[pl.BlockSpec((tm, D), lambda i, d: (i, 0)), pl.BlockSpec(memory_space=pl.ANY)],
            out_specs=pl.BlockSpec(memory_space=pl.ANY),
            scratch_shapes=[pltpu.SemaphoreType.DMA(())],
        ),
        out_shape=jax.ShapeDtypeStruct((n_rows, D), x.dtype),
        input_output_aliases={2: 0},
        compiler_params=pltpu.CompilerParams(dimension_semantics=("arbitrary",), vmem_limit_bytes=VMEM_LIMIT,
                                             has_side_effects=True),
        name="moe_dispatch",
    )(dest, x, zeros)


def _expert_kernel(be_ref, nused_ref, xs_ref, w1_ref, w3_ref, w2_ref, ys_ref, w1b, w3b, w2b):
    i = pl.program_id(0)

    @pl.when(i < nused_ref[0])
    def _():
        prev = be_ref[jnp.maximum(i - 1, 0)]

        @pl.when((i == 0) | (be_ref[i] != prev))
        def _():
            w1b[...] = w1_ref[0].astype(BF16)
            w3b[...] = w3_ref[0].astype(BF16)
            w2b[...] = w2_ref[0].astype(BF16)

        x = xs_ref[...].astype(BF16)
        a = jnp.dot(x, w1b[...], preferred_element_type=F32)
        b = jnp.dot(x, w3b[...], preferred_element_type=F32)
        hb = (a * jax.nn.sigmoid(a) * b).astype(BF16)
        ys_ref[...] = jnp.dot(hb, w2b[...], preferred_element_type=F32)

    @pl.when(i >= nused_ref[0])
    def _():
        ys_ref[...] = jnp.zeros_like(ys_ref)


def moe_experts(xs, block_e, n_used, w1, w3, w2, blk):
    n_rows, D = xs.shape
    E, _, Fd = w1.shape
    nb = n_rows // blk
    wspec = lambda s: pl.BlockSpec((1,) + s, lambda i, be, nu: (be[i], 0, 0))
    return pl.pallas_call(
        _expert_kernel,
        grid_spec=pltpu.PrefetchScalarGridSpec(
            num_scalar_prefetch=2,
            grid=(nb,),
            in_specs=[pl.BlockSpec((blk, D), lambda i, be, nu: (i, 0)),
                      wspec((D, Fd)), wspec((D, Fd)), wspec((Fd, D))],
            out_specs=pl.BlockSpec((blk, D), lambda i, be, nu: (i, 0)),
            scratch_shapes=[pltpu.VMEM((D, Fd), BF16), pltpu.VMEM((D, Fd), BF16), pltpu.VMEM((Fd, D), BF16)],
        ),
        out_shape=jax.ShapeDtypeStruct((n_rows, D), F32),
        compiler_params=_params("arbitrary"),
        name="moe_experts",
    )(block_e, n_used, xs, w1, w3, w2)


def _combine_kernel(dest_ref, ys_ref, h_ref, info_ref, g_ref, b_ref, o_ref, y1_buf, y2_buf, sem, *, n_tok):
    tm = h_ref.shape[0]
    base = pl.program_id(0) * tm

    def copies(t):
        return (_row_copy(ys_ref, dest_ref[base + t], y1_buf, t, sem),
                _row_copy(ys_ref, dest_ref[n_tok + base + t], y2_buf, t, sem))

    def start(t, c):
        for cp in copies(t):
            cp.start()
        return c

    def wait(t, c):
        for cp in copies(t):
            cp.wait()
        return c

    lax.fori_loop(0, tm, start, 0)
    lax.fori_loop(0, tm, wait, 0)
    info = info_ref[...]
    y = info[:, 2:3] * y1_buf[...] + info[:, 3:4] * y2_buf[...]
    o_ref[...] = _layer_norm(ALPHA * h_ref[...] + y, g_ref[...], b_ref[...])


def moe_combine(ys, dest, h, info, ln_g, ln_b, tm=512):
    T, D = h.shape
    return pl.pallas_call(
        functools.partial(_combine_kernel, n_tok=T),
        grid_spec=pltpu.PrefetchScalarGridSpec(
            num_scalar_prefetch=1,
            grid=(T // tm,),
            in_specs=[pl.BlockSpec(memory_space=pl.ANY),
                      pl.BlockSpec((tm, D), lambda i, d: (i, 0)),
                      pl.BlockSpec((tm, LANES), lambda i, d: (i, 0)),
                      pl.BlockSpec((1, D), lambda i, d: (0, 0)),
                      pl.BlockSpec((1, D), lambda i, d: (0, 0))],
            out_specs=pl.BlockSpec((tm, D), lambda i, d: (i, 0)),
            scratch_shapes=[pltpu.VMEM((tm, D), ys.dtype), pltpu.VMEM((tm, D), ys.dtype),
                            pltpu.SemaphoreType.DMA(())],
        ),
        out_shape=jax.ShapeDtypeStruct((T, D), F32),
        compiler_params=_params("arbitrary"),
        name="moe_combine",
    )(dest, ys, h, info, ln_g.reshape(1, D), ln_b.reshape(1, D))


def route_tables(info, counts, blk, n_blocks):
    e = info[:, 0:2].astype(jnp.int32)
    rank = info[:, 4:6].astype(jnp.int32)
    cnt = counts[0, :N_EXPERTS].astype(jnp.int32)
    padded = (cnt + blk - 1) // blk * blk
    pends = jnp.cumsum(padded)
    pstarts = pends - padded
    dest = pstarts[e] + rank
    dest = jnp.concatenate([dest[:, 0], dest[:, 1]])
    block_e = jnp.searchsorted(pends, jnp.arange(n_blocks, dtype=jnp.int32) * blk, side='right')
    block_e = jnp.clip(block_e, 0, N_EXPERTS - 1).astype(jnp.int32)
    n_used = (pends[-1:] // blk).astype(jnp.int32)
    return dest, block_e, n_used


def grouped_moe_norm(h1, info, counts, w1, w3, w2, ln_g, ln_b, blk=512):
    T, D = h1.shape
    n_blocks = (2 * T) // blk + N_EXPERTS
    dest, block_e, n_used = route_tables(info, counts, blk, n_blocks)
    xs = moe_dispatch(h1, dest, n_blocks * blk)
    ys = moe_experts(xs, block_e, n_used, w1, w3, w2, blk)
    return moe_combine(ys, dest, h1, info, ln_g, ln_b)


def _in_proj_layout(d_model):
    s5w = d_model // 4
    sbw = SB_HEADS * SB_HEAD_DIM
    rqk = RET_HEADS * RET_DK
    rv = RET_HEADS * RET_DV
    sizes = (s5w, sbw, sbw, sbw, rqk, rqk, rv, rv, d_model, d_model, d_model)
    off = np.concatenate([[0], np.cumsum(sizes)])
    u_a, q_b, k_b, v_b, q_c, k_c, v_c, g_c, ga, gb, gc = [np.arange(off[i], off[i + 1]) for i in range(11)]

    def rope_perm(cols):
        c = cols.reshape(RET_HEADS, 2, RET_DK // 2)
        return np.concatenate([c[:, 0].reshape(-1), c[:, 1].reshape(-1)])

    order = np.concatenate([ga, gb, gc, rope_perm(q_c), rope_perm(k_c), v_c, g_c, q_b, k_b, v_b, u_a])
    scale = np.ones(off[-1], np.float32)
    scale[q_b] = SB_HEAD_DIM ** -0.5
    scale[k_c] = RET_DK ** -0.5
    o_qk = 3 * d_model
    o_v = o_qk + 2 * rqk
    o_g = o_v + rv
    o_sb = o_g + rv
    o_u = o_sb + 3 * sbw
    blocks = dict(qk=o_qk // (2 * rqk), v=o_v // rv, g=o_g // rv, sb_q=o_sb // LANES,
                  sb_k=(o_sb + sbw) // LANES, sb_v=(o_sb + 2 * sbw) // LANES, u=o_u // s5w)
    assert o_qk % (2 * rqk) == 0 and o_v % rv == 0 and o_g % rv == 0 and o_sb % LANES == 0 and o_u % s5w == 0
    return order, scale[order], blocks


def _layer(h, batch, seq, p):
    D = h.shape[1]
    order, scale, blk = _in_proj_layout(D)
    w_in = (p['w_in'][:, order] * scale).astype(BF16)
    hp = in_projection(h, w_in)
    s5_tile = 512
    bblk, lam, lamk, cblk = s5_tables(p['lam_re'], p['lam_im'], p['log_dt'], p['b_re'], p['b_im'],
                                      p['c_re'], p['c_im'], s5_tile // SUBLANES)
    ya = s5_mixer(hp, blk['u'], batch, seq, bblk, lam, lamk, cblk, p['d_skip'], p['w_glu'].astype(BF16),
                  p['b_glu'], tm=s5_tile)
    yb = stick_breaking(hp, blk['sb_q'], blk['sb_k'], blk['sb_v'], batch, seq)
    yc = retention(hp, blk['qk'], blk['v'], blk['g'], batch, seq, p['gn_g'])
    rw = jnp.zeros((D, LANES), F32).at[:, :N_EXPERTS].set(p['router_w'])
    rb = jnp.full((1, LANES), NEG_BIG, F32).at[0, :N_EXPERTS].set(p['router_b'])
    h1, info, counts = merge_norm_route(
        hp, ya, yb, yc, h, p['w_up_a'].astype(BF16), p['w_up_b'].astype(BF16), p['w_up_c'].astype(BF16),
        p['w_out'].astype(BF16), p['ln1_g'], p['ln1_b'], rw, rb)
    return grouped_moe_norm(h1, info, counts, p['moe_w1'], p['moe_w3'], p['moe_w2'], p['ln2_g'], p['ln2_b'])


def kernel(x, ln0_g, ln0_b, w_in, s5_lambda_re, s5_lambda_im, s5_log_dt, s5_b_re, s5_b_im, s5_c_re, s5_c_im,
           s5_d, s5_w_glu, s5_b_glu, ret_gn_g, w_up_a, w_up_b, w_up_c, w_out, ln1_g, ln1_b, router_w, router_b,
           moe_w1, moe_w3, moe_w2, ln2_g, ln2_b):
    batch, seq, D = x.shape
    h = layer_norm_rows(x.reshape(batch * seq, D), ln0_g, ln0_b)
    for l in range(w_in.shape[0]):
        p = dict(w_in=w_in[l], lam_re=s5_lambda_re[l], lam_im=s5_lambda_im[l], log_dt=s5_log_dt[l],
                 b_re=s5_b_re[l], b_im=s5_b_im[l], c_re=s5_c_re[l], c_im=s5_c_im[l], d_skip=s5_d[l],
                 w_glu=s5_w_glu[l], b_glu=s5_b_glu[l], gn_g=ret_gn_g[l], w_up_a=w_up_a[l], w_up_b=w_up_b[l],
                 w_up_c=w_up_c[l], w_out=w_out[l], ln1_g=ln1_g[l], ln1_b=ln1_b[l], router_w=router_w,
                 router_b=router_b, moe_w1=moe_w1[l], moe_w3=moe_w3[l], moe_w2=moe_w2[l], ln2_g=ln2_g[l],
                 ln2_b=ln2_b[l])
        h = _layer(h, batch, seq, p)
    return h.reshape(batch, seq, D)
```

```python
import functools
import math

import numpy as np
import jax
import jax.numpy as jnp
from jax import lax
from jax.experimental import pallas as pl
from jax.experimental.pallas import tpu as pltpu

F32 = jnp.float32
BF16 = jnp.bfloat16

LANES = 128
SUBLANES = 8
VMEM_LIMIT = 56 * 1024 * 1024

DEPTH = 2
CHUNK = 64
S5_GROUP_CH = 16
S5_STATE = 64
SB_HEADS = 4
SB_HEAD_DIM = 64
RET_HEADS = 4
RET_DK = 64
RET_DV = 128
ROPE_BASE = 10000.0
N_EXPERTS = 16
EXPERTS_PER_GROUP = 4
ALPHA = (2 * DEPTH) ** 0.25
LN_EPS = 1e-5

GATE_OFF = 0
NEG_BIG = -1e30


def _params(*sem):
    return pltpu.CompilerParams(dimension_semantics=sem, vmem_limit_bytes=VMEM_LIMIT)


def _const_spec(shape):
    zeros = (0,) * len(shape)
    return pl.BlockSpec(shape, lambda *_: zeros, pipeline_mode=pl.Buffered(1))


def _layer_norm(x, g, b):
    mu = jnp.mean(x, axis=-1, keepdims=True)
    xc = x - mu
    var = jnp.mean(xc * xc, axis=-1, keepdims=True)
    return xc * lax.rsqrt(var + LN_EPS) * g + b


def _ln_kernel(x_ref, g_ref, b_ref, o_ref):
    o_ref[...] = _layer_norm(x_ref[...], g_ref[...], b_ref[...])


def layer_norm_rows(x, g, b, tm=512):
    T, D = x.shape
    return pl.pallas_call(
        _ln_kernel,
        grid=(T // tm,),
        in_specs=[pl.BlockSpec((tm, D), lambda i: (i, 0)), _const_spec((1, D)), _const_spec((1, D))],
        out_specs=pl.BlockSpec((tm, D), lambda i: (i, 0)),
        out_shape=jax.ShapeDtypeStruct((T, D), F32),
        compiler_params=_params("parallel"),
        name="ln0",
    )(x, g.reshape(1, D), b.reshape(1, D))


def _inproj_kernel(x_ref, w_ref, o_ref, *, col_chunk):
    xb = x_ref[...].astype(BF16)
    n = w_ref.shape[1]
    for c in range(0, n, col_chunk):
        o_ref[:, c:c + col_chunk] = jnp.dot(
            xb, w_ref[:, c:c + col_chunk], preferred_element_type=F32).astype(BF16)


def in_projection(h, w_bf16, tm=512, col_chunk=512):
    T, D = h.shape
    N = w_bf16.shape[1]
    return pl.pallas_call(
        functools.partial(_inproj_kernel, col_chunk=col_chunk),
        grid=(T // tm,),
        in_specs=[pl.BlockSpec((tm, D), lambda i: (i, 0)), _const_spec((D, N))],
        out_specs=pl.BlockSpec((tm, N), lambda i: (i, 0)),
        out_shape=jax.ShapeDtypeStruct((T, N), BF16),
        compiler_params=_params("parallel"),
        name="in_proj",
    )(h, w_bf16)


def _s5_kernel(u_ref, bblk_ref, lam_ref, lamk_ref, cblk_ref, d_ref, wglu_ref, bglu_ref,
               o_ref, io_ref, s_ref, carry_ref, *, n_state):
    tm, width = u_ref.shape
    K = tm // SUBLANES
    ncol = n_state // LANES
    nio = width // LANES

    @pl.when(pl.program_id(1) == 0)
    def _():
        carry_ref[...] = jnp.zeros_like(carry_ref)

    def sub_chunk_rows(k):
        return pl.ds(k, SUBLANES, stride=K)

    for c in range(nio):
        io_ref[c] = u_ref[:, c * LANES:(c + 1) * LANES].astype(F32)
    u = jnp.concatenate(
        [jnp.concatenate([io_ref[c, sub_chunk_rows(k), :] for k in range(K)], axis=0) for c in range(nio)],
        axis=1)
    s_ref[...] = jnp.dot(u.astype(BF16), bblk_ref[...], preferred_element_type=F32)

    def cols(j):
        return pl.ds(j * LANES, LANES), pl.ds(n_state + j * LANES, LANES)

    def rows(k):
        return pl.ds(pl.multiple_of(k * SUBLANES, SUBLANES), SUBLANES)

    a_re = [jnp.broadcast_to(lam_ref[0:1, pl.ds(j * LANES, LANES)], (SUBLANES, LANES)) for j in range(ncol)]
    a_im = [jnp.broadcast_to(lam_ref[1:2, pl.ds(j * LANES, LANES)], (SUBLANES, LANES)) for j in range(ncol)]

    def pass1(k, st):
        out = []
        for j in range(ncol):
            cr, ci = cols(j)
            sr, si = st[2 * j], st[2 * j + 1]
            nr = a_re[j] * sr - a_im[j] * si + s_ref[rows(k), cr]
            ni = a_re[j] * si + a_im[j] * sr + s_ref[rows(k), ci]
            s_ref[rows(k), cr] = nr
            s_ref[rows(k), ci] = ni
            out += [nr, ni]
        return tuple(out)

    zero = jnp.zeros((SUBLANES, LANES), F32)
    ends = lax.fori_loop(0, K, pass1, (zero,) * (2 * ncol))

    carries = []
    for j in range(ncol):
        cr, ci = cols(j)
        kr = lamk_ref[0:1, pl.ds(j * LANES, LANES)]
        ki = lamk_ref[1:2, pl.ds(j * LANES, LANES)]
        er, ei = ends[2 * j], ends[2 * j + 1]
        c_r = [carry_ref[0:1, cr]]
        c_i = [carry_ref[0:1, ci]]
        for r in range(1, SUBLANES + 1):
            pr, pi = c_r[-1], c_i[-1]
            c_r.append(kr * pr - ki * pi + er[r - 1:r, :])
            c_i.append(kr * pi + ki * pr + ei[r - 1:r, :])
        carry_ref[0:1, cr] = c_r[SUBLANES]
        carry_ref[0:1, ci] = c_i[SUBLANES]
        carries += [jnp.concatenate(c_r[:SUBLANES], axis=0), jnp.concatenate(c_i[:SUBLANES], axis=0)]

    def pass2(k, st):
        out = []
        for j in range(ncol):
            cr, ci = cols(j)
            dr, di = st[2 * j], st[2 * j + 1]
            nr = a_re[j] * dr - a_im[j] * di
            ni = a_re[j] * di + a_im[j] * dr
            s_ref[rows(k), cr] = s_ref[rows(k), cr] + nr
            s_ref[rows(k), ci] = s_ref[rows(k), ci] + ni
            out += [nr, ni]
        return tuple(out)

    lax.fori_loop(0, K, pass2, tuple(carries))

    y = jnp.dot(s_ref[...].astype(BF16), cblk_ref[...], preferred_element_type=F32)
    y = y + d_ref[...] * u
    y = jax.nn.gelu(y, approximate=True)
    gate = jnp.dot(y.astype(BF16), wglu_ref[...], preferred_element_type=F32) + bglu_ref[...]
    out = y * jax.nn.sigmoid(gate)
    for c in range(nio):
        for k in range(K):
            io_ref[c, sub_chunk_rows(k), :] = out[k * SUBLANES:(k + 1) * SUBLANES, c * LANES:(c + 1) * LANES]
    o_ref[...] = jnp.concatenate([io_ref[c] for c in range(nio)], axis=1).astype(BF16)


def s5_mixer(hp, u_col_block, batch, seq, bblk, lam, lamk, cblk, d_skip, w_glu, b_glu, tm=512):
    width = bblk.shape[0]
    n_state = bblk.shape[1] // 2
    nt = seq // tm
    return pl.pallas_call(
        functools.partial(_s5_kernel, n_state=n_state),
        grid=(batch, nt),
        in_specs=[
            pl.BlockSpec((tm, width), lambda b, t: (b * nt + t, u_col_block)),
            _const_spec(bblk.shape), _const_spec(lam.shape), _const_spec(lamk.shape),
            _const_spec(cblk.shape), _const_spec((1, width)), _const_spec(w_glu.shape),
            _const_spec((1, width)),
        ],
        out_specs=pl.BlockSpec((tm, width), lambda b, t: (b * nt + t, 0)),
        out_shape=jax.ShapeDtypeStruct((batch * seq, width), BF16),
        scratch_shapes=[pltpu.VMEM((width // LANES, tm, LANES), F32),
                        pltpu.VMEM((tm, 2 * n_state), F32),
                        pltpu.VMEM((SUBLANES, 2 * n_state), F32)],
        compiler_params=_params("parallel", "arbitrary"),
        name="s5_mixer",
    )(hp, bblk, lam, lamk, cblk, d_skip.reshape(1, width), w_glu, b_glu.reshape(1, width))


def s5_tables(lam_re, lam_im, log_dt, b_re, b_im, c_re, c_im, sub_chunk):
    G, P = lam_re.shape
    dt = jnp.exp(log_dt.astype(F32))[:, None]

    def lam_pow(n):
        mag = jnp.exp(lam_re * dt * n)
        return mag * jnp.cos(lam_im * dt * n), mag * jnp.sin(lam_im * dt * n)

    lr, li = lam_pow(1.0)
    kr, ki = lam_pow(float(sub_chunk))
    nr, ni = lr - 1.0, li
    den = lam_re * lam_re + lam_im * lam_im
    zr = (nr * lam_re + ni * lam_im) / den
    zi = (ni * lam_re - nr * lam_im) / den
    bbr = b_re * zr[..., None] - b_im * zi[..., None]
    bbi = b_re * zi[..., None] + b_im * zr[..., None]
    eye = jnp.eye(G, dtype=F32)
    Cg = b_re.shape[-1]

    def in_blk(m):
        return jnp.einsum('gpc,gh->gchp', m, eye).reshape(G * Cg, G * P)

    def out_blk(m):
        return jnp.einsum('gcp,gh->gphc', m, eye).reshape(G * P, G * Cg)

    bblk = jnp.concatenate([in_blk(bbr), in_blk(bbi)], axis=1).astype(BF16)
    cblk = jnp.concatenate([out_blk(c_re), -out_blk(c_im)], axis=0).astype(BF16)
    lam = jnp.stack([lr.reshape(-1), li.reshape(-1)])
    lamk = jnp.stack([kr.reshape(-1), ki.reshape(-1)])
    return bblk, lam, lamk, cblk


def _log_sigmoid_pair(z):
    soft = jnp.log(1.0 + jnp.exp(-jnp.abs(z)))
    lp = jnp.minimum(z, 0.0) - soft
    return lp, lp - z


SB_EXIT = -110.0


def _sb_kernel(q_ref, k_ref, v_ref, tri_ref, o_ref, acc_ref, carry_ref, *, blk, group):
    i = pl.program_id(2)
    q = q_ref[...]
    lane = lax.broadcasted_iota(jnp.int32, (1, LANES), 1)
    head_lanes = [lane < SB_HEAD_DIM, lane >= SB_HEAD_DIM]
    zero = jnp.zeros((), BF16)
    qh = [jnp.where(m, q, zero) for m in head_lanes]
    row = lax.broadcasted_iota(jnp.int32, (blk, blk), 0)
    col = lax.broadcasted_iota(jnp.int32, (blk, blk), 1)
    strictly_earlier = col < row

    acc_ref[...] = jnp.zeros_like(acc_ref)
    carry_ref[...] = jnp.zeros_like(carry_ref)

    def group_step(g, first):
        zs, lps, rs, vs = [], [], [], []
        for u in range(group):
            j = i - g * group - u
            start = pl.multiple_of(jnp.maximum(j, 0) * blk, blk)
            kj = k_ref[pl.ds(start, blk), :]
            vj = v_ref[pl.ds(start, blk), :]
            for h in range(2):
                zs.append(lax.dot_general(qh[h], kj, (((1,), (1,)), ((), ())), preferred_element_type=F32))
                vs.append(jnp.where(head_lanes[h] & (j >= 0), vj, zero))
        for n, z in enumerate(zs):
            lp, l1m = _log_sigmoid_pair(z)
            if first and n < 2:
                l1m = jnp.where(strictly_earlier, l1m, 0.0)
            lps.append(lp)
            rs.append(jnp.dot(l1m.astype(BF16), tri_ref[...], preferred_element_type=F32))
        ws = [None] * len(zs)
        top = None
        for h in range(2):
            cum = carry_ref[h]
            for u in range(group):
                n = 2 * u + h
                w = jnp.exp(lps[n] + rs[n][:, :blk] + cum)
                if first and u == 0:
                    w = jnp.where(strictly_earlier, w, 0.0)
                ws[n] = w.astype(BF16)
                cum = cum + rs[n][:, blk:]
            carry_ref[h] = cum
            top = cum if top is None else jnp.maximum(top, cum)
        acc = acc_ref[...]
        for w, v in zip(ws, vs):
            acc = acc + jnp.dot(w, v, preferred_element_type=F32)
        acc_ref[...] = acc
        return jnp.max(top)

    def more(state):
        g, top = state
        return (g * group <= i) & (top > SB_EXIT)

    def step(state):
        g, _ = state
        return g + 1, group_step(g, False)

    lax.while_loop(more, step, (1, group_step(0, True)))
    o_ref[...] = acc_ref[...].astype(BF16)


def stick_breaking(hp, q_blk0, k_blk0, v_blk0, batch, seq, blk=128, group=4):
    assert blk == LANES
    nq = seq // blk
    n_pairs = SB_HEADS * SB_HEAD_DIM // LANES
    r = np.arange(blk)
    tri = np.concatenate([(r[:, None] > r[None, :]), np.ones((blk, blk), bool)], axis=1)
    tri = jnp.asarray(tri, BF16)
    return pl.pallas_call(
        functools.partial(_sb_kernel, blk=blk, group=group),
        grid=(batch, n_pairs, nq),
        in_specs=[
            pl.BlockSpec((blk, LANES), lambda b, p, i: (b * nq + i, q_blk0 + p)),
            pl.BlockSpec((seq, LANES), lambda b, p, i: (b, k_blk0 + p)),
            pl.BlockSpec((seq, LANES), lambda b, p, i: (b, v_blk0 + p)),
            _const_spec(tri.shape),
        ],
        out_specs=pl.BlockSpec((blk, LANES), lambda b, p, i: (b * nq + i, p)),
        out_shape=jax.ShapeDtypeStruct((batch * seq, n_pairs * LANES), BF16),
        scratch_shapes=[pltpu.VMEM((blk, LANES), F32), pltpu.VMEM((2, blk, blk), F32)],
        compiler_params=_params("parallel", "parallel", "arbitrary"),
        name="stick_breaking",
    )(hp, hp, hp, tri)


def _ret_kernel(qk_ref, v_ref, g_ref, cos_ref, sin_ref, mask_ref, rdec_ref, cdec_ref, tdec_ref, gn_ref,
                o_ref, state_ref):
    @pl.when(pl.program_id(1) == 0)
    def _():
        state_ref[...] = jnp.zeros_like(state_ref)

    half = RET_HEADS * RET_DK // 2
    qk = qk_ref[...].astype(F32)
    cos, sin = cos_ref[...], sin_ref[...]

    def rope(t):
        t1, t2 = t[:, :half], t[:, half:]
        return jnp.concatenate([t1 * cos - t2 * sin, t1 * sin + t2 * cos], axis=1)

    qr = rope(qk[:, :2 * half])
    kr = rope(qk[:, 2 * half:])
    kb = kr.astype(BF16)
    lane = lax.broadcasted_iota(jnp.int32, (1, 2 * half), 1)
    head_of_lane = (lane % half) // (RET_DK // 2)
    for h in range(RET_HEADS):
        qh = jnp.where(head_of_lane == h, qr, 0.0)
        vh = v_ref[:, h * RET_DV:(h + 1) * RET_DV]
        scores = lax.dot_general(qh.astype(BF16), kb, (((1,), (1,)), ((), ())), preferred_element_type=F32)
        scores = scores * mask_ref[h]
        o = jnp.dot(scores.astype(BF16), vh, preferred_element_type=F32)
        o = o + jnp.dot((qh * rdec_ref[h]).astype(BF16), state_ref[h].astype(BF16), preferred_element_type=F32)
        kv = lax.dot_general((kr * cdec_ref[h]).astype(BF16), vh, (((0,), (0,)), ((), ())),
                             preferred_element_type=F32)
        state_ref[h] = tdec_ref[h] * state_ref[h] + kv
        mu = jnp.mean(o, axis=-1, keepdims=True)
        oc = o - mu
        var = jnp.mean(oc * oc, axis=-1, keepdims=True)
        on = oc * lax.rsqrt(var + LN_EPS) * gn_ref[:, h * RET_DV:(h + 1) * RET_DV]
        g = g_ref[:, h * RET_DV:(h + 1) * RET_DV].astype(F32)
        o_ref[:, h * RET_DV:(h + 1) * RET_DV] = (g * jax.nn.sigmoid(g) * on).astype(BF16)


def retention_tables(seq, tile):
    halfdim = RET_DK // 2
    inv_freq = ROPE_BASE ** (-jnp.arange(halfdim, dtype=F32) / halfdim)
    ang = jnp.arange(seq, dtype=F32)[:, None] * inv_freq[None, :]
    cos = jnp.tile(jnp.cos(ang), (1, RET_HEADS))
    sin = jnp.tile(jnp.sin(ang), (1, RET_HEADS))
    log_gamma = jnp.log(1.0 - 2.0 ** (-5.0 - jnp.arange(RET_HEADS, dtype=F32)))
    t = jnp.arange(tile, dtype=F32)
    same_or_earlier_chunk = (t[None, :] // CHUNK) <= (t[:, None] // CHUNK)
    mask = jnp.exp(log_gamma[:, None, None] * jnp.abs(t[:, None] - t[None, :]))
    mask = jnp.where(same_or_earlier_chunk[None], mask, 0.0)
    rdec = jnp.exp(log_gamma[:, None, None] * (t[None, :, None] + 1.0))
    cdec = jnp.exp(log_gamma[:, None, None] * (tile - 1.0 - t[None, :, None]))
    tdec = jnp.broadcast_to(jnp.exp(log_gamma * tile)[:, None, None], (RET_HEADS, 1, RET_DV))
    return cos, sin, mask, rdec, cdec, tdec


def retention(hp, qk_blk, v_blk, g_blk, batch, seq, gn_g, tile=256):
    cos, sin, mask, rdec, cdec, tdec = retention_tables(seq, tile)
    nt = seq // tile
    qkw = 2 * RET_HEADS * RET_DK
    vw = RET_HEADS * RET_DV
    half = RET_HEADS * RET_DK // 2
    return pl.pallas_call(
        _ret_kernel,
        grid=(batch, nt),
        in_specs=[
            pl.BlockSpec((tile, qkw), lambda b, t: (b * nt + t, qk_blk)),
            pl.BlockSpec((tile, vw), lambda b, t: (b * nt + t, v_blk)),
            pl.BlockSpec((tile, vw), lambda b, t: (b * nt + t, g_blk)),
            pl.BlockSpec((tile, half), lambda b, t: (t, 0)),
            pl.BlockSpec((tile, half), lambda b, t: (t, 0)),
            _const_spec(mask.shape), _const_spec(rdec.shape), _const_spec(cdec.shape),
            _const_spec(tdec.shape), _const_spec((1, vw)),
        ],
        out_specs=pl.BlockSpec((tile, vw), lambda b, t: (b * nt + t, 0)),
        out_shape=jax.ShapeDtypeStruct((batch * seq, vw), BF16),
        scratch_shapes=[pltpu.VMEM((RET_HEADS, 2 * half, RET_DV), F32)],
        compiler_params=_params("parallel", "arbitrary"),
        name="retention",
    )(hp, hp, hp, cos, sin, mask, rdec, cdec, tdec, gn_g.reshape(1, vw))


def _first_max(vals, lane):
    m = jnp.max(vals, axis=-1, keepdims=True)
    idx = jnp.min(jnp.where(vals == m, lane, float(LANES)), axis=-1, keepdims=True)
    return m, idx


def _merge_kernel(ga_ref, gb_ref, gc_ref, ya_ref, yb_ref, yc_ref, h_ref, wa_ref, wb_ref, wc_ref, wo_ref,
                  g1_ref, b1_ref, rw_ref, rb_ref, h1_ref, info_ref, cnt_ref, run_ref):
    @pl.when(pl.program_id(0) == 0)
    def _():
        run_ref[...] = jnp.zeros_like(run_ref)

    def branch(g_ref, y_ref, w_ref):
        up = jnp.dot(y_ref[...], w_ref[...], preferred_element_type=F32)
        return jax.nn.sigmoid(g_ref[...].astype(F32)) * up

    merged = branch(ga_ref, ya_ref, wa_ref) + branch(gb_ref, yb_ref, wb_ref) + branch(gc_ref, yc_ref, wc_ref)
    y = jnp.dot(merged.astype(BF16), wo_ref[...], preferred_element_type=F32)
    h1 = _layer_norm(ALPHA * h_ref[...] + y, g1_ref[...], b1_ref[...])
    h1_ref[...] = h1

    logits = jnp.dot(h1, rw_ref[...], preferred_element_type=F32, precision=lax.Precision.HIGHEST) + rb_ref[...]
    tm = logits.shape[0]
    lane_i = lax.broadcasted_iota(jnp.int32, (tm, LANES), 1)
    lane = lane_i.astype(F32)
    ex = jnp.exp(logits - jnp.max(logits, axis=-1, keepdims=True))
    scores = ex / jnp.sum(ex, axis=-1, keepdims=True)
    group = (lane_i // EXPERTS_PER_GROUP).astype(F32)
    best = jnp.zeros((tm, 1), F32)
    best_score = jnp.full((tm, 1), -1.0, F32)
    for gi in range(N_EXPERTS // EXPERTS_PER_GROUP):
        sg = jnp.where(group == float(gi), scores, -1.0)
        m1, i1 = _first_max(sg, lane)
        m2 = jnp.max(jnp.where(lane == i1, -1.0, sg), axis=-1, keepdims=True)
        gs = m1 + m2
        better = gs > best_score
        best = jnp.where(better, float(gi), best)
        best_score = jnp.where(better, gs, best_score)
    masked = jnp.where(group == best, scores, -1.0)
    w1, e1 = _first_max(masked, lane)
    w2, e2 = _first_max(jnp.where(lane == e1, -2.0, masked), lane)
    den = w1 + w2
    gate1, gate2 = w1 / den, w2 / den

    sel1, sel2 = lane == e1, lane == e2
    onehot = jnp.where(sel1 | sel2, 1.0, 0.0)
    r = lax.broadcasted_iota(jnp.int32, (tm, tm), 0)
    c = lax.broadcasted_iota(jnp.int32, (tm, tm), 1)
    earlier = jnp.where(c < r, 1.0, 0.0).astype(BF16)
    rank = jnp.dot(earlier, onehot.astype(BF16), preferred_element_type=F32) + run_ref[0:1, :]
    rank1 = jnp.sum(jnp.where(sel1, rank, 0.0), axis=-1, keepdims=True)
    rank2 = jnp.sum(jnp.where(sel2, rank, 0.0), axis=-1, keepdims=True)
    run_ref[0:1, :] = run_ref[0:1, :] + jnp.sum(onehot, axis=0, keepdims=True)
    cnt_ref[...] = run_ref[...]

    info = jnp.zeros((tm, LANES), F32)
    for k, val in enumerate((e1, e2, gate1, gate2, rank1, rank2)):
        info = jnp.where(lane_i == k, val, info)
    info_ref[...] = info


def merge_norm_route(hp, ya, yb, yc, h, wa, wb, wc, wo, ln_g, ln_b, rw, rb, tm=256):
    T, D = h.shape
    row = lambda i: (i, 0)
    return pl.pallas_call(
        _merge_kernel,
        grid=(T // tm,),
        in_specs=[
            pl.BlockSpec((tm, D), lambda i: (i, 0)),
            pl.BlockSpec((tm, D), lambda i: (i, 1)),
            pl.BlockSpec((tm, D), lambda i: (i, 2)),
            pl.BlockSpec((tm, ya.shape[1]), row), pl.BlockSpec((tm, yb.shape[1]), row),
            pl.BlockSpec((tm, yc.shape[1]), row), pl.BlockSpec((tm, D), row),
            _const_spec(wa.shape), _const_spec(wb.shape), _const_spec(wc.shape), _const_spec(wo.shape),
            _const_spec((1, D)), _const_spec((1, D)), _const_spec(rw.shape), _const_spec(rb.shape),
        ],
        out_specs=[pl.BlockSpec((tm, D), row), pl.BlockSpec((tm, LANES), row),
                   pl.BlockSpec((SUBLANES, LANES), lambda i: (0, 0))],
        out_shape=[jax.ShapeDtypeStruct((T, D), F32), jax.ShapeDtypeStruct((T, LANES), F32),
                   jax.ShapeDtypeStruct((SUBLANES, LANES), F32)],
        scratch_shapes=[pltpu.VMEM((SUBLANES, LANES), F32)],
        compiler_params=_params("arbitrary"),
        name="merge_norm_route",
    )(hp, hp, hp, ya, yb, yc, h, wa, wb, wc, wo, ln_g.reshape(1, D), ln_b.reshape(1, D), rw, rb)


def _row_copy(src_ref, src_row, dst_ref, dst_row, sem):
    return pltpu.make_async_copy(src_ref.at[pl.ds(src_row, 1)], dst_ref.at[pl.ds(dst_row, 1)], sem)


def _dispatch_kernel(dest_ref, x_ref, zeros_ref, xs_ref, sem, *, n_tok):
    del zeros_ref
    tm = x_ref.shape[0]
    base = pl.program_id(0) * tm

    def copies(t):
        return (_row_copy(x_ref, t, xs_ref, dest_ref[base + t], sem),
                _row_copy(x_ref, t, xs_ref, dest_ref[n_tok + base + t], sem))

    def start(t, c):
        for cp in copies(t):
            cp.start()
        return c

    def wait(t, c):
        for cp in copies(t):
            cp.wait()
        return c

    lax.fori_loop(0, tm, start, 0)
    lax.fori_loop(0, tm, wait, 0)


def moe_dispatch(x, dest, n_rows, tm=512):
    T, D = x.shape
    zeros = jnp.zeros((n_rows, D), x.dtype)
    return pl.pallas_call(
        functools.partial(_dispatch_kernel, n_tok=T),
        grid_spec=pltpu.PrefetchScalarGridSpec(
            num_scalar_prefetch=1,
            grid=(T // tm,),
            in_specs=[pl.BlockSpec((tm, D), lambda i, d: (i, 0)), pl.BlockSpec(memory_space=pl.ANY)],
            out_specs=pl.BlockSpec(memory_space=pl.ANY),
            scratch_shapes=[pltpu.SemaphoreType.DMA(())],
        ),
        out_shape=jax.ShapeDtypeStruct((n_rows, D), x.dtype),
        input_output_aliases={2: 0},
        compiler_params=pltpu.CompilerParams(dimension_semantics=("arbitrary",), vmem_limit_bytes=VMEM_LIMIT,
                                             has_side_effects=True),
        name="moe_dispatch",
    )(dest, x, zeros)


def _expert_kernel(be_ref, nused_ref, xs_ref, w1_ref, w3_ref, w2_ref, ys_ref, w1b, w3b, w2b):
    i = pl.program_id(0)

    @pl.when(i < nused_ref[0])
    def _():
        prev = be_ref[jnp.maximum(i - 1, 0)]

        @pl.when((i == 0) | (be_ref[i] != prev))
        def _():
            w1b[...] = w1_ref[0].astype(BF16)
            w3b[...] = w3_ref[0].astype(BF16)
            w2b[...] = w2_ref[0].astype(BF16)

        x = xs_ref[...].astype(BF16)
        a = jnp.dot(x, w1b[...], preferred_element_type=F32)
        b = jnp.dot(x, w3b[...], preferred_element_type=F32)
        hb = (a * jax.nn.sigmoid(a) * b).astype(BF16)
        ys_ref[...] = jnp.dot(hb, w2b[...], preferred_element_type=F32)

    @pl.when(i >= nused_ref[0])
    def _():
        ys_ref[...] = jnp.zeros_like(ys_ref)


def moe_experts(xs, block_e, n_used, w1, w3, w2, blk):
    n_rows, D = xs.shape
    E, _, Fd = w1.shape
    nb = n_rows // blk
    wspec = lambda s: pl.BlockSpec((1,) + s, lambda i, be, nu: (be[i], 0, 0))
    return pl.pallas_call(
        _expert_kernel,
        grid_spec=pltpu.PrefetchScalarGridSpec(
            num_scalar_prefetch=2,
            grid=(nb,),
            in_specs=[pl.BlockSpec((blk, D), lambda i, be, nu: (i, 0)),
                      wspec((D, Fd)), wspec((D, Fd)), wspec((Fd, D))],
            out_specs=pl.BlockSpec((blk, D), lambda i, be, nu: (i, 0)),
            scratch_shapes=[pltpu.VMEM((D, Fd), BF16), pltpu.VMEM((D, Fd), BF16), pltpu.VMEM((Fd, D), BF16)],
        ),
        out_shape=jax.ShapeDtypeStruct((n_rows, D), F32),
        compiler_params=_params("arbitrary"),
        name="moe_experts",
    )(block_e, n_used, xs, w1, w3, w2)


def _combine_kernel(dest_ref, ys_ref, h_ref, info_ref, g_ref, b_ref, o_ref, y1_buf, y2_buf, sem, *, n_tok):
    tm = h_ref.shape[0]
    base = pl.program_id(0) * tm

    def copies(t):
        return (_row_copy(ys_ref, dest_ref[base + t], y1_buf, t, sem),
                _row_copy(ys_ref, dest_ref[n_tok + base + t], y2_buf, t, sem))

    def start(t, c):
        for cp in copies(t):
            cp.start()
        return c

    def wait(t, c):
        for cp in copies(t):
            cp.wait()
        return c

    lax.fori_loop(0, tm, start, 0)
    lax.fori_loop(0, tm, wait, 0)
    info = info_ref[...]
    y = info[:, 2:3] * y1_buf[...] + info[:, 3:4] * y2_buf[...]
    o_ref[...] = _layer_norm(ALPHA * h_ref[...] + y, g_ref[...], b_ref[...])


def moe_combine(ys, dest, h, info, ln_g, ln_b, tm=512):
    T, D = h.shape
    return pl.pallas_call(
        functools.partial(_combine_kernel, n_tok=T),
        grid_spec=pltpu.PrefetchScalarGridSpec(
            num_scalar_prefetch=1,
            grid=(T // tm,),
            in_specs=[pl.BlockSpec(memory_space=pl.ANY),
                      pl.BlockSpec((tm, D), lambda i, d: (i, 0)),
                      pl.BlockSpec((tm, LANES), lambda i, d: (i, 0)),
                      pl.BlockSpec((1, D), lambda i, d: (0, 0)),
                      pl.BlockSpec((1, D), lambda i, d: (0, 0))],
            out_specs=pl.BlockSpec((tm, D), lambda i, d: (i, 0)),
            scratch_shapes=[pltpu.VMEM((tm, D), ys.dtype), pltpu.VMEM((tm, D), ys.dtype),
                            pltpu.SemaphoreType.DMA(())],
        ),
        out_shape=jax.ShapeDtypeStruct((T, D), F32),
        compiler_params=_params("arbitrary"),
        name="moe_combine",
    )(dest, ys, h, info, ln_g.reshape(1, D), ln_b.reshape(1, D))


def route_tables(info, counts, blk, n_blocks):
    e = info[:, 0:2].astype(jnp.int32)
    rank = info[:, 4:6].astype(jnp.int32)
    cnt = counts[0, :N_EXPERTS].astype(jnp.int32)
    padded = (cnt + blk - 1) // blk * blk
    pends = jnp.cumsum(padded)
    pstarts = pends - padded
    dest = pstarts[e] + rank
    dest = jnp.concatenate([dest[:, 0], dest[:, 1]])
    block_e = jnp.searchsorted(pends, jnp.arange(n_blocks, dtype=jnp.int32) * blk, side='right')
    block_e = jnp.clip(block_e, 0, N_EXPERTS - 1).astype(jnp.int32)
    n_used = (pends[-1:] // blk).astype(jnp.int32)
    return dest, block_e, n_used


def grouped_moe_norm(h1, info, counts, w1, w3, w2, ln_g, ln_b, blk=512):
    T, D = h1.shape
    n_blocks = (2 * T) // blk + N_EXPERTS
    dest, block_e, n_used = route_tables(info, counts, blk, n_blocks)
    xs = moe_dispatch(h1, dest, n_blocks * blk)
    ys = moe_experts(xs, block_e, n_used, w1, w3, w2, blk)
    return moe_combine(ys, dest, h1, info, ln_g, ln_b)


def _in_proj_layout(d_model):
    s5w = d_model // 4
    sbw = SB_HEADS * SB_HEAD_DIM
    rqk = RET_HEADS * RET_DK
    rv = RET_HEADS * RET_DV
    sizes = (s5w, sbw, sbw, sbw, rqk, rqk, rv, rv, d_model, d_model, d_model)
    off = np.concatenate([[0], np.cumsum(sizes)])
    u_a, q_b, k_b, v_b, q_c, k_c, v_c, g_c, ga, gb, gc = [np.arange(off[i], off[i + 1]) for i in range(11)]

    def rope_perm(cols):
        c = cols.reshape(RET_HEADS, 2, RET_DK // 2)
        return np.concatenate([c[:, 0].reshape(-1), c[:, 1].reshape(-1)])

    order = np.concatenate([ga, gb, gc, rope_perm(q_c), rope_perm(k_c), v_c, g_c, q_b, k_b, v_b, u_a])
    scale = np.ones(off[-1], np.float32)
    scale[q_b] = SB_HEAD_DIM ** -0.5
    scale[k_c] = RET_DK ** -0.5
    o_qk = 3 * d_model
    o_v = o_qk + 2 * rqk
    o_g = o_v + rv
    o_sb = o_g + rv
    o_u = o_sb + 3 * sbw
    blocks = dict(qk=o_qk // (2 * rqk), v=o_v // rv, g=o_g // rv, sb_q=o_sb // LANES,
                  sb_k=(o_sb + sbw) // LANES, sb_v=(o_sb + 2 * sbw) // LANES, u=o_u // s5w)
    assert o_qk % (2 * rqk) == 0 and o_v % rv == 0 and o_g % rv == 0 and o_sb % LANES == 0 and o_u % s5w == 0
    return order, scale[order], blocks


def _layer(h, batch, seq, p):
    D = h.shape[1]
    order, scale, blk = _in_proj_layout(D)
    w_in = (p['w_in'][:, order] * scale).astype(BF16)
    hp = in_projection(h, w_in)
    s5_tile = 512
    bblk, lam, lamk, cblk = s5_tables(p['lam_re'], p['lam_im'], p['log_dt'], p['b_re'], p['b_im'],
                                      p['c_re'], p['c_im'], s5_tile // SUBLANES)
    ya = s5_mixer(hp, blk['u'], batch, seq, bblk, lam, lamk, cblk, p['d_skip'], p['w_glu'].astype(BF16),
                  p['b_glu'], tm=s5_tile)
    yb = stick_breaking(hp, blk['sb_q'], blk['sb_k'], blk['sb_v'], batch, seq)
    yc = retention(hp, blk['qk'], blk['v'], blk['g'], batch, seq, p['gn_g'])
    rw = jnp.zeros((D, LANES), F32).at[:, :N_EXPERTS].set(p['router_w'])
    rb = jnp.full((1, LANES), NEG_BIG, F32).at[0, :N_EXPERTS].set(p['router_b'])
    h1, info, counts = merge_norm_route(
        hp, ya, yb, yc, h, p['w_up_a'].astype(BF16), p['w_up_b'].astype(BF16), p['w_up_c'].astype(BF16),
        p['w_out'].astype(BF16), p['ln1_g'], p['ln1_b'], rw, rb)
    return grouped_moe_norm(h1, info, counts, p['moe_w1'], p['moe_w3'], p['moe_w2'], p['ln2_g'], p['ln2_b'])


def kernel(x, ln0_g, ln0_b, w_in, s5_lambda_re, s5_lambda_im, s5_log_dt, s5_b_re, s5_b_im, s5_c_re, s5_c_im,
           s5_d, s5_w_glu, s5_b_glu, ret_gn_g, w_up_a, w_up_b, w_up_c, w_out, ln1_g, ln1_b, router_w, router_b,
           moe_w1, moe_w3, moe_w2, ln2_g, ln2_b):
    batch, seq, D = x.shape
    h = layer_norm_rows(x.reshape(batch * seq, D), ln0_g, ln0_b)
    for l in range(w_in.shape[0]):
        p = dict(w_in=w_in[l], lam_re=s5_lambda_re[l], lam_im=s5_lambda_im[l], log_dt=s5_log_dt[l],
                 b_re=s5_b_re[l], b_im=s5_b_im[l], c_re=s5_c_re[l], c_im=s5_c_im[l], d_skip=s5_d[l],
                 w_glu=s5_w_glu[l], b_glu=s5_b_glu[l], gn_g=ret_gn_g[l], w_up_a=w_up_a[l], w_up_b=w_up_b[l],
                 w_up_c=w_up_c[l], w_out=w_out[l], ln1_g=ln1_g[l], ln1_b=ln1_b[l], router_w=router_w,
                 router_b=router_b, moe_w1=moe_w1[l], moe_w3=moe_w3[l], moe_w2=moe_w2[l], ln2_g=ln2_g[l],
                 ln2_b=ln2_b[l])
        h = _layer(h, batch, seq, p)
    return h.reshape(batch, seq, D)
```

```python
import functools
import math

import numpy as np
import jax
import jax.numpy as jnp
from jax import lax
from jax.experimental import pallas as pl
from jax.experimental.pallas import tpu as pltpu

F32 = jnp.float32
BF16 = jnp.bfloat16

LANES = 128
SUBLANES = 8
VMEM_LIMIT = 56 * 1024 * 1024

DEPTH = 2
CHUNK = 64
S5_GROUP_CH = 16
S5_STATE = 64
SB_HEADS = 4
SB_HEAD_DIM = 64
RET_HEADS = 4
RET_DK = 64
RET_DV = 128
ROPE_BASE = 10000.0
N_EXPERTS = 16
EXPERTS_PER_GROUP = 4
ALPHA = (2 * DEPTH) ** 0.25
LN_EPS = 1e-5

GATE_OFF = 0
NEG_BIG = -1e30


def _params(*sem):
    return pltpu.CompilerParams(dimension_semantics=sem, vmem_limit_bytes=VMEM_LIMIT)


def _const_spec(shape):
    zeros = (0,) * len(shape)
    return pl.BlockSpec(shape, lambda *_: zeros, pipeline_mode=pl.Buffered(1))


def _layer_norm(x, g, b):
    mu = jnp.mean(x, axis=-1, keepdims=True)
    xc = x - mu
    var = jnp.mean(xc * xc, axis=-1, keepdims=True)
    return xc * lax.rsqrt(var + LN_EPS) * g + b


def _ln_kernel(x_ref, g_ref, b_ref, o_ref):
    o_ref[...] = _layer_norm(x_ref[...], g_ref[...], b_ref[...])


def layer_norm_rows(x, g, b, tm=512):
    T, D = x.shape
    return pl.pallas_call(
        _ln_kernel,
        grid=(T // tm,),
        in_specs=[pl.BlockSpec((tm, D), lambda i: (i, 0)), _const_spec((1, D)), _const_spec((1, D))],
        out_specs=pl.BlockSpec((tm, D), lambda i: (i, 0)),
        out_shape=jax.ShapeDtypeStruct((T, D), F32),
        compiler_params=_params("parallel"),
        name="ln0",
    )(x, g.reshape(1, D), b.reshape(1, D))


def _inproj_kernel(x_ref, w_ref, o_ref, *, col_chunk, n_gate_cols):
    xb = x_ref[...].astype(BF16)
    n = w_ref.shape[1]
    for c in range(0, n, col_chunk):
        acc = jnp.dot(xb, w_ref[:, c:c + col_chunk], preferred_element_type=F32)
        if c < n_gate_cols:
            acc = jax.nn.sigmoid(acc)
        o_ref[:, c:c + col_chunk] = acc.astype(BF16)


def in_projection(h, w_bf16, n_gate_cols, tm=512, col_chunk=512):
    T, D = h.shape
    N = w_bf16.shape[1]
    assert n_gate_cols % col_chunk == 0
    return pl.pallas_call(
        functools.partial(_inproj_kernel, col_chunk=col_chunk, n_gate_cols=n_gate_cols),
        grid=(T // tm,),
        in_specs=[pl.BlockSpec((tm, D), lambda i: (i, 0)), _const_spec((D, N))],
        out_specs=pl.BlockSpec((tm, N), lambda i: (i, 0)),
        out_shape=jax.ShapeDtypeStruct((T, N), BF16),
        compiler_params=_params("parallel"),
        name="in_proj",
    )(h, w_bf16)


def _s5_kernel(u_ref, bblk_ref, lam_ref, lamk_ref, cblk_ref, d_ref, wglu_ref, bglu_ref,
               o_ref, io_ref, s_ref, carry_ref, *, n_state):
    tm, width = u_ref.shape
    K = tm // SUBLANES
    ncol = n_state // LANES
    nio = width // LANES

    @pl.when(pl.program_id(1) == 0)
    def _():
        carry_ref[...] = jnp.zeros_like(carry_ref)

    def sub_chunk_rows(k):
        return pl.ds(k, SUBLANES, stride=K)

    for c in range(nio):
        io_ref[c] = u_ref[:, c * LANES:(c + 1) * LANES].astype(F32)
    u = jnp.concatenate(
        [jnp.concatenate([io_ref[c, sub_chunk_rows(k), :] for k in range(K)], axis=0) for c in range(nio)],
        axis=1)
    s_ref[...] = jnp.dot(u.astype(BF16), bblk_ref[...], preferred_element_type=F32)

    def cols(j):
        return pl.ds(j * LANES, LANES), pl.ds(n_state + j * LANES, LANES)

    def rows(k):
        return pl.ds(pl.multiple_of(k * SUBLANES, SUBLANES), SUBLANES)

    a_re = [jnp.broadcast_to(lam_ref[0:1, pl.ds(j * LANES, LANES)], (SUBLANES, LANES)) for j in range(ncol)]
    a_im = [jnp.broadcast_to(lam_ref[1:2, pl.ds(j * LANES, LANES)], (SUBLANES, LANES)) for j in range(ncol)]

    def pass1(k, st):
        out = []
        for j in range(ncol):
            cr, ci = cols(j)
            sr, si = st[2 * j], st[2 * j + 1]
            nr = a_re[j] * sr - a_im[j] * si + s_ref[rows(k), cr]
            ni = a_re[j] * si + a_im[j] * sr + s_ref[rows(k), ci]
            s_ref[rows(k), cr] = nr
            s_ref[rows(k), ci] = ni
            out += [nr, ni]
        return tuple(out)

    zero = jnp.zeros((SUBLANES, LANES), F32)
    ends = lax.fori_loop(0, K, pass1, (zero,) * (2 * ncol))

    carries = []
    for j in range(ncol):
        cr, ci = cols(j)
        kr = lamk_ref[0:1, pl.ds(j * LANES, LANES)]
        ki = lamk_ref[1:2, pl.ds(j * LANES, LANES)]
        er, ei = ends[2 * j], ends[2 * j + 1]
        c_r = [carry_ref[0:1, cr]]
        c_i = [carry_ref[0:1, ci]]
        for r in range(1, SUBLANES + 1):
            pr, pi = c_r[-1], c_i[-1]
            c_r.append(kr * pr - ki * pi + er[r - 1:r, :])
            c_i.append(kr * pi + ki * pr + ei[r - 1:r, :])
        carry_ref[0:1, cr] = c_r[SUBLANES]
        carry_ref[0:1, ci] = c_i[SUBLANES]
        carries += [jnp.concatenate(c_r[:SUBLANES], axis=0), jnp.concatenate(c_i[:SUBLANES], axis=0)]

    def pass2(k, st):
        out = []
        for j in range(ncol):
            cr, ci = cols(j)
            dr, di = st[2 * j], st[2 * j + 1]
            nr = a_re[j] * dr - a_im[j] * di
            ni = a_re[j] * di + a_im[j] * dr
            s_ref[rows(k), cr] = s_ref[rows(k), cr] + nr
            s_ref[rows(k), ci] = s_ref[rows(k), ci] + ni
            out += [nr, ni]
        return tuple(out)

    lax.fori_loop(0, K, pass2, tuple(carries))

    y = jnp.dot(s_ref[...].astype(BF16), cblk_ref[...], preferred_element_type=F32)
    y = y + d_ref[...] * u
    y = jax.nn.gelu(y, approximate=True)
    gate = jnp.dot(y.astype(BF16), wglu_ref[...], preferred_element_type=F32) + bglu_ref[...]
    out = y * jax.nn.sigmoid(gate)
    for c in range(nio):
        for k in range(K):
            io_ref[c, sub_chunk_rows(k), :] = out[k * SUBLANES:(k + 1) * SUBLANES, c * LANES:(c + 1) * LANES]
    o_ref[...] = jnp.concatenate([io_ref[c] for c in range(nio)], axis=1).astype(BF16)


def s5_mixer(hp, u_col_block, batch, seq, bblk, lam, lamk, cblk, d_skip, w_glu, b_glu, tm=512):
    width = bblk.shape[0]
    n_state = bblk.shape[1] // 2
    nt = seq // tm
    return pl.pallas_call(
        functools.partial(_s5_kernel, n_state=n_state),
        grid=(batch, nt),
        in_specs=[
            pl.BlockSpec((tm, width), lambda b, t: (b * nt + t, u_col_block)),
            _const_spec(bblk.shape), _const_spec(lam.shape), _const_spec(lamk.shape),
            _const_spec(cblk.shape), _const_spec((1, width)), _const_spec(w_glu.shape),
            _const_spec((1, width)),
        ],
        out_specs=pl.BlockSpec((tm, width), lambda b, t: (b * nt + t, 0)),
        out_shape=jax.ShapeDtypeStruct((batch * seq, width), BF16),
        scratch_shapes=[pltpu.VMEM((width // LANES, tm, LANES), F32),
                        pltpu.VMEM((tm, 2 * n_state), F32),
                        pltpu.VMEM((SUBLANES, 2 * n_state), F32)],
        compiler_params=_params("parallel", "arbitrary"),
        name="s5_mixer",
    )(hp, bblk, lam, lamk, cblk, d_skip.reshape(1, width), w_glu, b_glu.reshape(1, width))


def s5_tables(lam_re, lam_im, log_dt, b_re, b_im, c_re, c_im, sub_chunk):
    G, P = lam_re.shape
    dt = jnp.exp(log_dt.astype(F32))[:, None]

    def lam_pow(n):
        mag = jnp.exp(lam_re * dt * n)
        return mag * jnp.cos(lam_im * dt * n), mag * jnp.sin(lam_im * dt * n)

    lr, li = lam_pow(1.0)
    kr, ki = lam_pow(float(sub_chunk))
    nr, ni = lr - 1.0, li
    den = lam_re * lam_re + lam_im * lam_im
    zr = (nr * lam_re + ni * lam_im) / den
    zi = (ni * lam_re - nr * lam_im) / den
    bbr = b_re * zr[..., None] - b_im * zi[..., None]
    bbi = b_re * zi[..., None] + b_im * zr[..., None]
    eye = jnp.eye(G, dtype=F32)
    Cg = b_re.shape[-1]

    def in_blk(m):
        return jnp.einsum('gpc,gh->gchp', m, eye).reshape(G * Cg, G * P)

    def out_blk(m):
        return jnp.einsum('gcp,gh->gphc', m, eye).reshape(G * P, G * Cg)

    bblk = jnp.concatenate([in_blk(bbr), in_blk(bbi)], axis=1).astype(BF16)
    cblk = jnp.concatenate([out_blk(c_re), -out_blk(c_im)], axis=0).astype(BF16)
    lam = jnp.stack([lr.reshape(-1), li.reshape(-1)])
    lamk = jnp.stack([kr.reshape(-1), ki.reshape(-1)])
    return bblk, lam, lamk, cblk


def _log_sigmoid_pair(z):
    soft = jnp.log(1.0 + jnp.exp(-jnp.abs(z)))
    lp = jnp.minimum(z, 0.0) - soft
    return lp, lp - z


SB_EXIT = -110.0


def _sb_kernel(q_ref, k_ref, v_ref, tri_ref, o_ref, acc_ref, carry_ref, *, blk, group):
    i = pl.program_id(2)
    q = q_ref[...]
    lane = lax.broadcasted_iota(jnp.int32, (1, LANES), 1)
    head_lanes = [lane < SB_HEAD_DIM, lane >= SB_HEAD_DIM]
    zero = jnp.zeros((), BF16)
    qh = [jnp.where(m, q, zero) for m in head_lanes]
    row = lax.broadcasted_iota(jnp.int32, (blk, blk), 0)
    col = lax.broadcasted_iota(jnp.int32, (blk, blk), 1)
    strictly_earlier = col < row

    acc_ref[...] = jnp.zeros_like(acc_ref)
    carry_ref[...] = jnp.zeros_like(carry_ref)

    def group_step(g, first):
        zs, lps, rs, vs = [], [], [], []
        for u in range(group):
            j = i - g * group - u
            start = pl.multiple_of(jnp.maximum(j, 0) * blk, blk)
            kj = k_ref[pl.ds(start, blk), :]
            vj = v_ref[pl.ds(start, blk), :]
            for h in range(2):
                zs.append(lax.dot_general(qh[h], kj, (((1,), (1,)), ((), ())), preferred_element_type=F32))
                vs.append(jnp.where(head_lanes[h] & (j >= 0), vj, zero))
        for n, z in enumerate(zs):
            lp, l1m = _log_sigmoid_pair(z)
            if first and n < 2:
                l1m = jnp.where(strictly_earlier, l1m, 0.0)
            lps.append(lp)
            rs.append(jnp.dot(l1m.astype(BF16), tri_ref[...], preferred_element_type=F32))
        ws = [None] * len(zs)
        top = None
        for h in range(2):
            cum = carry_ref[h]
            for u in range(group):
                n = 2 * u + h
                w = jnp.exp(lps[n] + rs[n][:, :blk] + cum)
                if first and u == 0:
                    w = jnp.where(strictly_earlier, w, 0.0)
                ws[n] = w.astype(BF16)
                cum = cum + rs[n][:, blk:]
            carry_ref[h] = cum
            top = cum if top is None else jnp.maximum(top, cum)
        acc = acc_ref[...]
        for w, v in zip(ws, vs):
            acc = acc + jnp.dot(w, v, preferred_element_type=F32)
        acc_ref[...] = acc
        return jnp.max(top)

    def more(state):
        g, top = state
        return (g * group <= i) & (top > SB_EXIT)

    def step(state):
        g, _ = state
        return g + 1, group_step(g, False)

    lax.while_loop(more, step, (1, group_step(0, True)))
    o_ref[...] = acc_ref[...].astype(BF16)


def stick_breaking(hp, q_blk0, k_blk0, v_blk0, batch, seq, blk=128, group=4):
    assert blk == LANES
    nq = seq // blk
    n_pairs = SB_HEADS * SB_HEAD_DIM // LANES
    r = np.arange(blk)
    tri = np.concatenate([(r[:, None] > r[None, :]), np.ones((blk, blk), bool)], axis=1)
    tri = jnp.asarray(tri, BF16)
    return pl.pallas_call(
        functools.partial(_sb_kernel, blk=blk, group=group),
        grid=(batch, n_pairs, nq),
        in_specs=[
            pl.BlockSpec((blk, LANES), lambda b, p, i: (b * nq + i, q_blk0 + p)),
            pl.BlockSpec((seq, LANES), lambda b, p, i: (b, k_blk0 + p)),
            pl.BlockSpec((seq, LANES), lambda b, p, i: (b, v_blk0 + p)),
            _const_spec(tri.shape),
        ],
        out_specs=pl.BlockSpec((blk, LANES), lambda b, p, i: (b * nq + i, p)),
        out_shape=jax.ShapeDtypeStruct((batch * seq, n_pairs * LANES), BF16),
        scratch_shapes=[pltpu.VMEM((blk, LANES), F32), pltpu.VMEM((2, blk, blk), F32)],
        compiler_params=_params("parallel", "parallel", "arbitrary"),
        name="stick_breaking",
    )(hp, hp, hp, tri)


def _ret_kernel(qk_ref, v_ref, g_ref, cos_ref, sin_ref, mask_ref, rdec_ref, cdec_ref, tdec_ref, gn_ref,
                o_ref, state_ref):
    @pl.when(pl.program_id(1) == 0)
    def _():
        state_ref[...] = jnp.zeros_like(state_ref)

    half = RET_HEADS * RET_DK // 2
    qk = qk_ref[...].astype(F32)
    cos, sin = cos_ref[...], sin_ref[...]

    def rope(t):
        t1, t2 = t[:, :half], t[:, half:]
        return jnp.concatenate([t1 * cos - t2 * sin, t1 * sin + t2 * cos], axis=1)

    qr = rope(qk[:, :2 * half])
    kr = rope(qk[:, 2 * half:])
    kb = kr.astype(BF16)
    lane = lax.broadcasted_iota(jnp.int32, (1, 2 * half), 1)
    head_of_lane = (lane % half) // (RET_DK // 2)
    for h in range(RET_HEADS):
        qh = jnp.where(head_of_lane == h, qr, 0.0)
        vh = v_ref[:, h * RET_DV:(h + 1) * RET_DV]
        scores = lax.dot_general(qh.astype(BF16), kb, (((1,), (1,)), ((), ())), preferred_element_type=F32)
        scores = scores * mask_ref[h]
        o = jnp.dot(scores.astype(BF16), vh, preferred_element_type=F32)
        o = o + jnp.dot((qh * rdec_ref[h]).astype(BF16), state_ref[h].astype(BF16), preferred_element_type=F32)
        kv = lax.dot_general((kr * cdec_ref[h]).astype(BF16), vh, (((0,), (0,)), ((), ())),
                             preferred_element_type=F32)
        state_ref[h] = tdec_ref[h] * state_ref[h] + kv
        mu = jnp.mean(o, axis=-1, keepdims=True)
        oc = o - mu
        var = jnp.mean(oc * oc, axis=-1, keepdims=True)
        on = oc * lax.rsqrt(var + LN_EPS) * gn_ref[:, h * RET_DV:(h + 1) * RET_DV]
        g = g_ref[:, h * RET_DV:(h + 1) * RET_DV].astype(F32)
        o_ref[:, h * RET_DV:(h + 1) * RET_DV] = (g * jax.nn.sigmoid(g) * on).astype(BF16)


def retention_tables(seq, tile):
    halfdim = RET_DK // 2
    inv_freq = ROPE_BASE ** (-jnp.arange(halfdim, dtype=F32) / halfdim)
    ang = jnp.arange(seq, dtype=F32)[:, None] * inv_freq[None, :]
    cos = jnp.tile(jnp.cos(ang), (1, RET_HEADS))
    sin = jnp.tile(jnp.sin(ang), (1, RET_HEADS))
    log_gamma = jnp.log(1.0 - 2.0 ** (-5.0 - jnp.arange(RET_HEADS, dtype=F32)))
    t = jnp.arange(tile, dtype=F32)
    same_or_earlier_chunk = (t[None, :] // CHUNK) <= (t[:, None] // CHUNK)
    mask = jnp.exp(log_gamma[:, None, None] * jnp.abs(t[:, None] - t[None, :]))
    mask = jnp.where(same_or_earlier_chunk[None], mask, 0.0)
    rdec = jnp.exp(log_gamma[:, None, None] * (t[None, :, None] + 1.0))
    cdec = jnp.exp(log_gamma[:, None, None] * (tile - 1.0 - t[None, :, None]))
    tdec = jnp.broadcast_to(jnp.exp(log_gamma * tile)[:, None, None], (RET_HEADS, 1, RET_DV))
    return cos, sin, mask, rdec, cdec, tdec


def retention(hp, qk_blk, v_blk, g_blk, batch, seq, gn_g, tile=256):
    cos, sin, mask, rdec, cdec, tdec = retention_tables(seq, tile)
    nt = seq // tile
    qkw = 2 * RET_HEADS * RET_DK
    vw = RET_HEADS * RET_DV
    half = RET_HEADS * RET_DK // 2
    return pl.pallas_call(
        _ret_kernel,
        grid=(batch, nt),
        in_specs=[
            pl.BlockSpec((tile, qkw), lambda b, t: (b * nt + t, qk_blk)),
            pl.BlockSpec((tile, vw), lambda b, t: (b * nt + t, v_blk)),
            pl.BlockSpec((tile, vw), lambda b, t: (b * nt + t, g_blk)),
            pl.BlockSpec((tile, half), lambda b, t: (t, 0)),
            pl.BlockSpec((tile, half), lambda b, t: (t, 0)),
            _const_spec(mask.shape), _const_spec(rdec.shape), _const_spec(cdec.shape),
            _const_spec(tdec.shape), _const_spec((1, vw)),
        ],
        out_specs=pl.BlockSpec((tile, vw), lambda b, t: (b * nt + t, 0)),
        out_shape=jax.ShapeDtypeStruct((batch * seq, vw), BF16),
        scratch_shapes=[pltpu.VMEM((RET_HEADS, 2 * half, RET_DV), F32)],
        compiler_params=_params("parallel", "arbitrary"),
        name="retention",
    )(hp, hp, hp, cos, sin, mask, rdec, cdec, tdec, gn_g.reshape(1, vw))


def _first_max(vals, lane):
    m = jnp.max(vals, axis=-1, keepdims=True)
    idx = jnp.min(jnp.where(vals == m, lane, float(LANES)), axis=-1, keepdims=True)
    return m, idx


def _merge_kernel(ga_ref, gb_ref, gc_ref, ya_ref, yb_ref, yc_ref, h_ref, wa_ref, wb_ref, wc_ref, wo_ref,
                  g1_ref, b1_ref, rw_ref, rb_ref, h1_ref, info_ref, cnt_ref, run_ref):
    @pl.when(pl.program_id(0) == 0)
    def _():
        run_ref[...] = jnp.zeros_like(run_ref)

    def branch(g_ref, y_ref, w_ref):
        up = jnp.dot(y_ref[...], w_ref[...], preferred_element_type=F32)
        return g_ref[...].astype(F32) * up

    merged = branch(ga_ref, ya_ref, wa_ref) + branch(gb_ref, yb_ref, wb_ref) + branch(gc_ref, yc_ref, wc_ref)
    y = jnp.dot(merged.astype(BF16), wo_ref[...], preferred_element_type=F32)
    h1 = _layer_norm(ALPHA * h_ref[...] + y, g1_ref[...], b1_ref[...])
    h1_ref[...] = h1

    h_hi = h1.astype(BF16)
    h_lo = (h1 - h_hi.astype(F32)).astype(BF16)
    logits = (jnp.dot(h_hi, rw_ref[0], preferred_element_type=F32)
              + jnp.dot(h_lo, rw_ref[0], preferred_element_type=F32)
              + jnp.dot(h_hi, rw_ref[1], preferred_element_type=F32)) + rb_ref[...]
    tm = logits.shape[0]
    lane_i = lax.broadcasted_iota(jnp.int32, (tm, LANES), 1)
    lane = lane_i.astype(F32)
    ex = jnp.exp(logits - jnp.max(logits, axis=-1, keepdims=True))
    scores = ex / jnp.sum(ex, axis=-1, keepdims=True)
    group = (lane_i // EXPERTS_PER_GROUP).astype(F32)
    best = jnp.zeros((tm, 1), F32)
    best_score = jnp.full((tm, 1), -1.0, F32)
    for gi in range(N_EXPERTS // EXPERTS_PER_GROUP):
        sg = jnp.where(group == float(gi), scores, -1.0)
        m1, i1 = _first_max(sg, lane)
        m2 = jnp.max(jnp.where(lane == i1, -1.0, sg), axis=-1, keepdims=True)
        gs = m1 + m2
        better = gs > best_score
        best = jnp.where(better, float(gi), best)
        best_score = jnp.where(better, gs, best_score)
    masked = jnp.where(group == best, scores, -1.0)
    w1, e1 = _first_max(masked, lane)
    w2, e2 = _first_max(jnp.where(lane == e1, -2.0, masked), lane)
    den = w1 + w2
    gate1, gate2 = w1 / den, w2 / den

    sel1, sel2 = lane == e1, lane == e2
    onehot = jnp.where(sel1 | sel2, 1.0, 0.0)
    r = lax.broadcasted_iota(jnp.int32, (tm, tm), 0)
    c = lax.broadcasted_iota(jnp.int32, (tm, tm), 1)
    earlier = jnp.where(c < r, 1.0, 0.0).astype(BF16)
    rank = jnp.dot(earlier, onehot.astype(BF16), preferred_element_type=F32) + run_ref[0:1, :]
    rank1 = jnp.sum(jnp.where(sel1, rank, 0.0), axis=-1, keepdims=True)
    rank2 = jnp.sum(jnp.where(sel2, rank, 0.0), axis=-1, keepdims=True)
    run_ref[0:1, :] = run_ref[0:1, :] + jnp.sum(onehot, axis=0, keepdims=True)
    cnt_ref[...] = run_ref[...]

    info = jnp.zeros((tm, LANES), F32)
    for k, val in enumerate((e1, e2, gate1, gate2, rank1, rank2)):
        info = jnp.where(lane_i == k, val, info)
    info_ref[...] = info


def merge_norm_route(hp, ya, yb, yc, h, wa, wb, wc, wo, ln_g, ln_b, rw, rb, tm=256):
    T, D = h.shape
    row = lambda i: (i, 0)
    return pl.pallas_call(
        _merge_kernel,
        grid=(T // tm,),
        in_specs=[
            pl.BlockSpec((tm, D), lambda i: (i, 0)),
            pl.BlockSpec((tm, D), lambda i: (i, 1)),
            pl.BlockSpec((tm, D), lambda i: (i, 2)),
            pl.BlockSpec((tm, ya.shape[1]), row), pl.BlockSpec((tm, yb.shape[1]), row),
            pl.BlockSpec((tm, yc.shape[1]), row), pl.BlockSpec((tm, D), row),
            _const_spec(wa.shape), _const_spec(wb.shape), _const_spec(wc.shape), _const_spec(wo.shape),
            _const_spec((1, D)), _const_spec((1, D)), _const_spec(rw.shape), _const_spec(rb.shape),
        ],
        out_specs=[pl.BlockSpec((tm, D), row), pl.BlockSpec((tm, LANES), row),
                   pl.BlockSpec((SUBLANES, LANES), lambda i: (0, 0))],
        out_shape=[jax.ShapeDtypeStruct((T, D), F32), jax.ShapeDtypeStruct((T, LANES), F32),
                   jax.ShapeDtypeStruct((SUBLANES, LANES), F32)],
        scratch_shapes=[pltpu.VMEM((SUBLANES, LANES), F32)],
        compiler_params=_params("arbitrary"),
        name="merge_norm_route",
    )(hp, hp, hp, ya, yb, yc, h, wa, wb, wc, wo, ln_g.reshape(1, D), ln_b.reshape(1, D), rw, rb)


ROW_UNROLL = 8


def _expert_kernel(be_ref, nused_ref, nvalid_ref, inv_ref, h_ref, w1_ref, w3_ref, w2_ref, ys_ref,
                   xbuf, ybuf, w1b, w3b, w2b, gsem, ssem, *, n_tok):
    i = pl.program_id(0)
    n_used = nused_ref[0]
    blk = xbuf.shape[1]
    slot = lax.rem(i, 2)

    def for_all_rows(fn):
        def body(c, carry):
            for k in range(ROW_UNROLL):
                fn(c * ROW_UNROLL + k)
            return carry
        lax.fori_loop(0, blk // ROW_UNROLL, body, 0)

    def start_gather(block, s):
        def one(r):
            tok = lax.rem(inv_ref[block * blk + r], n_tok)
            pltpu.make_async_copy(h_ref.at[pl.ds(tok, 1)], xbuf.at[s, pl.ds(r, 1)], gsem.at[s]).start()
        for_all_rows(one)

    def start_scatter(block, s):
        n_rows = nvalid_ref[block]

        def one(r):
            pltpu.make_async_copy(ybuf.at[s, pl.ds(r, 1)], ys_ref.at[pl.ds(inv_ref[block * blk + r], 1)],
                                  ssem.at[s]).start()

        @pl.when(n_rows == blk)
        def _():
            for_all_rows(one)

        @pl.when(n_rows < blk)
        def _():
            def body(r, carry):
                one(r)
                return carry
            lax.fori_loop(0, n_rows, body, 0)

    def wait_gather(s):
        pltpu.make_async_copy(h_ref.at[pl.ds(0, blk)], xbuf.at[s], gsem.at[s]).wait()

    def wait_scatter(block, s):
        n_rows = nvalid_ref[block]

        @pl.when(n_rows == blk)
        def _():
            pltpu.make_async_copy(ybuf.at[s], ys_ref.at[pl.ds(0, blk)], ssem.at[s]).wait()

        @pl.when(n_rows < blk)
        def _():
            def body(r, carry):
                pltpu.make_async_copy(ybuf.at[s, pl.ds(0, 1)], ys_ref.at[pl.ds(0, 1)], ssem.at[s]).wait()
                return carry
            lax.fori_loop(0, n_rows, body, 0)

    @pl.when(i < n_used)
    def _():
        @pl.when(i == 0)
        def _():
            start_gather(0, 0)

        wait_gather(slot)

        @pl.when(i + 1 < n_used)
        def _():
            start_gather(i + 1, 1 - slot)

        @pl.when(i >= 2)
        def _():
            wait_scatter(i - 2, slot)

        @pl.when((i == 0) | (be_ref[i] != be_ref[jnp.maximum(i - 1, 0)]))
        def _():
            w1b[...] = w1_ref[0].astype(BF16)
            w3b[...] = w3_ref[0].astype(BF16)
            w2b[...] = w2_ref[0].astype(BF16)

        x = xbuf[slot].astype(BF16)
        a = jnp.dot(x, w1b[...], preferred_element_type=F32)
        b = jnp.dot(x, w3b[...], preferred_element_type=F32)
        hb = (a * jax.nn.sigmoid(a) * b).astype(BF16)
        ybuf[slot] = jnp.dot(hb, w2b[...], preferred_element_type=F32)
        start_scatter(i, slot)

        @pl.when(i == n_used - 1)
        def _():
            wait_scatter(i, slot)

            @pl.when(i >= 1)
            def _():
                wait_scatter(i - 1, 1 - slot)


def moe_experts(h1, inv, block_e, n_used, n_valid, w1, w3, w2, blk):
    T, D = h1.shape
    E, _, Fd = w1.shape
    nb = inv.shape[0] // blk
    wspec = lambda s: pl.BlockSpec((1,) + s, lambda i, be, nu, nv, iv: (be[i], 0, 0))
    return pl.pallas_call(
        functools.partial(_expert_kernel, n_tok=T),
        grid_spec=pltpu.PrefetchScalarGridSpec(
            num_scalar_prefetch=4,
            grid=(nb,),
            in_specs=[pl.BlockSpec(memory_space=pl.ANY), wspec((D, Fd)), wspec((D, Fd)), wspec((Fd, D))],
            out_specs=pl.BlockSpec(memory_space=pl.ANY),
            scratch_shapes=[pltpu.VMEM((2, blk, D), F32), pltpu.VMEM((2, blk, D), F32),
                            pltpu.VMEM((D, Fd), BF16), pltpu.VMEM((D, Fd), BF16), pltpu.VMEM((Fd, D), BF16),
                            pltpu.SemaphoreType.DMA((2,)), pltpu.SemaphoreType.DMA((2,))],
        ),
        out_shape=jax.ShapeDtypeStruct((2 * T, D), F32),
        compiler_params=pltpu.CompilerParams(dimension_semantics=("arbitrary",), vmem_limit_bytes=VMEM_LIMIT,
                                             has_side_effects=True),
        name="moe_experts",
    )(block_e, n_used, n_valid, inv, h1, w1, w3, w2)


def _combine_kernel(y1_ref, y2_ref, h_ref, info_ref, g_ref, b_ref, o_ref):
    info = info_ref[...]
    y = info[:, 2:3] * y1_ref[...] + info[:, 3:4] * y2_ref[...]
    o_ref[...] = _layer_norm(ALPHA * h_ref[...] + y, g_ref[...], b_ref[...])


def moe_combine(ys, h, info, ln_g, ln_b, tm=512):
    T, D = h.shape
    nt = T // tm
    return pl.pallas_call(
        _combine_kernel,
        grid=(nt,),
        in_specs=[pl.BlockSpec((tm, D), lambda i: (i, 0)),
                  pl.BlockSpec((tm, D), lambda i: (nt + i, 0)),
                  pl.BlockSpec((tm, D), lambda i: (i, 0)),
                  pl.BlockSpec((tm, LANES), lambda i: (i, 0)),
                  _const_spec((1, D)), _const_spec((1, D))],
        out_specs=pl.BlockSpec((tm, D), lambda i: (i, 0)),
        out_shape=jax.ShapeDtypeStruct((T, D), F32),
        compiler_params=_params("parallel"),
        name="moe_combine",
    )(ys, ys, h, info, ln_g.reshape(1, D), ln_b.reshape(1, D))


def route_tables(info, counts, blk, n_blocks):
    T = info.shape[0]
    e = info[:, 0:2].astype(jnp.int32)
    rank = info[:, 4:6].astype(jnp.int32)
    cnt = counts[0, :N_EXPERTS].astype(jnp.int32)
    padded = (cnt + blk - 1) // blk * blk
    pends = jnp.cumsum(padded)
    pstarts = pends - padded
    dest = (pstarts[e] + rank).T.reshape(-1)
    n_rows = n_blocks * blk
    inv = jnp.arange(n_rows, dtype=jnp.int32).at[dest].set(jnp.arange(2 * T, dtype=jnp.int32))
    block_row = jnp.arange(n_blocks, dtype=jnp.int32) * blk
    block_e = jnp.clip(jnp.searchsorted(pends, block_row, side='right'), 0, N_EXPERTS - 1).astype(jnp.int32)
    n_valid = jnp.clip(pstarts[block_e] + cnt[block_e] - block_row, 0, blk).astype(jnp.int32)
    n_used = (pends[-1:] // blk).astype(jnp.int32)
    return inv, block_e, n_used, n_valid


def grouped_moe_norm(h1, info, counts, w1, w3, w2, ln_g, ln_b, blk=512):
    T, D = h1.shape
    n_blocks = (2 * T) // blk + N_EXPERTS
    inv, block_e, n_used, n_valid = route_tables(info, counts, blk, n_blocks)
    ys = moe_experts(h1, inv, block_e, n_used, n_valid, w1, w3, w2, blk)
    return moe_combine(ys, h1, info, ln_g, ln_b)


def _in_proj_layout(d_model):
    s5w = d_model // 4
    sbw = SB_HEADS * SB_HEAD_DIM
    rqk = RET_HEADS * RET_DK
    rv = RET_HEADS * RET_DV
    sizes = (s5w, sbw, sbw, sbw, rqk, rqk, rv, rv, d_model, d_model, d_model)
    off = np.concatenate([[0], np.cumsum(sizes)])
    u_a, q_b, k_b, v_b, q_c, k_c, v_c, g_c, ga, gb, gc = [np.arange(off[i], off[i + 1]) for i in range(11)]

    def rope_perm(cols):
        c = cols.reshape(RET_HEADS, 2, RET_DK // 2)
        return np.concatenate([c[:, 0].reshape(-1), c[:, 1].reshape(-1)])

    order = np.concatenate([ga, gb, gc, rope_perm(q_c), rope_perm(k_c), v_c, g_c, q_b, k_b, v_b, u_a])
    scale = np.ones(off[-1], np.float32)
    scale[q_b] = SB_HEAD_DIM ** -0.5
    scale[k_c] = RET_DK ** -0.5
    o_qk = 3 * d_model
    o_v = o_qk + 2 * rqk
    o_g = o_v + rv
    o_sb = o_g + rv
    o_u = o_sb + 3 * sbw
    blocks = dict(qk=o_qk // (2 * rqk), v=o_v // rv, g=o_g // rv, sb_q=o_sb // LANES,
                  sb_k=(o_sb + sbw) // LANES, sb_v=(o_sb + 2 * sbw) // LANES, u=o_u // s5w)
    assert o_qk % (2 * rqk) == 0 and o_v % rv == 0 and o_g % rv == 0 and o_sb % LANES == 0 and o_u % s5w == 0
    return order, scale[order], blocks


def _layer(h, batch, seq, p):
    D = h.shape[1]
    order, scale, blk = _in_proj_layout(D)
    w_in = (p['w_in'][:, order] * scale).astype(BF16)
    hp = in_projection(h, w_in, n_gate_cols=3 * D)
    s5_tile = 512
    bblk, lam, lamk, cblk = s5_tables(p['lam_re'], p['lam_im'], p['log_dt'], p['b_re'], p['b_im'],
                                      p['c_re'], p['c_im'], s5_tile // SUBLANES)
    ya = s5_mixer(hp, blk['u'], batch, seq, bblk, lam, lamk, cblk, p['d_skip'], p['w_glu'].astype(BF16),
                  p['b_glu'], tm=s5_tile)
    yb = stick_breaking(hp, blk['sb_q'], blk['sb_k'], blk['sb_v'], batch, seq)
    yc = retention(hp, blk['qk'], blk['v'], blk['g'], batch, seq, p['gn_g'])
    rw = jnp.zeros((D, LANES), F32).at[:, :N_EXPERTS].set(p['router_w'])
    rw_hi = rw.astype(BF16)
    rw = jnp.stack([rw_hi, (rw - rw_hi.astype(F32)).astype(BF16)])
    rb = jnp.full((1, LANES), NEG_BIG, F32).at[0, :N_EXPERTS].set(p['router_b'])
    h1, info, counts = merge_norm_route(
        hp, ya, yb, yc, h, p['w_up_a'].astype(BF16), p['w_up_b'].astype(BF16), p['w_up_c'].astype(BF16),
        p['w_out'].astype(BF16), p['ln1_g'], p['ln1_b'], rw, rb)
    return grouped_moe_norm(h1, info, counts, p['moe_w1'], p['moe_w3'], p['moe_w2'], p['ln2_g'], p['ln2_b'])


def kernel(x, ln0_g, ln0_b, w_in, s5_lambda_re, s5_lambda_im, s5_log_dt, s5_b_re, s5_b_im, s5_c_re, s5_c_im,
           s5_d, s5_w_glu, s5_b_glu, ret_gn_g, w_up_a, w_up_b, w_up_c, w_out, ln1_g, ln1_b, router_w, router_b,
           moe_w1, moe_w3, moe_w2, ln2_g, ln2_b):
    batch, seq, D = x.shape
    h = layer_norm_rows(x.reshape(batch * seq, D), ln0_g, ln0_b)
    for l in range(w_in.shape[0]):
        p = dict(w_in=w_in[l], lam_re=s5_lambda_re[l], lam_im=s5_lambda_im[l], log_dt=s5_log_dt[l],
                 b_re=s5_b_re[l], b_im=s5_b_im[l], c_re=s5_c_re[l], c_im=s5_c_im[l], d_skip=s5_d[l],
                 w_glu=s5_w_glu[l], b_glu=s5_b_glu[l], gn_g=ret_gn_g[l], w_up_a=w_up_a[l], w_up_b=w_up_b[l],
                 w_up_c=w_up_c[l], w_out=w_out[l], ln1_g=ln1_g[l], ln1_b=ln1_b[l], router_w=router_w,
                 router_b=router_b, moe_w1=moe_w1[l], moe_w3=moe_w3[l], moe_w2=moe_w2[l], ln2_g=ln2_g[l],
                 ln2_b=ln2_b[l])
        h = _layer(h, batch, seq, p)
    return h.reshape(batch, seq, D)
```

```python
import functools
import math

import numpy as np
import jax
import jax.numpy as jnp
from jax import lax
from jax.experimental import pallas as pl
from jax.experimental.pallas import tpu as pltpu

F32 = jnp.float32
BF16 = jnp.bfloat16

LANES = 128
SUBLANES = 8
VMEM_LIMIT = 56 * 1024 * 1024

DEPTH = 2
CHUNK = 64
S5_GROUP_CH = 16
S5_STATE = 64
SB_HEADS = 4
SB_HEAD_DIM = 64
RET_HEADS = 4
RET_DK = 64
RET_DV = 128
ROPE_BASE = 10000.0
N_EXPERTS = 16
EXPERTS_PER_GROUP = 4
ALPHA = (2 * DEPTH) ** 0.25
LN_EPS = 1e-5

GATE_OFF = 0
NEG_BIG = -1e30


def _params(*sem):
    return pltpu.CompilerParams(dimension_semantics=sem, vmem_limit_bytes=VMEM_LIMIT)


def _const_spec(shape):
    zeros = (0,) * len(shape)
    return pl.BlockSpec(shape, lambda *_: zeros, pipeline_mode=pl.Buffered(1))


def _layer_norm(x, g, b):
    mu = jnp.mean(x, axis=-1, keepdims=True)
    xc = x - mu
    var = jnp.mean(xc * xc, axis=-1, keepdims=True)
    return xc * lax.rsqrt(var + LN_EPS) * g + b


def _inproj_kernel(*refs, col_chunk, n_gate_cols, with_ln):
    if with_ln:
        x_ref, g_ref, b_ref, w_ref, o_ref, h_ref = refs
        x = _layer_norm(x_ref[...], g_ref[...], b_ref[...])
        h_ref[...] = x
    else:
        x_ref, w_ref, o_ref = refs
        x = x_ref[...]
    xb = x.astype(BF16)
    n = w_ref.shape[1]
    for c in range(0, n, col_chunk):
        acc = jnp.dot(xb, w_ref[:, c:c + col_chunk], preferred_element_type=F32)
        if c < n_gate_cols:
            acc = jax.nn.sigmoid(acc)
        o_ref[:, c:c + col_chunk] = acc.astype(BF16)


def in_projection(h, w_bf16, n_gate_cols, ln=None, tm=512, col_chunk=512):
    T, D = h.shape
    N = w_bf16.shape[1]
    assert n_gate_cols % col_chunk == 0
    rows = pl.BlockSpec((tm, D), lambda i: (i, 0))
    proj = pl.BlockSpec((tm, N), lambda i: (i, 0))
    args, in_specs = [h], [rows]
    out_specs, out_shape = [proj], [jax.ShapeDtypeStruct((T, N), BF16)]
    if ln is not None:
        args += [ln[0].reshape(1, D), ln[1].reshape(1, D)]
        in_specs += [_const_spec((1, D)), _const_spec((1, D))]
        out_specs.append(rows)
        out_shape.append(jax.ShapeDtypeStruct((T, D), F32))
    outs = pl.pallas_call(
        functools.partial(_inproj_kernel, col_chunk=col_chunk, n_gate_cols=n_gate_cols, with_ln=ln is not None),
        grid=(T // tm,),
        in_specs=in_specs + [_const_spec((D, N))],
        out_specs=out_specs,
        out_shape=out_shape,
        compiler_params=_params("parallel"),
        name="in_proj",
    )(*args, w_bf16)
    return (outs[0], outs[1]) if ln is not None else (outs[0], h)


def _s5_kernel(u_ref, bblk_ref, lam_ref, lamk_ref, cblk_ref, d_ref, wglu_ref, bglu_ref,
               o_ref, io_ref, s_ref, carry_ref, *, n_state):
    tm, width = u_ref.shape
    K = tm // SUBLANES
    ncol = n_state // LANES
    nio = width // LANES

    @pl.when(pl.program_id(1) == 0)
    def _():
        carry_ref[...] = jnp.zeros_like(carry_ref)

    def sub_chunk_rows(k):
        return pl.ds(k, SUBLANES, stride=K)

    for c in range(nio):
        io_ref[c] = u_ref[:, c * LANES:(c + 1) * LANES].astype(F32)
    u = jnp.concatenate(
        [jnp.concatenate([io_ref[c, sub_chunk_rows(k), :] for k in range(K)], axis=0) for c in range(nio)],
        axis=1)
    s_ref[...] = jnp.dot(u.astype(BF16), bblk_ref[...], preferred_element_type=F32)

    def cols(j):
        return pl.ds(j * LANES, LANES), pl.ds(n_state + j * LANES, LANES)

    def rows(k):
        return pl.ds(pl.multiple_of(k * SUBLANES, SUBLANES), SUBLANES)

    a_re = [jnp.broadcast_to(lam_ref[0:1, pl.ds(j * LANES, LANES)], (SUBLANES, LANES)) for j in range(ncol)]
    a_im = [jnp.broadcast_to(lam_ref[1:2, pl.ds(j * LANES, LANES)], (SUBLANES, LANES)) for j in range(ncol)]

    def pass1(k, st):
        out = []
        for j in range(ncol):
            cr, ci = cols(j)
            sr, si = st[2 * j], st[2 * j + 1]
            nr = a_re[j] * sr - a_im[j] * si + s_ref[rows(k), cr]
            ni = a_re[j] * si + a_im[j] * sr + s_ref[rows(k), ci]
            s_ref[rows(k), cr] = nr
            s_ref[rows(k), ci] = ni
            out += [nr, ni]
        return tuple(out)

    zero = jnp.zeros((SUBLANES, LANES), F32)
    ends = lax.fori_loop(0, K, pass1, (zero,) * (2 * ncol))

    carries = []
    for j in range(ncol):
        cr, ci = cols(j)
        kr = lamk_ref[0:1, pl.ds(j * LANES, LANES)]
        ki = lamk_ref[1:2, pl.ds(j * LANES, LANES)]
        er, ei = ends[2 * j], ends[2 * j + 1]
        c_r = [carry_ref[0:1, cr]]
        c_i = [carry_ref[0:1, ci]]
        for r in range(1, SUBLANES + 1):
            pr, pi = c_r[-1], c_i[-1]
            c_r.append(kr * pr - ki * pi + er[r - 1:r, :])
            c_i.append(kr * pi + ki * pr + ei[r - 1:r, :])
        carry_ref[0:1, cr] = c_r[SUBLANES]
        carry_ref[0:1, ci] = c_i[SUBLANES]
        carries += [jnp.concatenate(c_r[:SUBLANES], axis=0), jnp.concatenate(c_i[:SUBLANES], axis=0)]

    def pass2(k, st):
        out = []
        for j in range(ncol):
            cr, ci = cols(j)
            dr, di = st[2 * j], st[2 * j + 1]
            nr = a_re[j] * dr - a_im[j] * di
            ni = a_re[j] * di + a_im[j] * dr
            s_ref[rows(k), cr] = s_ref[rows(k), cr] + nr
            s_ref[rows(k), ci] = s_ref[rows(k), ci] + ni
            out += [nr, ni]
        return tuple(out)

    lax.fori_loop(0, K, pass2, tuple(carries))

    y = jnp.dot(s_ref[...].astype(BF16), cblk_ref[...], preferred_element_type=F32)
    y = y + d_ref[...] * u
    y = jax.nn.gelu(y, approximate=True)
    gate = jnp.dot(y.astype(BF16), wglu_ref[...], preferred_element_type=F32) + bglu_ref[...]
    out = y * jax.nn.sigmoid(gate)
    for c in range(nio):
        for k in range(K):
            io_ref[c, sub_chunk_rows(k), :] = out[k * SUBLANES:(k + 1) * SUBLANES, c * LANES:(c + 1) * LANES]
    o_ref[...] = jnp.concatenate([io_ref[c] for c in range(nio)], axis=1).astype(BF16)


def s5_mixer(hp, u_col_block, batch, seq, bblk, lam, lamk, cblk, d_skip, w_glu, b_glu, tm=512):
    width = bblk.shape[0]
    n_state = bblk.shape[1] // 2
    nt = seq // tm
    return pl.pallas_call(
        functools.partial(_s5_kernel, n_state=n_state),
        grid=(batch, nt),
        in_specs=[
            pl.BlockSpec((tm, width), lambda b, t: (b * nt + t, u_col_block)),
            _const_spec(bblk.shape), _const_spec(lam.shape), _const_spec(lamk.shape),
            _const_spec(cblk.shape), _const_spec((1, width)), _const_spec(w_glu.shape),
            _const_spec((1, width)),
        ],
        out_specs=pl.BlockSpec((tm, width), lambda b, t: (b * nt + t, 0)),
        out_shape=jax.ShapeDtypeStruct((batch * seq, width), BF16),
        scratch_shapes=[pltpu.VMEM((width // LANES, tm, LANES), F32),
                        pltpu.VMEM((tm, 2 * n_state), F32),
                        pltpu.VMEM((SUBLANES, 2 * n_state), F32)],
        compiler_params=_params("parallel", "arbitrary"),
        name="s5_mixer",
    )(hp, bblk, lam, lamk, cblk, d_skip.reshape(1, width), w_glu, b_glu.reshape(1, width))


def s5_tables(lam_re, lam_im, log_dt, b_re, b_im, c_re, c_im, sub_chunk):
    G, P = lam_re.shape
    dt = jnp.exp(log_dt.astype(F32))[:, None]

    def lam_pow(n):
        mag = jnp.exp(lam_re * dt * n)
        return mag * jnp.cos(lam_im * dt * n), mag * jnp.sin(lam_im * dt * n)

    lr, li = lam_pow(1.0)
    kr, ki = lam_pow(float(sub_chunk))
    nr, ni = lr - 1.0, li
    den = lam_re * lam_re + lam_im * lam_im
    zr = (nr * lam_re + ni * lam_im) / den
    zi = (ni * lam_re - nr * lam_im) / den
    bbr = b_re * zr[..., None] - b_im * zi[..., None]
    bbi = b_re * zi[..., None] + b_im * zr[..., None]
    eye = jnp.eye(G, dtype=F32)
    Cg = b_re.shape[-1]

    def in_blk(m):
        return jnp.einsum('gpc,gh->gchp', m, eye).reshape(G * Cg, G * P)

    def out_blk(m):
        return jnp.einsum('gcp,gh->gphc', m, eye).reshape(G * P, G * Cg)

    bblk = jnp.concatenate([in_blk(bbr), in_blk(bbi)], axis=1).astype(BF16)
    cblk = jnp.concatenate([out_blk(c_re), -out_blk(c_im)], axis=0).astype(BF16)
    lam = jnp.stack([lr.reshape(-1), li.reshape(-1)])
    lamk = jnp.stack([kr.reshape(-1), ki.reshape(-1)])
    return bblk, lam, lamk, cblk


def _log_sigmoid_pair(z):
    soft = jnp.log(1.0 + jnp.exp(-jnp.abs(z)))
    lp = jnp.minimum(z, 0.0) - soft
    return lp, lp - z


SB_EXIT = -110.0


def _sb_kernel(q_ref, k_ref, v_ref, tri_ref, o_ref, acc_ref, carry_ref, *, blk, group):
    i = pl.program_id(2)
    q = q_ref[...]
    lane = lax.broadcasted_iota(jnp.int32, (1, LANES), 1)
    head_lanes = [lane < SB_HEAD_DIM, lane >= SB_HEAD_DIM]
    zero = jnp.zeros((), BF16)
    qh = [jnp.where(m, q, zero) for m in head_lanes]
    row = lax.broadcasted_iota(jnp.int32, (blk, blk), 0)
    col = lax.broadcasted_iota(jnp.int32, (blk, blk), 1)
    strictly_earlier = col < row

    acc_ref[...] = jnp.zeros_like(acc_ref)
    carry_ref[...] = jnp.zeros_like(carry_ref)

    def group_step(g, first):
        zs, lps, rs, vs = [], [], [], []
        for u in range(group):
            j = i - g * group - u
            start = pl.multiple_of(jnp.maximum(j, 0) * blk, blk)
            kj = k_ref[pl.ds(start, blk), :]
            vj = v_ref[pl.ds(start, blk), :]
            for h in range(2):
                zs.append(lax.dot_general(qh[h], kj, (((1,), (1,)), ((), ())), preferred_element_type=F32))
                vs.append(jnp.where(head_lanes[h] & (j >= 0), vj, zero))
        for n, z in enumerate(zs):
            lp, l1m = _log_sigmoid_pair(z)
            if first and n < 2:
                l1m = jnp.where(strictly_earlier, l1m, 0.0)
            lps.append(lp)
            rs.append(jnp.dot(l1m.astype(BF16), tri_ref[...], preferred_element_type=F32))
        ws = [None] * len(zs)
        top = None
        for h in range(2):
            cum = carry_ref[h]
            for u in range(group):
                n = 2 * u + h
                w = jnp.exp(lps[n] + rs[n][:, :blk] + cum)
                if first and u == 0:
                    w = jnp.where(strictly_earlier, w, 0.0)
                ws[n] = w.astype(BF16)
                cum = cum + rs[n][:, blk:]
            carry_ref[h] = cum
            top = cum if top is None else jnp.maximum(top, cum)
        acc = acc_ref[...]
        for w, v in zip(ws, vs):
            acc = acc + jnp.dot(w, v, preferred_element_type=F32)
        acc_ref[...] = acc
        return jnp.max(top)

    def more(state):
        g, top = state
        return (g * group <= i) & (top > SB_EXIT)

    def step(state):
        g, _ = state
        return g + 1, group_step(g, False)

    lax.while_loop(more, step, (1, group_step(0, True)))
    o_ref[...] = acc_ref[...].astype(BF16)


def stick_breaking(hp, q_blk0, k_blk0, v_blk0, batch, seq, blk=128, group=3):
    assert blk == LANES
    nq = seq // blk
    n_pairs = SB_HEADS * SB_HEAD_DIM // LANES
    r = np.arange(blk)
    tri = np.concatenate([(r[:, None] > r[None, :]), np.ones((blk, blk), bool)], axis=1)
    tri = jnp.asarray(tri, BF16)
    return pl.pallas_call(
        functools.partial(_sb_kernel, blk=blk, group=group),
        grid=(batch, n_pairs, nq),
        in_specs=[
            pl.BlockSpec((blk, LANES), lambda b, p, i: (b * nq + i, q_blk0 + p)),
            pl.BlockSpec((seq, LANES), lambda b, p, i: (b, k_blk0 + p)),
            pl.BlockSpec((seq, LANES), lambda b, p, i: (b, v_blk0 + p)),
            _const_spec(tri.shape),
        ],
        out_specs=pl.BlockSpec((blk, LANES), lambda b, p, i: (b * nq + i, p)),
        out_shape=jax.ShapeDtypeStruct((batch * seq, n_pairs * LANES), BF16),
        scratch_shapes=[pltpu.VMEM((blk, LANES), F32), pltpu.VMEM((2, blk, blk), F32)],
        compiler_params=_params("parallel", "parallel", "arbitrary"),
        name="stick_breaking",
    )(hp, hp, hp, tri)


def _ret_kernel(qk_ref, v_ref, g_ref, cos_ref, sin_ref, mask_ref, rdec_ref, cdec_ref, tdec_ref, gn_ref,
                o_ref, state_ref):
    @pl.when(pl.program_id(1) == 0)
    def _():
        state_ref[...] = jnp.zeros_like(state_ref)

    half = RET_HEADS * RET_DK // 2
    qk = qk_ref[...].astype(F32)
    cos, sin = cos_ref[...], sin_ref[...]

    def rope(t):
        t1, t2 = t[:, :half], t[:, half:]
        return jnp.concatenate([t1 * cos - t2 * sin, t1 * sin + t2 * cos], axis=1)

    qr = rope(qk[:, :2 * half])
    kr = rope(qk[:, 2 * half:])
    kb = kr.astype(BF16)
    lane = lax.broadcasted_iota(jnp.int32, (1, 2 * half), 1)
    head_of_lane = (lane % half) // (RET_DK // 2)
    for h in range(RET_HEADS):
        qh = jnp.where(head_of_lane == h, qr, 0.0)
        vh = v_ref[:, h * RET_DV:(h + 1) * RET_DV]
        scores = lax.dot_general(qh.astype(BF16), kb, (((1,), (1,)), ((), ())), preferred_element_type=F32)
        scores = scores * mask_ref[h]
        o = jnp.dot(scores.astype(BF16), vh, preferred_element_type=F32)
        o = o + jnp.dot((qh * rdec_ref[h]).astype(BF16), state_ref[h].astype(BF16), preferred_element_type=F32)
        kv = lax.dot_general((kr * cdec_ref[h]).astype(BF16), vh, (((0,), (0,)), ((), ())),
                             preferred_element_type=F32)
        state_ref[h] = tdec_ref[h] * state_ref[h] + kv
        mu = jnp.mean(o, axis=-1, keepdims=True)
        oc = o - mu
        var = jnp.mean(oc * oc, axis=-1, keepdims=True)
        on = oc * lax.rsqrt(var + LN_EPS) * gn_ref[:, h * RET_DV:(h + 1) * RET_DV]
        g = g_ref[:, h * RET_DV:(h + 1) * RET_DV].astype(F32)
        o_ref[:, h * RET_DV:(h + 1) * RET_DV] = (g * jax.nn.sigmoid(g) * on).astype(BF16)


def retention_tables(seq, tile):
    halfdim = RET_DK // 2
    inv_freq = ROPE_BASE ** (-jnp.arange(halfdim, dtype=F32) / halfdim)
    ang = jnp.arange(seq, dtype=F32)[:, None] * inv_freq[None, :]
    cos = jnp.tile(jnp.cos(ang), (1, RET_HEADS))
    sin = jnp.tile(jnp.sin(ang), (1, RET_HEADS))
    log_gamma = jnp.log(1.0 - 2.0 ** (-5.0 - jnp.arange(RET_HEADS, dtype=F32)))
    t = jnp.arange(tile, dtype=F32)
    same_or_earlier_chunk = (t[None, :] // CHUNK) <= (t[:, None] // CHUNK)
    mask = jnp.exp(log_gamma[:, None, None] * jnp.abs(t[:, None] - t[None, :]))
    mask = jnp.where(same_or_earlier_chunk[None], mask, 0.0)
    rdec = jnp.exp(log_gamma[:, None, None] * (t[None, :, None] + 1.0))
    cdec = jnp.exp(log_gamma[:, None, None] * (tile - 1.0 - t[None, :, None]))
    tdec = jnp.broadcast_to(jnp.exp(log_gamma * tile)[:, None, None], (RET_HEADS, 1, RET_DV))
    return cos, sin, mask, rdec, cdec, tdec


def retention(hp, qk_blk, v_blk, g_blk, batch, seq, gn_g, tile=256):
    cos, sin, mask, rdec, cdec, tdec = retention_tables(seq, tile)
    nt = seq // tile
    qkw = 2 * RET_HEADS * RET_DK
    vw = RET_HEADS * RET_DV
    half = RET_HEADS * RET_DK // 2
    return pl.pallas_call(
        _ret_kernel,
        grid=(batch, nt),
        in_specs=[
            pl.BlockSpec((tile, qkw), lambda b, t: (b * nt + t, qk_blk)),
            pl.BlockSpec((tile, vw), lambda b, t: (b * nt + t, v_blk)),
            pl.BlockSpec((tile, vw), lambda b, t: (b * nt + t, g_blk)),
            pl.BlockSpec((tile, half), lambda b, t: (t, 0)),
            pl.BlockSpec((tile, half), lambda b, t: (t, 0)),
            _const_spec(mask.shape), _const_spec(rdec.shape), _const_spec(cdec.shape),
            _const_spec(tdec.shape), _const_spec((1, vw)),
        ],
        out_specs=pl.BlockSpec((tile, vw), lambda b, t: (b * nt + t, 0)),
        out_shape=jax.ShapeDtypeStruct((batch * seq, vw), BF16),
        scratch_shapes=[pltpu.VMEM((RET_HEADS, 2 * half, RET_DV), F32)],
        compiler_params=_params("parallel", "arbitrary"),
        name="retention",
    )(hp, hp, hp, cos, sin, mask, rdec, cdec, tdec, gn_g.reshape(1, vw))


def _first_max(vals, lane):
    m = jnp.max(vals, axis=-1, keepdims=True)
    idx = jnp.min(jnp.where(vals == m, lane, float(LANES)), axis=-1, keepdims=True)
    return m, idx


def _merge_kernel(ga_ref, gb_ref, gc_ref, ya_ref, yb_ref, yc_ref, h_ref, wa_ref, wb_ref, wc_ref, wo_ref,
                  g1_ref, b1_ref, rw_ref, rb_ref, h1_ref, info_ref, infot_ref, cnt_ref, run_ref):
    @pl.when(pl.program_id(0) == 0)
    def _():
        run_ref[...] = jnp.zeros_like(run_ref)

    def branch(g_ref, y_ref, w_ref):
        up = jnp.dot(y_ref[...], w_ref[...], preferred_element_type=F32)
        return g_ref[...].astype(F32) * up

    merged = branch(ga_ref, ya_ref, wa_ref) + branch(gb_ref, yb_ref, wb_ref) + branch(gc_ref, yc_ref, wc_ref)
    y = jnp.dot(merged.astype(BF16), wo_ref[...], preferred_element_type=F32)
    h1 = _layer_norm(ALPHA * h_ref[...] + y, g1_ref[...], b1_ref[...])
    h1_ref[...] = h1

    h_hi = h1.astype(BF16)
    h_lo = (h1 - h_hi.astype(F32)).astype(BF16)
    logits = (jnp.dot(h_hi, rw_ref[0], preferred_element_type=F32)
              + jnp.dot(h_lo, rw_ref[0], preferred_element_type=F32)
              + jnp.dot(h_hi, rw_ref[1], preferred_element_type=F32)) + rb_ref[...]
    tm = logits.shape[0]
    lane_i = lax.broadcasted_iota(jnp.int32, (tm, LANES), 1)
    lane = lane_i.astype(F32)
    ex = jnp.exp(logits - jnp.max(logits, axis=-1, keepdims=True))
    scores = ex / jnp.sum(ex, axis=-1, keepdims=True)
    group = (lane_i // EXPERTS_PER_GROUP).astype(F32)
    best = jnp.zeros((tm, 1), F32)
    best_score = jnp.full((tm, 1), -1.0, F32)
    for gi in range(N_EXPERTS // EXPERTS_PER_GROUP):
        sg = jnp.where(group == float(gi), scores, -1.0)
        m1, i1 = _first_max(sg, lane)
        m2 = jnp.max(jnp.where(lane == i1, -1.0, sg), axis=-1, keepdims=True)
        gs = m1 + m2
        better = gs > best_score
        best = jnp.where(better, float(gi), best)
        best_score = jnp.where(better, gs, best_score)
    masked = jnp.where(group == best, scores, -1.0)
    w1, e1 = _first_max(masked, lane)
    w2, e2 = _first_max(jnp.where(lane == e1, -2.0, masked), lane)
    den = w1 + w2
    gate1, gate2 = w1 / den, w2 / den

    onehot = jnp.where((lane == e1) | (lane == e2), 1.0, 0.0)
    run_ref[0:1, :] = run_ref[0:1, :] + jnp.sum(onehot, axis=0, keepdims=True)
    cnt_ref[...] = run_ref[...]

    info = jnp.zeros((tm, LANES), F32)
    for k, val in enumerate((e1, e2, gate1, gate2)):
        info = jnp.where(lane_i == k, val, info)
    info_ref[...] = info
    infot_ref[...] = info.T[:SUBLANES]


def merge_norm_route(hp, ya, yb, yc, h, wa, wb, wc, wo, ln_g, ln_b, rw, rb, tm=256):
    T, D = h.shape
    row = lambda i: (i, 0)
    return pl.pallas_call(
        _merge_kernel,
        grid=(T // tm,),
        in_specs=[
            pl.BlockSpec((tm, D), lambda i: (i, 0)),
            pl.BlockSpec((tm, D), lambda i: (i, 1)),
            pl.BlockSpec((tm, D), lambda i: (i, 2)),
            pl.BlockSpec((tm, ya.shape[1]), row), pl.BlockSpec((tm, yb.shape[1]), row),
            pl.BlockSpec((tm, yc.shape[1]), row), pl.BlockSpec((tm, D), row),
            _const_spec(wa.shape), _const_spec(wb.shape), _const_spec(wc.shape), _const_spec(wo.shape),
            _const_spec((1, D)), _const_spec((1, D)), _const_spec(rw.shape), _const_spec(rb.shape),
        ],
        out_specs=[pl.BlockSpec((tm, D), row), pl.BlockSpec((tm, LANES), row),
                   pl.BlockSpec((SUBLANES, tm), lambda i: (0, i)),
                   pl.BlockSpec((SUBLANES, LANES), lambda i: (0, 0))],
        out_shape=[jax.ShapeDtypeStruct((T, D), F32), jax.ShapeDtypeStruct((T, LANES), F32),
                   jax.ShapeDtypeStruct((SUBLANES, T), F32), jax.ShapeDtypeStruct((SUBLANES, LANES), F32)],
        scratch_shapes=[pltpu.VMEM((SUBLANES, LANES), F32)],
        compiler_params=_params("arbitrary"),
        name="merge_norm_route",
    )(hp, hp, hp, ya, yb, yc, h, wa, wb, wc, wo, ln_g.reshape(1, D), ln_b.reshape(1, D), rw, rb)


ROW_UNROLL = 8


def _expert_kernel(be_ref, nused_ref, nvalid_ref, first_ref, ids_ref, h_ref, w1_ref, w3_ref, w2_ref, ys_ref,
                   xbuf, ybuf, w1b, w3b, w2b, gsem, ssem, *, n_tok):
    i = pl.program_id(0)
    n_used = nused_ref[0]
    blk = xbuf.shape[1]
    slot = lax.rem(i, 2)
    last_id = ids_ref.shape[0] - 1

    def for_all_rows(fn):
        def body(c, carry):
            for k in range(ROW_UNROLL):
                fn(c * ROW_UNROLL + k)
            return carry
        lax.fori_loop(0, blk // ROW_UNROLL, body, 0)

    def start_gather(block, s):
        first = first_ref[block]

        def one(r):
            tok = lax.rem(ids_ref[jnp.minimum(first + r, last_id)], n_tok)
            pltpu.make_async_copy(h_ref.at[pl.ds(tok, 1)], xbuf.at[s, pl.ds(r, 1)], gsem.at[s]).start()
        for_all_rows(one)

    def start_scatter(block, s):
        n_rows = nvalid_ref[block]
        first = first_ref[block]

        def one(r):
            pltpu.make_async_copy(ybuf.at[s, pl.ds(r, 1)], ys_ref.at[pl.ds(ids_ref[first + r], 1)],
                                  ssem.at[s]).start()

        @pl.when(n_rows == blk)
        def _():
            for_all_rows(one)

        @pl.when(n_rows < blk)
        def _():
            def body(r, carry):
                one(r)
                return carry
            lax.fori_loop(0, n_rows, body, 0)

    def wait_gather(s):
        pltpu.make_async_copy(h_ref.at[pl.ds(0, blk)], xbuf.at[s], gsem.at[s]).wait()

    def wait_scatter(block, s):
        n_rows = nvalid_ref[block]

        @pl.when(n_rows == blk)
        def _():
            pltpu.make_async_copy(ybuf.at[s], ys_ref.at[pl.ds(0, blk)], ssem.at[s]).wait()

        @pl.when(n_rows < blk)
        def _():
            def body(r, carry):
                pltpu.make_async_copy(ybuf.at[s, pl.ds(0, 1)], ys_ref.at[pl.ds(0, 1)], ssem.at[s]).wait()
                return carry
            lax.fori_loop(0, n_rows, body, 0)

    @pl.when(i < n_used)
    def _():
        @pl.when(i == 0)
        def _():
            start_gather(0, 0)

        wait_gather(slot)

        @pl.when(i + 1 < n_used)
        def _():
            start_gather(i + 1, 1 - slot)

        @pl.when(i >= 2)
        def _():
            wait_scatter(i - 2, slot)

        @pl.when((i == 0) | (be_ref[i] != be_ref[jnp.maximum(i - 1, 0)]))
        def _():
            w1b[...] = w1_ref[0].astype(BF16)
            w3b[...] = w3_ref[0].astype(BF16)
            w2b[...] = w2_ref[0].astype(BF16)

        x = xbuf[slot].astype(BF16)
        a = jnp.dot(x, w1b[...], preferred_element_type=F32)
        b = jnp.dot(x, w3b[...], preferred_element_type=F32)
        hb = (a * jax.nn.sigmoid(a) * b).astype(BF16)
        ybuf[slot] = jnp.dot(hb, w2b[...], preferred_element_type=F32)
        start_scatter(i, slot)

        @pl.when(i == n_used - 1)
        def _():
            wait_scatter(i, slot)

            @pl.when(i >= 1)
            def _():
                wait_scatter(i - 1, 1 - slot)


def moe_experts(h1, ids, block_e, n_used, n_valid, first, w1, w3, w2, blk):
    T, D = h1.shape
    E, _, Fd = w1.shape
    nb = block_e.shape[0]
    wspec = lambda s: pl.BlockSpec((1,) + s, lambda i, be, *_: (be[i], 0, 0))
    return pl.pallas_call(
        functools.partial(_expert_kernel, n_tok=T),
        grid_spec=pltpu.PrefetchScalarGridSpec(
            num_scalar_prefetch=5,
            grid=(nb,),
            in_specs=[pl.BlockSpec(memory_space=pl.ANY), wspec((D, Fd)), wspec((D, Fd)), wspec((Fd, D))],
            out_specs=pl.BlockSpec(memory_space=pl.ANY),
            scratch_shapes=[pltpu.VMEM((2, blk, D), F32), pltpu.VMEM((2, blk, D), F32),
                            pltpu.VMEM((D, Fd), BF16), pltpu.VMEM((D, Fd), BF16), pltpu.VMEM((Fd, D), BF16),
                            pltpu.SemaphoreType.DMA((2,)), pltpu.SemaphoreType.DMA((2,))],
        ),
        out_shape=jax.ShapeDtypeStruct((2 * T, D), F32),
        compiler_params=pltpu.CompilerParams(dimension_semantics=("arbitrary",), vmem_limit_bytes=VMEM_LIMIT,
                                             has_side_effects=True),
        name="moe_experts",
    )(block_e, n_used, n_valid, first, ids, h1, w1, w3, w2)


def _combine_kernel(y1_ref, y2_ref, h_ref, info_ref, g_ref, b_ref, o_ref):
    info = info_ref[...]
    y = info[:, 2:3] * y1_ref[...] + info[:, 3:4] * y2_ref[...]
    o_ref[...] = _layer_norm(ALPHA * h_ref[...] + y, g_ref[...], b_ref[...])


def moe_combine(ys, h, info, ln_g, ln_b, tm=512):
    T, D = h.shape
    nt = T // tm
    return pl.pallas_call(
        _combine_kernel,
        grid=(nt,),
        in_specs=[pl.BlockSpec((tm, D), lambda i: (i, 0)),
                  pl.BlockSpec((tm, D), lambda i: (nt + i, 0)),
                  pl.BlockSpec((tm, D), lambda i: (i, 0)),
                  pl.BlockSpec((tm, LANES), lambda i: (i, 0)),
                  _const_spec((1, D)), _const_spec((1, D))],
        out_specs=pl.BlockSpec((tm, D), lambda i: (i, 0)),
        out_shape=jax.ShapeDtypeStruct((T, D), F32),
        compiler_params=_params("parallel"),
        name="moe_combine",
    )(ys, ys, h, info, ln_g.reshape(1, D), ln_b.reshape(1, D))


def route_tables(info_t, counts, blk, n_blocks):
    experts = info_t[0:2].reshape(-1).astype(jnp.int32)
    ids = jnp.argsort(experts, stable=True).astype(jnp.int32)
    cnt = counts[0, :N_EXPERTS].astype(jnp.int32)
    padded = (cnt + blk - 1) // blk * blk
    pends = jnp.cumsum(padded)
    block_row = jnp.arange(n_blocks, dtype=jnp.int32) * blk
    block_e = jnp.minimum(jnp.sum(block_row[:, None] >= pends[None, :], axis=1), N_EXPERTS - 1).astype(jnp.int32)
    of_block = block_e[:, None] == jnp.arange(N_EXPERTS, dtype=jnp.int32)[None, :]

    def per_block(table):
        return jnp.sum(jnp.where(of_block, table[None, :], 0), axis=1)

    row_in_expert = block_row - per_block(pends - padded)
    n_valid = jnp.clip(per_block(cnt) - row_in_expert, 0, blk).astype(jnp.int32)
    first = (per_block(jnp.cumsum(cnt) - cnt) + row_in_expert).astype(jnp.int32)
    n_used = (pends[-1:] // blk).astype(jnp.int32)
    return ids, block_e, n_used, n_valid, first


def grouped_moe_norm(h1, info, info_t, counts, w1, w3, w2, ln_g, ln_b, blk=512):
    T, D = h1.shape
    n_blocks = (2 * T) // blk + N_EXPERTS
    ids, block_e, n_used, n_valid, first = route_tables(info_t, counts, blk, n_blocks)
    ys = moe_experts(h1, ids, block_e, n_used, n_valid, first, w1, w3, w2, blk)
    return moe_combine(ys, h1, info, ln_g, ln_b)


def _in_proj_layout(d_model):
    s5w = d_model // 4
    sbw = SB_HEADS * SB_HEAD_DIM
    rqk = RET_HEADS * RET_DK
    rv = RET_HEADS * RET_DV
    sizes = (s5w, sbw, sbw, sbw, rqk, rqk, rv, rv, d_model, d_model, d_model)
    off = np.concatenate([[0], np.cumsum(sizes)]).tolist()

    def permute(w):
        u_a, q_b, k_b, v_b, q_c, k_c, v_c, g_c, ga, gb, gc = [w[:, off[i]:off[i + 1]] for i in range(11)]

        def rope_perm(m):
            m = m.reshape(d_model, RET_HEADS, 2, RET_DK // 2)
            return jnp.swapaxes(m, 1, 2).reshape(d_model, rqk)

        parts = [ga, gb, gc, rope_perm(q_c), rope_perm(k_c) * RET_DK ** -0.5, v_c, g_c,
                 q_b * SB_HEAD_DIM ** -0.5, k_b, v_b, u_a]
        return jnp.concatenate([m.astype(BF16) for m in parts], axis=1)

    o_qk = 3 * d_model
    o_v = o_qk + 2 * rqk
    o_g = o_v + rv
    o_sb = o_g + rv
    o_u = o_sb + 3 * sbw
    blocks = dict(qk=o_qk // (2 * rqk), v=o_v // rv, g=o_g // rv, sb_q=o_sb // LANES,
                  sb_k=(o_sb + sbw) // LANES, sb_v=(o_sb + 2 * sbw) // LANES, u=o_u // s5w)
    assert o_qk % (2 * rqk) == 0 and o_v % rv == 0 and o_g % rv == 0 and o_sb % LANES == 0 and o_u % s5w == 0
    return permute, blocks


def _layer(h, batch, seq, p, ln0=None):
    D = h.shape[1]
    permute, blk = _in_proj_layout(D)
    hp, h = in_projection(h, permute(p['w_in']), n_gate_cols=3 * D, ln=ln0)
    s5_tile = 512
    bblk, lam, lamk, cblk = s5_tables(p['lam_re'], p['lam_im'], p['log_dt'], p['b_re'], p['b_im'],
                                      p['c_re'], p['c_im'], s5_tile // SUBLANES)
    ya = s5_mixer(hp, blk['u'], batch, seq, bblk, lam, lamk, cblk, p['d_skip'], p['w_glu'].astype(BF16),
                  p['b_glu'], tm=s5_tile)
    yb = stick_breaking(hp, blk['sb_q'], blk['sb_k'], blk['sb_v'], batch, seq)
    yc = retention(hp, blk['qk'], blk['v'], blk['g'], batch, seq, p['gn_g'])
    rw = jnp.zeros((D, LANES), F32).at[:, :N_EXPERTS].set(p['router_w'])
    rw_hi = rw.astype(BF16)
    rw = jnp.stack([rw_hi, (rw - rw_hi.astype(F32)).astype(BF16)])
    rb = jnp.full((1, LANES), NEG_BIG, F32).at[0, :N_EXPERTS].set(p['router_b'])
    h1, info, info_t, counts = merge_norm_route(
        hp, ya, yb, yc, h, p['w_up_a'].astype(BF16), p['w_up_b'].astype(BF16), p['w_up_c'].astype(BF16),
        p['w_out'].astype(BF16), p['ln1_g'], p['ln1_b'], rw, rb)
    return grouped_moe_norm(h1, info, info_t, counts, p['moe_w1'], p['moe_w3'], p['moe_w2'],
                            p['ln2_g'], p['ln2_b'])


def kernel(x, ln0_g, ln0_b, w_in, s5_lambda_re, s5_lambda_im, s5_log_dt, s5_b_re, s5_b_im, s5_c_re, s5_c_im,
           s5_d, s5_w_glu, s5_b_glu, ret_gn_g, w_up_a, w_up_b, w_up_c, w_out, ln1_g, ln1_b, router_w, router_b,
           moe_w1, moe_w3, moe_w2, ln2_g, ln2_b):
    batch, seq, D = x.shape
    h = x.reshape(batch * seq, D)
    for l in range(w_in.shape[0]):
        p = dict(w_in=w_in[l], lam_re=s5_lambda_re[l], lam_im=s5_lambda_im[l], log_dt=s5_log_dt[l],
                 b_re=s5_b_re[l], b_im=s5_b_im[l], c_re=s5_c_re[l], c_im=s5_c_im[l], d_skip=s5_d[l],
                 w_glu=s5_w_glu[l], b_glu=s5_b_glu[l], gn_g=ret_gn_g[l], w_up_a=w_up_a[l], w_up_b=w_up_b[l],
                 w_up_c=w_up_c[l], w_out=w_out[l], ln1_g=ln1_g[l], ln1_b=ln1_b[l], router_w=router_w,
                 router_b=router_b, moe_w1=moe_w1[l], moe_w3=moe_w3[l], moe_w2=moe_w2[l], ln2_g=ln2_g[l],
                 ln2_b=ln2_b[l])
        h = _layer(h, batch, seq, p, ln0=(ln0_g, ln0_b) if l == 0 else None)
    return h.reshape(batch, seq, D)
```

```python
import functools
import math

import numpy as np
import jax
import jax.numpy as jnp
from jax import lax
from jax.experimental import pallas as pl
from jax.experimental.pallas import tpu as pltpu

F32 = jnp.float32
BF16 = jnp.bfloat16

LANES = 128
SUBLANES = 8
VMEM_LIMIT = 56 * 1024 * 1024

DEPTH = 2
CHUNK = 64
S5_GROUP_CH = 16
S5_STATE = 64
SB_HEADS = 4
SB_HEAD_DIM = 64
RET_HEADS = 4
RET_DK = 64
RET_DV = 128
ROPE_BASE = 10000.0
N_EXPERTS = 16
EXPERTS_PER_GROUP = 4
ALPHA = (2 * DEPTH) ** 0.25
LN_EPS = 1e-5

GATE_OFF = 0
NEG_BIG = -1e30


def _params(*sem):
    return pltpu.CompilerParams(dimension_semantics=sem, vmem_limit_bytes=VMEM_LIMIT)


def _const_spec(shape):
    zeros = (0,) * len(shape)
    return pl.BlockSpec(shape, lambda *_: zeros, pipeline_mode=pl.Buffered(1))


def _layer_norm(x, g, b):
    mu = jnp.mean(x, axis=-1, keepdims=True)
    xc = x - mu
    var = jnp.mean(xc * xc, axis=-1, keepdims=True)
    return xc * lax.rsqrt(var + LN_EPS) * g + b


def _inproj_kernel(*refs, col_chunk, n_gate_cols, with_ln):
    if with_ln:
        x_ref, g_ref, b_ref, w_ref, o_ref, h_ref = refs
        x = _layer_norm(x_ref[...], g_ref[...], b_ref[...])
        h_ref[...] = x
    else:
        x_ref, w_ref, o_ref = refs
        x = x_ref[...]
    xb = x.astype(BF16)
    n = w_ref.shape[1]
    for c in range(0, n, col_chunk):
        acc = jnp.dot(xb, w_ref[:, c:c + col_chunk], preferred_element_type=F32)
        if c < n_gate_cols:
            acc = jax.nn.sigmoid(acc)
        o_ref[:, c:c + col_chunk] = acc.astype(BF16)


def in_projection(h, w_bf16, n_gate_cols, ln=None, tm=512, col_chunk=512):
    T, D = h.shape
    N = w_bf16.shape[1]
    assert n_gate_cols % col_chunk == 0
    rows = pl.BlockSpec((tm, D), lambda i: (i, 0))
    proj = pl.BlockSpec((tm, N), lambda i: (i, 0))
    args, in_specs = [h], [rows]
    out_specs, out_shape = [proj], [jax.ShapeDtypeStruct((T, N), BF16)]
    if ln is not None:
        args += [ln[0].reshape(1, D), ln[1].reshape(1, D)]
        in_specs += [_const_spec((1, D)), _const_spec((1, D))]
        out_specs.append(rows)
        out_shape.append(jax.ShapeDtypeStruct((T, D), F32))
    outs = pl.pallas_call(
        functools.partial(_inproj_kernel, col_chunk=col_chunk, n_gate_cols=n_gate_cols, with_ln=ln is not None),
        grid=(T // tm,),
        in_specs=in_specs + [_const_spec((D, N))],
        out_specs=out_specs,
        out_shape=out_shape,
        compiler_params=_params("parallel"),
        name="in_proj",
    )(*args, w_bf16)
    return (outs[0], outs[1]) if ln is not None else (outs[0], h)


def _s5_kernel(u_ref, bblk_ref, lam_ref, lamk_ref, cblk_ref, d_ref, wglu_ref, bglu_ref,
               o_ref, io_ref, s_ref, carry_ref, *, n_state):
    tm, width = u_ref.shape
    K = tm // SUBLANES
    ncol = n_state // LANES
    nio = width // LANES

    @pl.when(pl.program_id(1) == 0)
    def _():
        carry_ref[...] = jnp.zeros_like(carry_ref)

    def sub_chunk_rows(k):
        return pl.ds(k, SUBLANES, stride=K)

    for c in range(nio):
        io_ref[c] = u_ref[:, c * LANES:(c + 1) * LANES].astype(F32)
    u = jnp.concatenate(
        [jnp.concatenate([io_ref[c, sub_chunk_rows(k), :] for k in range(K)], axis=0) for c in range(nio)],
        axis=1)
    s_ref[...] = jnp.dot(u.astype(BF16), bblk_ref[...], preferred_element_type=F32)

    def cols(j):
        return pl.ds(j * LANES, LANES), pl.ds(n_state + j * LANES, LANES)

    def rows(k):
        return pl.ds(pl.multiple_of(k * SUBLANES, SUBLANES), SUBLANES)

    a_re = [jnp.broadcast_to(lam_ref[0:1, pl.ds(j * LANES, LANES)], (SUBLANES, LANES)) for j in range(ncol)]
    a_im = [jnp.broadcast_to(lam_ref[1:2, pl.ds(j * LANES, LANES)], (SUBLANES, LANES)) for j in range(ncol)]

    def pass1(k, st):
        out = []
        for j in range(ncol):
            cr, ci = cols(j)
            sr, si = st[2 * j], st[2 * j + 1]
            nr = a_re[j] * sr - a_im[j] * si + s_ref[rows(k), cr]
            ni = a_re[j] * si + a_im[j] * sr + s_ref[rows(k), ci]
            s_ref[rows(k), cr] = nr
            s_ref[rows(k), ci] = ni
            out += [nr, ni]
        return tuple(out)

    zero = jnp.zeros((SUBLANES, LANES), F32)
    ends = lax.fori_loop(0, K, pass1, (zero,) * (2 * ncol))

    carries = []
    for j in range(ncol):
        cr, ci = cols(j)
        kr = lamk_ref[0:1, pl.ds(j * LANES, LANES)]
        ki = lamk_ref[1:2, pl.ds(j * LANES, LANES)]
        er, ei = ends[2 * j], ends[2 * j + 1]
        c_r = [carry_ref[0:1, cr]]
        c_i = [carry_ref[0:1, ci]]
        for r in range(1, SUBLANES + 1):
            pr, pi = c_r[-1], c_i[-1]
            c_r.append(kr * pr - ki * pi + er[r - 1:r, :])
            c_i.append(kr * pi + ki * pr + ei[r - 1:r, :])
        carry_ref[0:1, cr] = c_r[SUBLANES]
        carry_ref[0:1, ci] = c_i[SUBLANES]
        carries += [jnp.concatenate(c_r[:SUBLANES], axis=0), jnp.concatenate(c_i[:SUBLANES], axis=0)]

    def pass2(k, st):
        out = []
        for j in range(ncol):
            cr, ci = cols(j)
            dr, di = st[2 * j], st[2 * j + 1]
            nr = a_re[j] * dr - a_im[j] * di
            ni = a_re[j] * di + a_im[j] * dr
            s_ref[rows(k), cr] = s_ref[rows(k), cr] + nr
            s_ref[rows(k), ci] = s_ref[rows(k), ci] + ni
            out += [nr, ni]
        return tuple(out)

    lax.fori_loop(0, K, pass2, tuple(carries))

    y = jnp.dot(s_ref[...].astype(BF16), cblk_ref[...], preferred_element_type=F32)
    y = y + d_ref[...] * u
    y = jax.nn.gelu(y, approximate=True)
    gate = jnp.dot(y.astype(BF16), wglu_ref[...], preferred_element_type=F32) + bglu_ref[...]
    out = y * jax.nn.sigmoid(gate)
    for c in range(nio):
        for k in range(K):
            io_ref[c, sub_chunk_rows(k), :] = out[k * SUBLANES:(k + 1) * SUBLANES, c * LANES:(c + 1) * LANES]
    o_ref[...] = jnp.concatenate([io_ref[c] for c in range(nio)], axis=1).astype(BF16)


def s5_mixer(hp, u_col_block, batch, seq, bblk, lam, lamk, cblk, d_skip, w_glu, b_glu, tm=512):
    width = bblk.shape[0]
    n_state = bblk.shape[1] // 2
    nt = seq // tm
    return pl.pallas_call(
        functools.partial(_s5_kernel, n_state=n_state),
        grid=(batch, nt),
        in_specs=[
            pl.BlockSpec((tm, width), lambda b, t: (b * nt + t, u_col_block)),
            _const_spec(bblk.shape), _const_spec(lam.shape), _const_spec(lamk.shape),
            _const_spec(cblk.shape), _const_spec((1, width)), _const_spec(w_glu.shape),
            _const_spec((1, width)),
        ],
        out_specs=pl.BlockSpec((tm, width), lambda b, t: (b * nt + t, 0)),
        out_shape=jax.ShapeDtypeStruct((batch * seq, width), BF16),
        scratch_shapes=[pltpu.VMEM((width // LANES, tm, LANES), F32),
                        pltpu.VMEM((tm, 2 * n_state), F32),
                        pltpu.VMEM((SUBLANES, 2 * n_state), F32)],
        compiler_params=_params("parallel", "arbitrary"),
        name="s5_mixer",
    )(hp, bblk, lam, lamk, cblk, d_skip.reshape(1, width), w_glu, b_glu.reshape(1, width))


def s5_tables(lam_re, lam_im, log_dt, b_re, b_im, c_re, c_im, sub_chunk):
    G, P = lam_re.shape
    dt = jnp.exp(log_dt.astype(F32))[:, None]

    def lam_pow(n):
        mag = jnp.exp(lam_re * dt * n)
        return mag * jnp.cos(lam_im * dt * n), mag * jnp.sin(lam_im * dt * n)

    lr, li = lam_pow(1.0)
    kr, ki = lam_pow(float(sub_chunk))
    nr, ni = lr - 1.0, li
    den = lam_re * lam_re + lam_im * lam_im
    zr = (nr * lam_re + ni * lam_im) / den
    zi = (ni * lam_re - nr * lam_im) / den
    bbr = b_re * zr[..., None] - b_im * zi[..., None]
    bbi = b_re * zi[..., None] + b_im * zr[..., None]
    eye = jnp.eye(G, dtype=F32)
    Cg = b_re.shape[-1]

    def in_blk(m):
        return jnp.einsum('gpc,gh->gchp', m, eye).reshape(G * Cg, G * P)

    def out_blk(m):
        return jnp.einsum('gcp,gh->gphc', m, eye).reshape(G * P, G * Cg)

    bblk = jnp.concatenate([in_blk(bbr), in_blk(bbi)], axis=1).astype(BF16)
    cblk = jnp.concatenate([out_blk(c_re), -out_blk(c_im)], axis=0).astype(BF16)
    lam = jnp.stack([lr.reshape(-1), li.reshape(-1)])
    lamk = jnp.stack([kr.reshape(-1), ki.reshape(-1)])
    return bblk, lam, lamk, cblk


def _log_sigmoid_pair(z):
    soft = jnp.log(1.0 + jnp.exp(-jnp.abs(z)))
    lp = jnp.minimum(z, 0.0) - soft
    return lp, lp - z


SB_EXIT = -110.0


def _sb_kernel(q_ref, k_ref, v_ref, tri_ref, o_ref, acc_ref, carry_ref, *, blk, group):
    i = pl.program_id(2)
    q = q_ref[...]
    lane = lax.broadcasted_iota(jnp.int32, (1, LANES), 1)
    head_lanes = [lane < SB_HEAD_DIM, lane >= SB_HEAD_DIM]
    zero = jnp.zeros((), BF16)
    qh = [jnp.where(m, q, zero) for m in head_lanes]
    row = lax.broadcasted_iota(jnp.int32, (blk, blk), 0)
    col = lax.broadcasted_iota(jnp.int32, (blk, blk), 1)
    strictly_earlier = col < row

    acc_ref[...] = jnp.zeros_like(acc_ref)
    carry_ref[...] = jnp.zeros_like(carry_ref)

    def group_step(g, first):
        zs, lps, rs, vs = [], [], [], []
        for u in range(group):
            j = i - g * group - u
            start = pl.multiple_of(jnp.maximum(j, 0) * blk, blk)
            kj = k_ref[pl.ds(start, blk), :]
            vj = v_ref[pl.ds(start, blk), :]
            for h in range(2):
                zs.append(lax.dot_general(qh[h], kj, (((1,), (1,)), ((), ())), preferred_element_type=F32))
                vs.append(jnp.where(head_lanes[h] & (j >= 0), vj, zero))
        for n, z in enumerate(zs):
            lp, l1m = _log_sigmoid_pair(z)
            if first and n < 2:
                l1m = jnp.where(strictly_earlier, l1m, 0.0)
            lps.append(lp)
            rs.append(jnp.dot(l1m.astype(BF16), tri_ref[...], preferred_element_type=F32))
        ws = [None] * len(zs)
        top = None
        for h in range(2):
            cum = carry_ref[h]
            for u in range(group):
                n = 2 * u + h
                w = jnp.exp(lps[n] + rs[n][:, :blk] + cum)
                if first and u == 0:
                    w = jnp.where(strictly_earlier, w, 0.0)
                ws[n] = w.astype(BF16)
                cum = cum + rs[n][:, blk:]
            carry_ref[h] = cum
            top = cum if top is None else jnp.maximum(top, cum)
        acc = acc_ref[...]
        for w, v in zip(ws, vs):
            acc = acc + jnp.dot(w, v, preferred_element_type=F32)
        acc_ref[...] = acc
        return jnp.max(top)

    def more(state):
        g, top = state
        return (g * group <= i) & (top > SB_EXIT)

    def step(state):
        g, _ = state
        return g + 1, group_step(g, False)

    lax.while_loop(more, step, (1, group_step(0, True)))
    o_ref[...] = acc_ref[...].astype(BF16)


def stick_breaking(hp, q_blk0, k_blk0, v_blk0, batch, seq, blk=128, group=3):
    assert blk == LANES
    nq = seq // blk
    n_pairs = SB_HEADS * SB_HEAD_DIM // LANES
    r = np.arange(blk)
    tri = np.concatenate([(r[:, None] > r[None, :]), np.ones((blk, blk), bool)], axis=1)
    tri = jnp.asarray(tri, BF16)
    return pl.pallas_call(
        functools.partial(_sb_kernel, blk=blk, group=group),
        grid=(batch, n_pairs, nq),
        in_specs=[
            pl.BlockSpec((blk, LANES), lambda b, p, i: (b * nq + i, q_blk0 + p)),
            pl.BlockSpec((seq, LANES), lambda b, p, i: (b, k_blk0 + p)),
            pl.BlockSpec((seq, LANES), lambda b, p, i: (b, v_blk0 + p)),
            _const_spec(tri.shape),
        ],
        out_specs=pl.BlockSpec((blk, LANES), lambda b, p, i: (b * nq + i, p)),
        out_shape=jax.ShapeDtypeStruct((batch * seq, n_pairs * LANES), BF16),
        scratch_shapes=[pltpu.VMEM((blk, LANES), F32), pltpu.VMEM((2, blk, blk), F32)],
        compiler_params=_params("parallel", "parallel", "arbitrary"),
        name="stick_breaking",
    )(hp, hp, hp, tri)


def _ret_kernel(qk_ref, v_ref, g_ref, cos_ref, sin_ref, mask_ref, rdec_ref, cdec_ref, tdec_ref, gn_ref,
                o_ref, state_ref):
    @pl.when(pl.program_id(1) == 0)
    def _():
        state_ref[...] = jnp.zeros_like(state_ref)

    half = RET_HEADS * RET_DK // 2
    qk = qk_ref[...].astype(F32)
    cos, sin = cos_ref[...], sin_ref[...]

    def rope(t):
        t1, t2 = t[:, :half], t[:, half:]
        return jnp.concatenate([t1 * cos - t2 * sin, t1 * sin + t2 * cos], axis=1)

    qr = rope(qk[:, :2 * half])
    kr = rope(qk[:, 2 * half:])
    kb = kr.astype(BF16)
    lane = lax.broadcasted_iota(jnp.int32, (1, 2 * half), 1)
    head_of_lane = (lane % half) // (RET_DK // 2)
    for h in range(RET_HEADS):
        qh = jnp.where(head_of_lane == h, qr, 0.0)
        vh = v_ref[:, h * RET_DV:(h + 1) * RET_DV]
        scores = lax.dot_general(qh.astype(BF16), kb, (((1,), (1,)), ((), ())), preferred_element_type=F32)
        scores = scores * mask_ref[h]
        o = jnp.dot(scores.astype(BF16), vh, preferred_element_type=F32)
        o = o + jnp.dot((qh * rdec_ref[h]).astype(BF16), state_ref[h].astype(BF16), preferred_element_type=F32)
        kv = lax.dot_general((kr * cdec_ref[h]).astype(BF16), vh, (((0,), (0,)), ((), ())),
                             preferred_element_type=F32)
        state_ref[h] = tdec_ref[h] * state_ref[h] + kv
        mu = jnp.mean(o, axis=-1, keepdims=True)
        oc = o - mu
        var = jnp.mean(oc * oc, axis=-1, keepdims=True)
        on = oc * lax.rsqrt(var + LN_EPS) * gn_ref[:, h * RET_DV:(h + 1) * RET_DV]
        g = g_ref[:, h * RET_DV:(h + 1) * RET_DV].astype(F32)
        o_ref[:, h * RET_DV:(h + 1) * RET_DV] = (g * jax.nn.sigmoid(g) * on).astype(BF16)


def retention_tables(seq, tile):
    halfdim = RET_DK // 2
    inv_freq = ROPE_BASE ** (-jnp.arange(halfdim, dtype=F32) / halfdim)
    ang = jnp.arange(seq, dtype=F32)[:, None] * inv_freq[None, :]
    cos = jnp.tile(jnp.cos(ang), (1, RET_HEADS))
    sin = jnp.tile(jnp.sin(ang), (1, RET_HEADS))
    log_gamma = jnp.log(1.0 - 2.0 ** (-5.0 - jnp.arange(RET_HEADS, dtype=F32)))
    t = jnp.arange(tile, dtype=F32)
    same_or_earlier_chunk = (t[None, :] // CHUNK) <= (t[:, None] // CHUNK)
    mask = jnp.exp(log_gamma[:, None, None] * jnp.abs(t[:, None] - t[None, :]))
    mask = jnp.where(same_or_earlier_chunk[None], mask, 0.0)
    rdec = jnp.exp(log_gamma[:, None, None] * (t[None, :, None] + 1.0))
    cdec = jnp.exp(log_gamma[:, None, None] * (tile - 1.0 - t[None, :, None]))
    tdec = jnp.broadcast_to(jnp.exp(log_gamma * tile)[:, None, None], (RET_HEADS, 1, RET_DV))
    return cos, sin, mask, rdec, cdec, tdec


def retention(hp, qk_blk, v_blk, g_blk, batch, seq, gn_g, tile=256):
    cos, sin, mask, rdec, cdec, tdec = retention_tables(seq, tile)
    nt = seq // tile
    qkw = 2 * RET_HEADS * RET_DK
    vw = RET_HEADS * RET_DV
    half = RET_HEADS * RET_DK // 2
    return pl.pallas_call(
        _ret_kernel,
        grid=(batch, nt),
        in_specs=[
            pl.BlockSpec((tile, qkw), lambda b, t: (b * nt + t, qk_blk)),
            pl.BlockSpec((tile, vw), lambda b, t: (b * nt + t, v_blk)),
            pl.BlockSpec((tile, vw), lambda b, t: (b * nt + t, g_blk)),
            pl.BlockSpec((tile, half), lambda b, t: (t, 0)),
            pl.BlockSpec((tile, half), lambda b, t: (t, 0)),
            _const_spec(mask.shape), _const_spec(rdec.shape), _const_spec(cdec.shape),
            _const_spec(tdec.shape), _const_spec((1, vw)),
        ],
        out_specs=pl.BlockSpec((tile, vw), lambda b, t: (b * nt + t, 0)),
        out_shape=jax.ShapeDtypeStruct((batch * seq, vw), BF16),
        scratch_shapes=[pltpu.VMEM((RET_HEADS, 2 * half, RET_DV), F32)],
        compiler_params=_params("parallel", "arbitrary"),
        name="retention",
    )(hp, hp, hp, cos, sin, mask, rdec, cdec, tdec, gn_g.reshape(1, vw))


def _first_max(vals, lane):
    m = jnp.max(vals, axis=-1, keepdims=True)
    idx = jnp.min(jnp.where(vals == m, lane, float(LANES)), axis=-1, keepdims=True)
    return m, idx


def _merge_kernel(ga_ref, gb_ref, gc_ref, ya_ref, yb_ref, yc_ref, h_ref, wa_ref, wb_ref, wc_ref, wo_ref,
                  g1_ref, b1_ref, rw_ref, rb_ref, h1_ref, info_ref, infot_ref, cnt_ref, run_ref):
    @pl.when(pl.program_id(0) == 0)
    def _():
        run_ref[...] = jnp.zeros_like(run_ref)

    def branch(g_ref, y_ref, w_ref):
        up = jnp.dot(y_ref[...], w_ref[...], preferred_element_type=F32)
        return g_ref[...].astype(F32) * up

    merged = branch(ga_ref, ya_ref, wa_ref) + branch(gb_ref, yb_ref, wb_ref) + branch(gc_ref, yc_ref, wc_ref)
    y = jnp.dot(merged.astype(BF16), wo_ref[...], preferred_element_type=F32)
    h1 = _layer_norm(ALPHA * h_ref[...] + y, g1_ref[...], b1_ref[...])
    h1_ref[...] = h1

    h_hi = h1.astype(BF16)
    h_lo = (h1 - h_hi.astype(F32)).astype(BF16)
    logits = (jnp.dot(h_hi, rw_ref[0], preferred_element_type=F32)
              + jnp.dot(h_lo, rw_ref[0], preferred_element_type=F32)
              + jnp.dot(h_hi, rw_ref[1], preferred_element_type=F32)) + rb_ref[...]
    tm = logits.shape[0]
    lane_i = lax.broadcasted_iota(jnp.int32, (tm, LANES), 1)
    lane = lane_i.astype(F32)
    ex = jnp.exp(logits - jnp.max(logits, axis=-1, keepdims=True))
    scores = ex / jnp.sum(ex, axis=-1, keepdims=True)
    group = (lane_i // EXPERTS_PER_GROUP).astype(F32)
    best = jnp.zeros((tm, 1), F32)
    best_score = jnp.full((tm, 1), -1.0, F32)
    for gi in range(N_EXPERTS // EXPERTS_PER_GROUP):
        sg = jnp.where(group == float(gi), scores, -1.0)
        m1, i1 = _first_max(sg, lane)
        m2 = jnp.max(jnp.where(lane == i1, -1.0, sg), axis=-1, keepdims=True)
        gs = m1 + m2
        better = gs > best_score
        best = jnp.where(better, float(gi), best)
        best_score = jnp.where(better, gs, best_score)
    masked = jnp.where(group == best, scores, -1.0)
    w1, e1 = _first_max(masked, lane)
    w2, e2 = _first_max(jnp.where(lane == e1, -2.0, masked), lane)
    den = w1 + w2
    gate1, gate2 = w1 / den, w2 / den

    onehot = jnp.where((lane == e1) | (lane == e2), 1.0, 0.0)
    run_ref[0:1, :] = run_ref[0:1, :] + jnp.sum(onehot, axis=0, keepdims=True)
    cnt_ref[...] = run_ref[...]

    info = jnp.zeros((tm, LANES), F32)
    for k, val in enumerate((e1, e2, gate1, gate2)):
        info = jnp.where(lane_i == k, val, info)
    info_ref[...] = info
    infot_ref[...] = info.T[:SUBLANES]


def merge_norm_route(hp, ya, yb, yc, h, wa, wb, wc, wo, ln_g, ln_b, rw, rb, tm=256):
    T, D = h.shape
    row = lambda i: (i, 0)
    return pl.pallas_call(
        _merge_kernel,
        grid=(T // tm,),
        in_specs=[
            pl.BlockSpec((tm, D), lambda i: (i, 0)),
            pl.BlockSpec((tm, D), lambda i: (i, 1)),
            pl.BlockSpec((tm, D), lambda i: (i, 2)),
            pl.BlockSpec((tm, ya.shape[1]), row), pl.BlockSpec((tm, yb.shape[1]), row),
            pl.BlockSpec((tm, yc.shape[1]), row), pl.BlockSpec((tm, D), row),
            _const_spec(wa.shape), _const_spec(wb.shape), _const_spec(wc.shape), _const_spec(wo.shape),
            _const_spec((1, D)), _const_spec((1, D)), _const_spec(rw.shape), _const_spec(rb.shape),
        ],
        out_specs=[pl.BlockSpec((tm, D), row), pl.BlockSpec((tm, LANES), row),
                   pl.BlockSpec((SUBLANES, tm), lambda i: (0, i)),
                   pl.BlockSpec((SUBLANES, LANES), lambda i: (0, 0))],
        out_shape=[jax.ShapeDtypeStruct((T, D), F32), jax.ShapeDtypeStruct((T, LANES), F32),
                   jax.ShapeDtypeStruct((SUBLANES, T), F32), jax.ShapeDtypeStruct((SUBLANES, LANES), F32)],
        scratch_shapes=[pltpu.VMEM((SUBLANES, LANES), F32)],
        compiler_params=_params("arbitrary"),
        name="merge_norm_route",
    )(hp, hp, hp, ya, yb, yc, h, wa, wb, wc, wo, ln_g.reshape(1, D), ln_b.reshape(1, D), rw, rb)


ROW_UNROLL = 8


def _expert_kernel(be_ref, nused_ref, nvalid_ref, first_ref, ids_ref, h_ref, w1_ref, w3_ref, w2_ref, ys_ref,
                   xa, xb, ya, yb, w1b, w3b, w2b, gsem, ssem, *, n_tok):
    j = pl.program_id(0)
    n_used = nused_ref[0]
    n_blocks = nvalid_ref.shape[0]
    blk = xa.shape[0]
    last_id = ids_ref.shape[0] - 1
    dump_row = 2 * n_tok

    def assignment(first, r):
        return ids_ref[jnp.minimum(first + r, last_id)]

    def gather_copy(first, r, x_dst, sem):
        tok = lax.rem(assignment(first, r), n_tok)
        return pltpu.make_async_copy(h_ref.at[pl.ds(tok, 1)], x_dst.at[pl.ds(r, 1)], sem)

    def scatter_copy(first, n_rows, r, y_src, sem):
        dst = jnp.where(r < n_rows, assignment(first, r), dump_row + r)
        return pltpu.make_async_copy(y_src.at[pl.ds(r, 1)], ys_ref.at[pl.ds(dst, 1)], sem)

    def wait_gathered(x_dst, sem):
        pltpu.make_async_copy(h_ref.at[pl.ds(0, blk)], x_dst, sem).wait()

    def wait_scattered(y_src, sem):
        pltpu.make_async_copy(y_src, ys_ref.at[pl.ds(0, blk)], sem).wait()

    def rolled(start_row_copy):
        def body(c, carry):
            for k in range(ROW_UNROLL):
                start_row_copy(c * ROW_UNROLL + k)
            return carry
        lax.fori_loop(0, blk // ROW_UNROLL, body, 0)

    def block_step(b, x_own, x_other, y_own, y_other, g_own, g_other, s_own, s_other, leads_pair):
        @pl.when(b < n_used)
        def _():
            if leads_pair:
                @pl.when(b == 0)
                def _():
                    first = first_ref[0]
                    rolled(lambda r: gather_copy(first, r, x_own, g_own).start())
                    y_other[...] = jnp.zeros_like(y_other)

            wait_gathered(x_own, g_own)

            @pl.when(b >= 1)
            def _():
                wait_scattered(y_own, s_own)

            if leads_pair:
                @pl.when((b == 0) | (be_ref[b] != be_ref[jnp.maximum(b - 2, 0)]))
                def _():
                    w1b[...] = w1_ref[0, 0].astype(BF16)
                    w3b[...] = w3_ref[0, 0].astype(BF16)
                    w2b[...] = w2_ref[0, 0].astype(BF16)

            first_next = first_ref[jnp.minimum(b + 1, n_blocks - 1)]
            prev = jnp.maximum(b - 1, 0)
            first_prev = first_ref[prev]
            rows_prev = jnp.where(b >= 1, nvalid_ref[prev], 0)
            for r in range(blk):
                gather_copy(first_next, r, x_other, g_other).start()
                scatter_copy(first_prev, rows_prev, r, y_other, s_other).start()
            x = x_own[...].astype(BF16)
            a = jnp.dot(x, w1b[...], preferred_element_type=F32)
            g = jnp.dot(x, w3b[...], preferred_element_type=F32)
            hb = (a * jax.nn.sigmoid(a) * g).astype(BF16)
            y_own[...] = jnp.dot(hb, w2b[...], preferred_element_type=F32)

            @pl.when(b == n_used - 1)
            def _():
                first = first_ref[b]
                n_rows = nvalid_ref[b]
                wait_scattered(y_other, s_other)
                rolled(lambda r: scatter_copy(first, n_rows, r, y_own, s_own).start())
                wait_scattered(y_own, s_own)
                wait_gathered(x_other, g_other)

    block_step(2 * j, xa, xb, ya, yb, gsem.at[0], gsem.at[1], ssem.at[0], ssem.at[1], True)
    block_step(2 * j + 1, xb, xa, yb, ya, gsem.at[1], gsem.at[0], ssem.at[1], ssem.at[0], False)


def moe_experts(h1, ids, block_e, n_used, n_valid, first, w1, w3, w2, layer, blk):
    T, D = h1.shape
    Fd = w1.shape[-1]
    nb = block_e.shape[0]
    assert nb % 2 == 0
    wspec = lambda s: pl.BlockSpec((1, 1) + s, lambda j, be, *_: (layer, be[2 * j], 0, 0))
    return pl.pallas_call(
        functools.partial(_expert_kernel, n_tok=T),
        grid_spec=pltpu.PrefetchScalarGridSpec(
            num_scalar_prefetch=5,
            grid=(nb // 2,),
            in_specs=[pl.BlockSpec(memory_space=pl.ANY), wspec((D, Fd)), wspec((D, Fd)), wspec((Fd, D))],
            out_specs=pl.BlockSpec(memory_space=pl.ANY),
            scratch_shapes=[pltpu.VMEM((blk, D), F32), pltpu.VMEM((blk, D), F32),
                            pltpu.VMEM((blk, D), F32), pltpu.VMEM((blk, D), F32),
                            pltpu.VMEM((D, Fd), BF16), pltpu.VMEM((D, Fd), BF16), pltpu.VMEM((Fd, D), BF16),
                            pltpu.SemaphoreType.DMA((2,)), pltpu.SemaphoreType.DMA((2,))],
        ),
        out_shape=jax.ShapeDtypeStruct((2 * T + blk, D), F32),
        compiler_params=pltpu.CompilerParams(dimension_semantics=("arbitrary",), vmem_limit_bytes=VMEM_LIMIT,
                                             has_side_effects=True),
        name="moe_experts",
    )(block_e, n_used, n_valid, first, ids, h1, w1, w3, w2)


def _combine_kernel(y1_ref, y2_ref, h_ref, info_ref, g_ref, b_ref, o_ref):
    info = info_ref[...]
    y = info[:, 2:3] * y1_ref[...] + info[:, 3:4] * y2_ref[...]
    o_ref[...] = _layer_norm(ALPHA * h_ref[...] + y, g_ref[...], b_ref[...])


def moe_combine(ys, h, info, ln_g, ln_b, tm=512):
    T, D = h.shape
    nt = T // tm
    return pl.pallas_call(
        _combine_kernel,
        grid=(nt,),
        in_specs=[pl.BlockSpec((tm, D), lambda i: (i, 0)),
                  pl.BlockSpec((tm, D), lambda i: (nt + i, 0)),
                  pl.BlockSpec((tm, D), lambda i: (i, 0)),
                  pl.BlockSpec((tm, LANES), lambda i: (i, 0)),
                  _const_spec((1, D)), _const_spec((1, D))],
        out_specs=pl.BlockSpec((tm, D), lambda i: (i, 0)),
        out_shape=jax.ShapeDtypeStruct((T, D), F32),
        compiler_params=_params("parallel"),
        name="moe_combine",
    )(ys, ys, h, info, ln_g.reshape(1, D), ln_b.reshape(1, D))


def route_tables(info_t, counts, blk, n_blocks):
    experts = info_t[0:2].reshape(-1).astype(jnp.int32)
    ids = jnp.argsort(experts, stable=True).astype(jnp.int32)
    cnt = counts[0, :N_EXPERTS].astype(jnp.int32)
    pair = 2 * blk
    padded = (cnt + pair - 1) // pair * pair
    pends = jnp.cumsum(padded)
    block_row = jnp.arange(n_blocks, dtype=jnp.int32) * blk
    block_e = jnp.minimum(jnp.sum(block_row[:, None] >= pends[None, :], axis=1), N_EXPERTS - 1).astype(jnp.int32)
    of_block = block_e[:, None] == jnp.arange(N_EXPERTS, dtype=jnp.int32)[None, :]

    def per_block(table):
        return jnp.sum(jnp.where(of_block, table[None, :], 0), axis=1)

    row_in_expert = block_row - per_block(pends - padded)
    n_valid = jnp.clip(per_block(cnt) - row_in_expert, 0, blk).astype(jnp.int32)
    first = (per_block(jnp.cumsum(cnt) - cnt) + row_in_expert).astype(jnp.int32)
    n_used = (pends[-1:] // blk).astype(jnp.int32)
    return ids, block_e, n_used, n_valid, first


def grouped_moe_norm(h1, info, info_t, counts, w1, w3, w2, layer, ln_g, ln_b, blk=512):
    T, D = h1.shape
    n_blocks = (2 * T) // blk + 2 * N_EXPERTS
    ids, block_e, n_used, n_valid, first = route_tables(info_t, counts, blk, n_blocks)
    ys = moe_experts(h1, ids, block_e, n_used, n_valid, first, w1, w3, w2, layer, blk)
    return moe_combine(ys, h1, info, ln_g, ln_b)


def _in_proj_layout(d_model):
    s5w = d_model // 4
    sbw = SB_HEADS * SB_HEAD_DIM
    rqk = RET_HEADS * RET_DK
    rv = RET_HEADS * RET_DV
    sizes = (s5w, sbw, sbw, sbw, rqk, rqk, rv, rv, d_model, d_model, d_model)
    off = np.concatenate([[0], np.cumsum(sizes)]).tolist()

    def permute(w):
        u_a, q_b, k_b, v_b, q_c, k_c, v_c, g_c, ga, gb, gc = [w[:, off[i]:off[i + 1]] for i in range(11)]

        def rope_perm(m):
            m = m.reshape(d_model, RET_HEADS, 2, RET_DK // 2)
            return jnp.swapaxes(m, 1, 2).reshape(d_model, rqk)

        parts = [ga, gb, gc, rope_perm(q_c), rope_perm(k_c) * RET_DK ** -0.5, v_c, g_c,
                 q_b * SB_HEAD_DIM ** -0.5, k_b, v_b, u_a]
        return jnp.concatenate([m.astype(BF16) for m in parts], axis=1)

    o_qk = 3 * d_model
    o_v = o_qk + 2 * rqk
    o_g = o_v + rv
    o_sb = o_g + rv
    o_u = o_sb + 3 * sbw
    blocks = dict(qk=o_qk // (2 * rqk), v=o_v // rv, g=o_g // rv, sb_q=o_sb // LANES,
                  sb_k=(o_sb + sbw) // LANES, sb_v=(o_sb + 2 * sbw) // LANES, u=o_u // s5w)
    assert o_qk % (2 * rqk) == 0 and o_v % rv == 0 and o_g % rv == 0 and o_sb % LANES == 0 and o_u % s5w == 0
    return permute, blocks


def _layer(h, batch, seq, p, ln0=None):
    D = h.shape[1]
    permute, blk = _in_proj_layout(D)
    hp, h = in_projection(h, permute(p['w_in']), n_gate_cols=3 * D, ln=ln0)
    s5_tile = 512
    bblk, lam, lamk, cblk = s5_tables(p['lam_re'], p['lam_im'], p['log_dt'], p['b_re'], p['b_im'],
                                      p['c_re'], p['c_im'], s5_tile // SUBLANES)
    ya = s5_mixer(hp, blk['u'], batch, seq, bblk, lam, lamk, cblk, p['d_skip'], p['w_glu'].astype(BF16),
                  p['b_glu'], tm=s5_tile)
    yb = stick_breaking(hp, blk['sb_q'], blk['sb_k'], blk['sb_v'], batch, seq)
    yc = retention(hp, blk['qk'], blk['v'], blk['g'], batch, seq, p['gn_g'])
    rw = jnp.zeros((D, LANES), F32).at[:, :N_EXPERTS].set(p['router_w'])
    rw_hi = rw.astype(BF16)
    rw = jnp.stack([rw_hi, (rw - rw_hi.astype(F32)).astype(BF16)])
    rb = jnp.full((1, LANES), NEG_BIG, F32).at[0, :N_EXPERTS].set(p['router_b'])
    h1, info, info_t, counts = merge_norm_route(
        hp, ya, yb, yc, h, p['w_up_a'].astype(BF16), p['w_up_b'].astype(BF16), p['w_up_c'].astype(BF16),
        p['w_out'].astype(BF16), p['ln1_g'], p['ln1_b'], rw, rb)
    return grouped_moe_norm(h1, info, info_t, counts, p['moe_w1'], p['moe_w3'], p['moe_w2'], p['layer'],
                            p['ln2_g'], p['ln2_b'])


def kernel(x, ln0_g, ln0_b, w_in, s5_lambda_re, s5_lambda_im, s5_log_dt, s5_b_re, s5_b_im, s5_c_re, s5_c_im,
           s5_d, s5_w_glu, s5_b_glu, ret_gn_g, w_up_a, w_up_b, w_up_c, w_out, ln1_g, ln1_b, router_w, router_b,
           moe_w1, moe_w3, moe_w2, ln2_g, ln2_b):
    batch, seq, D = x.shape
    h = x.reshape(batch * seq, D)
    for l in range(w_in.shape[0]):
        p = dict(w_in=w_in[l], lam_re=s5_lambda_re[l], lam_im=s5_lambda_im[l], log_dt=s5_log_dt[l],
                 b_re=s5_b_re[l], b_im=s5_b_im[l], c_re=s5_c_re[l], c_im=s5_c_im[l], d_skip=s5_d[l],
                 w_glu=s5_w_glu[l], b_glu=s5_b_glu[l], gn_g=ret_gn_g[l], w_up_a=w_up_a[l], w_up_b=w_up_b[l],
                 w_up_c=w_up_c[l], w_out=w_out[l], ln1_g=ln1_g[l], ln1_b=ln1_b[l], router_w=router_w,
                 router_b=router_b, moe_w1=moe_w1, moe_w3=moe_w3, moe_w2=moe_w2, layer=l, ln2_g=ln2_g[l],
                 ln2_b=ln2_b[l])
        h = _layer(h, batch, seq, p, ln0=(ln0_g, ln0_b) if l == 0 else None)
    return h.reshape(batch, seq, D)
```

```python
import functools
import math

import numpy as np
import jax
import jax.numpy as jnp
from jax import lax
from jax.experimental import pallas as pl
from jax.experimental.pallas import tpu as pltpu

F32 = jnp.float32
BF16 = jnp.bfloat16

LANES = 128
SUBLANES = 8
VMEM_LIMIT = 56 * 1024 * 1024

DEPTH = 2
CHUNK = 64
S5_GROUP_CH = 16
S5_STATE = 64
SB_HEADS = 4
SB_HEAD_DIM = 64
RET_HEADS = 4
RET_DK = 64
RET_DV = 128
ROPE_BASE = 10000.0
N_EXPERTS = 16
EXPERTS_PER_GROUP = 4
ALPHA = (2 * DEPTH) ** 0.25
LN_EPS = 1e-5

GATE_OFF = 0
NEG_BIG = -1e30


def _params(*sem):
    return pltpu.CompilerParams(dimension_semantics=sem, vmem_limit_bytes=VMEM_LIMIT)


def _const_spec(shape):
    zeros = (0,) * len(shape)
    return pl.BlockSpec(shape, lambda *_: zeros, pipeline_mode=pl.Buffered(1))


def _layer_norm(x, g, b):
    mu = jnp.mean(x, axis=-1, keepdims=True)
    xc = x - mu
    var = jnp.mean(xc * xc, axis=-1, keepdims=True)
    return xc * lax.rsqrt(var + LN_EPS) * g + b


def _inproj_kernel(*refs, col_chunk, n_gate_cols, with_ln):
    if with_ln:
        x_ref, g_ref, b_ref, w_ref, o_ref, h_ref = refs
        x = _layer_norm(x_ref[...], g_ref[...], b_ref[...])
        h_ref[...] = x
    else:
        x_ref, w_ref, o_ref = refs
        x = x_ref[...]
    xb = x.astype(BF16)
    n = w_ref.shape[1]
    for c in range(0, n, col_chunk):
        acc = jnp.dot(xb, w_ref[:, c:c + col_chunk], preferred_element_type=F32)
        if c < n_gate_cols:
            acc = jax.nn.sigmoid(acc)
        o_ref[:, c:c + col_chunk] = acc.astype(BF16)


def in_projection(h, w_bf16, n_gate_cols, ln=None, tm=512, col_chunk=512):
    T, D = h.shape
    N = w_bf16.shape[1]
    assert n_gate_cols % col_chunk == 0
    rows = pl.BlockSpec((tm, D), lambda i: (i, 0))
    proj = pl.BlockSpec((tm, N), lambda i: (i, 0))
    args, in_specs = [h], [rows]
    out_specs, out_shape = [proj], [jax.ShapeDtypeStruct((T, N), BF16)]
    if ln is not None:
        args += [ln[0].reshape(1, D), ln[1].reshape(1, D)]
        in_specs += [_const_spec((1, D)), _const_spec((1, D))]
        out_specs.append(rows)
        out_shape.append(jax.ShapeDtypeStruct((T, D), F32))
    outs = pl.pallas_call(
        functools.partial(_inproj_kernel, col_chunk=col_chunk, n_gate_cols=n_gate_cols, with_ln=ln is not None),
        grid=(T // tm,),
        in_specs=in_specs + [_const_spec((D, N))],
        out_specs=out_specs,
        out_shape=out_shape,
        compiler_params=_params("parallel"),
        name="in_proj",
    )(*args, w_bf16)
    return (outs[0], outs[1]) if ln is not None else (outs[0], h)


def _s5_kernel(u_ref, bblk_ref, lam_ref, lamk_ref, cblk_ref, d_ref, wglu_ref, bglu_ref,
               o_ref, io_ref, s_ref, carry_ref, *, n_state):
    tm, width = u_ref.shape
    K = tm // SUBLANES
    ncol = n_state // LANES
    nio = width // LANES

    @pl.when(pl.program_id(1) == 0)
    def _():
        carry_ref[...] = jnp.zeros_like(carry_ref)

    def sub_chunk_rows(k):
        return pl.ds(k, SUBLANES, stride=K)

    for c in range(nio):
        io_ref[c] = u_ref[:, c * LANES:(c + 1) * LANES].astype(F32)
    u = jnp.concatenate(
        [jnp.concatenate([io_ref[c, sub_chunk_rows(k), :] for k in range(K)], axis=0) for c in range(nio)],
        axis=1)
    s_ref[...] = jnp.dot(u.astype(BF16), bblk_ref[...], preferred_element_type=F32)

    def cols(j):
        return pl.ds(j * LANES, LANES), pl.ds(n_state + j * LANES, LANES)

    def rows(k):
        return pl.ds(pl.multiple_of(k * SUBLANES, SUBLANES), SUBLANES)

    a_re = [jnp.broadcast_to(lam_ref[0:1, pl.ds(j * LANES, LANES)], (SUBLANES, LANES)) for j in range(ncol)]
    a_im = [jnp.broadcast_to(lam_ref[1:2, pl.ds(j * LANES, LANES)], (SUBLANES, LANES)) for j in range(ncol)]

    def pass1(k, st):
        out = []
        for j in range(ncol):
            cr, ci = cols(j)
            sr, si = st[2 * j], st[2 * j + 1]
            nr = a_re[j] * sr - a_im[j] * si + s_ref[rows(k), cr]
            ni = a_re[j] * si + a_im[j] * sr + s_ref[rows(k), ci]
            s_ref[rows(k), cr] = nr
            s_ref[rows(k), ci] = ni
            out += [nr, ni]
        return tuple(out)

    zero = jnp.zeros((SUBLANES, LANES), F32)
    ends = lax.fori_loop(0, K, pass1, (zero,) * (2 * ncol))

    carries = []
    for j in range(ncol):
        cr, ci = cols(j)
        kr = lamk_ref[0:1, pl.ds(j * LANES, LANES)]
        ki = lamk_ref[1:2, pl.ds(j * LANES, LANES)]
        er, ei = ends[2 * j], ends[2 * j + 1]
        c_r = [carry_ref[0:1, cr]]
        c_i = [carry_ref[0:1, ci]]
        for r in range(1, SUBLANES + 1):
            pr, pi = c_r[-1], c_i[-1]
            c_r.append(kr * pr - ki * pi + er[r - 1:r, :])
            c_i.append(kr * pi + ki * pr + ei[r - 1:r, :])
        carry_ref[0:1, cr] = c_r[SUBLANES]
        carry_ref[0:1, ci] = c_i[SUBLANES]
        carries += [jnp.concatenate(c_r[:SUBLANES], axis=0), jnp.concatenate(c_i[:SUBLANES], axis=0)]

    def pass2(k, st):
        out = []
        for j in range(ncol):
            cr, ci = cols(j)
            dr, di = st[2 * j], st[2 * j + 1]
            nr = a_re[j] * dr - a_im[j] * di
            ni = a_re[j] * di + a_im[j] * dr
            s_ref[rows(k), cr] = s_ref[rows(k), cr] + nr
            s_ref[rows(k), ci] = s_ref[rows(k), ci] + ni
            out += [nr, ni]
        return tuple(out)

    lax.fori_loop(0, K, pass2, tuple(carries))

    y = jnp.dot(s_ref[...].astype(BF16), cblk_ref[...], preferred_element_type=F32)
    y = y + d_ref[...] * u
    y = jax.nn.gelu(y, approximate=True)
    gate = jnp.dot(y.astype(BF16), wglu_ref[...], preferred_element_type=F32) + bglu_ref[...]
    out = y * jax.nn.sigmoid(gate)
    for c in range(nio):
        for k in range(K):
            io_ref[c, sub_chunk_rows(k), :] = out[k * SUBLANES:(k + 1) * SUBLANES, c * LANES:(c + 1) * LANES]
    o_ref[...] = jnp.concatenate([io_ref[c] for c in range(nio)], axis=1).astype(BF16)


def s5_mixer(hp, u_col_block, batch, seq, bblk, lam, lamk, cblk, d_skip, w_glu, b_glu, tm=512):
    width = bblk.shape[0]
    n_state = bblk.shape[1] // 2
    nt = seq // tm
    return pl.pallas_call(
        functools.partial(_s5_kernel, n_state=n_state),
        grid=(batch, nt),
        in_specs=[
            pl.BlockSpec((tm, width), lambda b, t: (b * nt + t, u_col_block)),
            _const_spec(bblk.shape), _const_spec(lam.shape), _const_spec(lamk.shape),
            _const_spec(cblk.shape), _const_spec((1, width)), _const_spec(w_glu.shape),
            _const_spec((1, width)),
        ],
        out_specs=pl.BlockSpec((tm, width), lambda b, t: (b * nt + t, 0)),
        out_shape=jax.ShapeDtypeStruct((batch * seq, width), BF16),
        scratch_shapes=[pltpu.VMEM((width // LANES, tm, LANES), F32),
                        pltpu.VMEM((tm, 2 * n_state), F32),
                        pltpu.VMEM((SUBLANES, 2 * n_state), F32)],
        compiler_params=_params("parallel", "arbitrary"),
        name="s5_mixer",
    )(hp, bblk, lam, lamk, cblk, d_skip.reshape(1, width), w_glu, b_glu.reshape(1, width))


def s5_tables(lam_re, lam_im, log_dt, b_re, b_im, c_re, c_im, sub_chunk):
    G, P = lam_re.shape
    dt = jnp.exp(log_dt.astype(F32))[:, None]

    def lam_pow(n):
        mag = jnp.exp(lam_re * dt * n)
        return mag * jnp.cos(lam_im * dt * n), mag * jnp.sin(lam_im * dt * n)

    lr, li = lam_pow(1.0)
    kr, ki = lam_pow(float(sub_chunk))
    nr, ni = lr - 1.0, li
    den = lam_re * lam_re + lam_im * lam_im
    zr = (nr * lam_re + ni * lam_im) / den
    zi = (ni * lam_re - nr * lam_im) / den
    bbr = b_re * zr[..., None] - b_im * zi[..., None]
    bbi = b_re * zi[..., None] + b_im * zr[..., None]
    eye = jnp.eye(G, dtype=F32)
    Cg = b_re.shape[-1]

    def in_blk(m):
        return jnp.einsum('gpc,gh->gchp', m, eye).reshape(G * Cg, G * P)

    def out_blk(m):
        return jnp.einsum('gcp,gh->gphc', m, eye).reshape(G * P, G * Cg)

    bblk = jnp.concatenate([in_blk(bbr), in_blk(bbi)], axis=1).astype(BF16)
    cblk = jnp.concatenate([out_blk(c_re), -out_blk(c_im)], axis=0).astype(BF16)
    lam = jnp.stack([lr.reshape(-1), li.reshape(-1)])
    lamk = jnp.stack([kr.reshape(-1), ki.reshape(-1)])
    return bblk, lam, lamk, cblk


def _log_sigmoid_pair(z):
    soft = jnp.log(1.0 + jnp.exp(-jnp.abs(z)))
    lp = jnp.minimum(z, 0.0) - soft
    return lp, lp - z


SB_EXIT = -110.0


def _sb_kernel(q_ref, k_ref, v_ref, tri_ref, o_ref, acc_ref, carry_ref, *, blk, group, nsub):
    i = pl.program_id(2)
    lane = lax.broadcasted_iota(jnp.int32, (1, LANES), 1)
    head_lanes = [lane < SB_HEAD_DIM, lane >= SB_HEAD_DIM]
    zero = jnp.zeros((), BF16)
    qh = []
    for s in range(nsub):
        q = q_ref[s * blk:(s + 1) * blk, :]
        qh.append([jnp.where(m, q, zero) for m in head_lanes])
    row = lax.broadcasted_iota(jnp.int32, (blk, blk), 0)
    col = lax.broadcasted_iota(jnp.int32, (blk, blk), 1)
    strictly_earlier = col < row

    acc_ref[...] = jnp.zeros_like(acc_ref)
    carry_ref[...] = jnp.zeros_like(carry_ref)

    def group_step(g, first):
        ks, vs = {}, {}
        for d in range(1 - group, nsub):
            j = i * nsub - g * group + d
            start = pl.multiple_of(jnp.maximum(j, 0) * blk, blk)
            ks[d] = k_ref[pl.ds(start, blk), :]
            vj = v_ref[pl.ds(start, blk), :]
            vs[d] = [jnp.where(m & (j >= 0), vj, zero) for m in head_lanes]
        chains = [(s, u, h) for s in range(nsub) for u in range(group) for h in range(2)]
        zs = {c: lax.dot_general(qh[c[0]][c[2]], ks[c[0] - c[1]], (((1,), (1,)), ((), ())),
                                 preferred_element_type=F32) for c in chains}
        lps, rs = {}, {}
        for c in chains:
            lp, l1m = _log_sigmoid_pair(zs[c])
            if first and c[1] == 0:
                l1m = jnp.where(strictly_earlier, l1m, 0.0)
            lps[c] = lp
            rs[c] = jnp.dot(l1m.astype(BF16), tri_ref[...], preferred_element_type=F32)
        ws = {}
        top = None
        for s in range(nsub):
            for h in range(2):
                cum = carry_ref[s, h]
                for u in range(group):
                    c = (s, u, h)
                    w = jnp.exp(lps[c] + rs[c][:, :blk] + cum)
                    if first and u == 0:
                        w = jnp.where(strictly_earlier, w, 0.0)
                    ws[c] = w.astype(BF16)
                    cum = cum + rs[c][:, blk:]
                carry_ref[s, h] = cum
                top = cum if top is None else jnp.maximum(top, cum)
        for s in range(nsub):
            acc = acc_ref[s]
            for u in range(group):
                for h in range(2):
                    acc = acc + jnp.dot(ws[(s, u, h)], vs[s - u][h], preferred_element_type=F32)
            acc_ref[s] = acc
        return jnp.max(top)

    def more(state):
        g, top = state
        return (g * group <= i * nsub + nsub - 1) & (top > SB_EXIT)

    def step(state):
        g, _ = state
        return g + 1, group_step(g, False)

    lax.while_loop(more, step, (1, group_step(0, True)))
    for s in range(nsub):
        o_ref[s * blk:(s + 1) * blk, :] = acc_ref[s].astype(BF16)


def stick_breaking(hp, q_blk0, k_blk0, v_blk0, batch, seq, blk=128, group=3, nsub=4):
    assert blk == LANES
    tile = nsub * blk
    nq = seq // tile
    n_pairs = SB_HEADS * SB_HEAD_DIM // LANES
    r = np.arange(blk)
    tri = np.concatenate([(r[:, None] > r[None, :]), np.ones((blk, blk), bool)], axis=1)
    tri = jnp.asarray(tri, BF16)
    return pl.pallas_call(
        functools.partial(_sb_kernel, blk=blk, group=group, nsub=nsub),
        grid=(batch, n_pairs, nq),
        in_specs=[
            pl.BlockSpec((tile, LANES), lambda b, p, i: (b * nq + i, q_blk0 + p)),
            pl.BlockSpec((seq, LANES), lambda b, p, i: (b, k_blk0 + p)),
            pl.BlockSpec((seq, LANES), lambda b, p, i: (b, v_blk0 + p)),
            _const_spec(tri.shape),
        ],
        out_specs=pl.BlockSpec((tile, LANES), lambda b, p, i: (b * nq + i, p)),
        out_shape=jax.ShapeDtypeStruct((batch * seq, n_pairs * LANES), BF16),
        scratch_shapes=[pltpu.VMEM((nsub, blk, LANES), F32), pltpu.VMEM((nsub, 2, blk, blk), F32)],
        compiler_params=_params("parallel", "parallel", "arbitrary"),
        name="stick_breaking",
    )(hp, hp, hp, tri)


def _ret_kernel(qk_ref, v_ref, g_ref, cos_ref, sin_ref, mask_ref, rdec_ref, cdec_ref, tdec_ref, gn_ref,
                o_ref, state_ref):
    @pl.when(pl.program_id(1) == 0)
    def _():
        state_ref[...] = jnp.zeros_like(state_ref)

    half = RET_HEADS * RET_DK // 2
    qk = qk_ref[...].astype(F32)
    cos, sin = cos_ref[...], sin_ref[...]

    def rope(t):
        t1, t2 = t[:, :half], t[:, half:]
        return jnp.concatenate([t1 * cos - t2 * sin, t1 * sin + t2 * cos], axis=1)

    qr = rope(qk[:, :2 * half])
    kr = rope(qk[:, 2 * half:])
    kb = kr.astype(BF16)
    kt = kr.T
    lane = lax.broadcasted_iota(jnp.int32, (1, 2 * half), 1)
    head_of_lane = (lane % half) // (RET_DK // 2)
    for h in range(RET_HEADS):
        qh = jnp.where(head_of_lane == h, qr, 0.0)
        vh = v_ref[:, h * RET_DV:(h + 1) * RET_DV]
        scores = lax.dot_general(qh.astype(BF16), kb, (((1,), (1,)), ((), ())), preferred_element_type=F32)
        scores = scores * mask_ref[h]
        o = jnp.dot(scores.astype(BF16), vh, preferred_element_type=F32)
        o = o + jnp.dot((qh * rdec_ref[h]).astype(BF16), state_ref[h].astype(BF16), preferred_element_type=F32)
        kv = jnp.dot((kt * cdec_ref[h]).astype(BF16), vh, preferred_element_type=F32)
        state_ref[h] = tdec_ref[h] * state_ref[h] + kv
        mu = jnp.mean(o, axis=-1, keepdims=True)
        oc = o - mu
        var = jnp.mean(oc * oc, axis=-1, keepdims=True)
        on = oc * lax.rsqrt(var + LN_EPS) * gn_ref[:, h * RET_DV:(h + 1) * RET_DV]
        g = g_ref[:, h * RET_DV:(h + 1) * RET_DV].astype(F32)
        o_ref[:, h * RET_DV:(h + 1) * RET_DV] = (g * jax.nn.sigmoid(g) * on).astype(BF16)


def retention_tables(seq, tile):
    halfdim = RET_DK // 2
    inv_freq = ROPE_BASE ** (-jnp.arange(halfdim, dtype=F32) / halfdim)
    ang = jnp.arange(seq, dtype=F32)[:, None] * inv_freq[None, :]
    cos = jnp.tile(jnp.cos(ang), (1, RET_HEADS))
    sin = jnp.tile(jnp.sin(ang), (1, RET_HEADS))
    log_gamma = jnp.log(1.0 - 2.0 ** (-5.0 - jnp.arange(RET_HEADS, dtype=F32)))
    t = jnp.arange(tile, dtype=F32)
    same_or_earlier_chunk = (t[None, :] // CHUNK) <= (t[:, None] // CHUNK)
    mask = jnp.exp(log_gamma[:, None, None] * jnp.abs(t[:, None] - t[None, :]))
    mask = jnp.where(same_or_earlier_chunk[None], mask, 0.0)
    rdec = jnp.exp(log_gamma[:, None, None] * (t[None, :, None] + 1.0))
    cdec = jnp.exp(log_gamma[:, None, None] * (tile - 1.0 - t[None, None, :]))
    tdec = jnp.broadcast_to(jnp.exp(log_gamma * tile)[:, None, None], (RET_HEADS, 1, RET_DV))
    return cos, sin, mask, rdec, cdec, tdec


def retention(hp, qk_blk, v_blk, g_blk, batch, seq, gn_g, tile=256):
    cos, sin, mask, rdec, cdec, tdec = retention_tables(seq, tile)
    nt = seq // tile
    qkw = 2 * RET_HEADS * RET_DK
    vw = RET_HEADS * RET_DV
    half = RET_HEADS * RET_DK // 2
    return pl.pallas_call(
        _ret_kernel,
        grid=(batch, nt),
        in_specs=[
            pl.BlockSpec((tile, qkw), lambda b, t: (b * nt + t, qk_blk)),
            pl.BlockSpec((tile, vw), lambda b, t: (b * nt + t, v_blk)),
            pl.BlockSpec((tile, vw), lambda b, t: (b * nt + t, g_blk)),
            pl.BlockSpec((tile, half), lambda b, t: (t, 0)),
            pl.BlockSpec((tile, half), lambda b, t: (t, 0)),
            _const_spec(mask.shape), _const_spec(rdec.shape), _const_spec(cdec.shape),
            _const_spec(tdec.shape), _const_spec((1, vw)),
        ],
        out_specs=pl.BlockSpec((tile, vw), lambda b, t: (b * nt + t, 0)),
        out_shape=jax.ShapeDtypeStruct((batch * seq, vw), BF16),
        scratch_shapes=[pltpu.VMEM((RET_HEADS, 2 * half, RET_DV), F32)],
        compiler_params=_params("parallel", "arbitrary"),
        name="retention",
    )(hp, hp, hp, cos, sin, mask, rdec, cdec, tdec, gn_g.reshape(1, vw))


def _first_max(vals, lane):
    m = jnp.max(vals, axis=-1, keepdims=True)
    idx = jnp.min(jnp.where(vals == m, lane, float(LANES)), axis=-1, keepdims=True)
    return m, idx


def _merge_kernel(ga_ref, gb_ref, gc_ref, ya_ref, yb_ref, yc_ref, h_ref, wa_ref, wb_ref, wc_ref, wo_ref,
                  g1_ref, b1_ref, rw_ref, rb_ref, h1_ref, info_ref, infot_ref, cnt_ref, run_ref, merged_ref,
                  *, col_chunk):
    @pl.when(pl.program_id(0) == 0)
    def _():
        run_ref[...] = jnp.zeros_like(run_ref)

    d_model = h_ref.shape[1]
    for c in range(0, d_model, col_chunk):
        cs = slice(c, c + col_chunk)
        acc = ga_ref[:, cs].astype(F32) * jnp.dot(ya_ref[...], wa_ref[:, cs], preferred_element_type=F32)
        acc = acc + gb_ref[:, cs].astype(F32) * jnp.dot(yb_ref[...], wb_ref[:, cs], preferred_element_type=F32)
        acc = acc + gc_ref[:, cs].astype(F32) * jnp.dot(yc_ref[...], wc_ref[:, cs], preferred_element_type=F32)
        merged_ref[:, cs] = acc.astype(BF16)
    for c in range(0, d_model, col_chunk):
        cs = slice(c, c + col_chunk)
        h1_ref[:, cs] = ALPHA * h_ref[:, cs] + jnp.dot(merged_ref[...], wo_ref[:, cs], preferred_element_type=F32)
    h1 = _layer_norm(h1_ref[...], g1_ref[...], b1_ref[...])
    h1_ref[...] = h1

    h_hi = h1.astype(BF16)
    h_lo = (h1 - h_hi.astype(F32)).astype(BF16)
    logits = (jnp.dot(h_hi, rw_ref[0], preferred_element_type=F32)
              + jnp.dot(h_lo, rw_ref[0], preferred_element_type=F32)
              + jnp.dot(h_hi, rw_ref[1], preferred_element_type=F32)) + rb_ref[...]
    tm = logits.shape[0]
    lane_i = lax.broadcasted_iota(jnp.int32, (tm, LANES), 1)
    lane = lane_i.astype(F32)
    ex = jnp.exp(logits - jnp.max(logits, axis=-1, keepdims=True))
    scores = ex / jnp.sum(ex, axis=-1, keepdims=True)
    group = (lane_i // EXPERTS_PER_GROUP).astype(F32)
    best = jnp.zeros((tm, 1), F32)
    best_score = jnp.full((tm, 1), -1.0, F32)
    for gi in range(N_EXPERTS // EXPERTS_PER_GROUP):
        sg = jnp.where(group == float(gi), scores, -1.0)
        m1, i1 = _first_max(sg, lane)
        m2 = jnp.max(jnp.where(lane == i1, -1.0, sg), axis=-1, keepdims=True)
        gs = m1 + m2
        better = gs > best_score
        best = jnp.where(better, float(gi), best)
        best_score = jnp.where(better, gs, best_score)
    masked = jnp.where(group == best, scores, -1.0)
    w1, e1 = _first_max(masked, lane)
    w2, e2 = _first_max(jnp.where(lane == e1, -2.0, masked), lane)
    den = w1 + w2
    gate1, gate2 = w1 / den, w2 / den

    onehot = jnp.where((lane == e1) | (lane == e2), 1.0, 0.0)
    run_ref[0:1, :] = run_ref[0:1, :] + jnp.sum(onehot, axis=0, keepdims=True)
    cnt_ref[...] = run_ref[...]

    info = jnp.zeros((tm, LANES), F32)
    for k, val in enumerate((e1, e2, gate1, gate2)):
        info = jnp.where(lane_i == k, val, info)
    info_ref[...] = info
    infot_ref[...] = info.T[:SUBLANES]


def merge_norm_route(hp, ya, yb, yc, h, wa, wb, wc, wo, ln_g, ln_b, rw, rb, tm=512, col_chunk=512):
    T, D = h.shape
    row = lambda i: (i, 0)
    return pl.pallas_call(
        functools.partial(_merge_kernel, col_chunk=col_chunk),
        grid=(T // tm,),
        in_specs=[
            pl.BlockSpec((tm, D), lambda i: (i, 0)),
            pl.BlockSpec((tm, D), lambda i: (i, 1)),
            pl.BlockSpec((tm, D), lambda i: (i, 2)),
            pl.BlockSpec((tm, ya.shape[1]), row), pl.BlockSpec((tm, yb.shape[1]), row),
            pl.BlockSpec((tm, yc.shape[1]), row), pl.BlockSpec((tm, D), row),
            _const_spec(wa.shape), _const_spec(wb.shape), _const_spec(wc.shape), _const_spec(wo.shape),
            _const_spec((1, D)), _const_spec((1, D)), _const_spec(rw.shape), _const_spec(rb.shape),
        ],
        out_specs=[pl.BlockSpec((tm, D), row), pl.BlockSpec((tm, LANES), row),
                   pl.BlockSpec((SUBLANES, tm), lambda i: (0, i)),
                   pl.BlockSpec((SUBLANES, LANES), lambda i: (0, 0))],
        out_shape=[jax.ShapeDtypeStruct((T, D), F32), jax.ShapeDtypeStruct((T, LANES), F32),
                   jax.ShapeDtypeStruct((SUBLANES, T), F32), jax.ShapeDtypeStruct((SUBLANES, LANES), F32)],
        scratch_shapes=[pltpu.VMEM((SUBLANES, LANES), F32), pltpu.VMEM((tm, D), BF16)],
        compiler_params=_params("arbitrary"),
        name="merge_norm_route",
    )(hp, hp, hp, ya, yb, yc, h, wa, wb, wc, wo, ln_g.reshape(1, D), ln_b.reshape(1, D), rw, rb)


ROW_UNROLL = 8


def _expert_kernel(be_ref, nused_ref, nvalid_ref, first_ref, ids_ref, h_ref, w1_ref, w3_ref, w2_ref, ys_ref,
                   xa, xb, ya, yb, w1b, w3b, w2b, gsem, ssem, *, n_tok):
    j = pl.program_id(0)
    n_used = nused_ref[0]
    n_blocks = nvalid_ref.shape[0]
    blk = xa.shape[0]
    last_id = ids_ref.shape[0] - 1
    dump_row = 2 * n_tok

    def assignment(first, r):
        return ids_ref[jnp.minimum(first + r, last_id)]

    def gather_copy(first, r, x_dst, sem):
        tok = lax.rem(assignment(first, r), n_tok)
        return pltpu.make_async_copy(h_ref.at[pl.ds(tok, 1)], x_dst.at[pl.ds(r, 1)], sem)

    def scatter_copy(first, n_rows, r, y_src, sem):
        dst = jnp.where(r < n_rows, assignment(first, r), dump_row + r)
        return pltpu.make_async_copy(y_src.at[pl.ds(r, 1)], ys_ref.at[pl.ds(dst, 1)], sem)

    def wait_gathered(x_dst, sem):
        pltpu.make_async_copy(h_ref.at[pl.ds(0, blk)], x_dst, sem).wait()

    def wait_scattered(y_src, sem):
        pltpu.make_async_copy(y_src, ys_ref.at[pl.ds(0, blk)], sem).wait()

    def rolled(start_row_copy):
        def body(c, carry):
            for k in range(ROW_UNROLL):
                start_row_copy(c * ROW_UNROLL + k)
            return carry
        lax.fori_loop(0, blk // ROW_UNROLL, body, 0)

    def block_step(b, x_own, x_other, y_own, y_other, g_own, g_other, s_own, s_other, leads_pair):
        @pl.when(b < n_used)
        def _():
            if leads_pair:
                @pl.when(b == 0)
                def _():
                    first = first_ref[0]
                    rolled(lambda r: gather_copy(first, r, x_own, g_own).start())
                    y_other[...] = jnp.zeros_like(y_other)

            wait_gathered(x_own, g_own)

            @pl.when(b >= 1)
            def _():
                wait_scattered(y_own, s_own)

            if leads_pair:
                @pl.when((b == 0) | (be_ref[b] != be_ref[jnp.maximum(b - 2, 0)]))
                def _():
                    w1b[...] = w1_ref[0, 0].astype(BF16)
                    w3b[...] = w3_ref[0, 0].astype(BF16)
                    w2b[...] = w2_ref[0, 0].astype(BF16)

            first_next = first_ref[jnp.minimum(b + 1, n_blocks - 1)]
            prev = jnp.maximum(b - 1, 0)
            first_prev = first_ref[prev]
            rows_prev = jnp.where(b >= 1, nvalid_ref[prev], 0)
            for r in range(blk):
                gather_copy(first_next, r, x_other, g_other).start()
                scatter_copy(first_prev, rows_prev, r, y_other, s_other).start()
            x = x_own[...].astype(BF16)
            a = jnp.dot(x, w1b[...], preferred_element_type=F32)
            g = jnp.dot(x, w3b[...], preferred_element_type=F32)
            hb = (a * jax.nn.sigmoid(a) * g).astype(BF16)
            y_own[...] = jnp.dot(hb, w2b[...], preferred_element_type=F32)

            @pl.when(b == n_used - 1)
            def _():
                first = first_ref[b]
                n_rows = nvalid_ref[b]
                wait_scattered(y_other, s_other)
                rolled(lambda r: scatter_copy(first, n_rows, r, y_own, s_own).start())
                wait_scattered(y_own, s_own)
                wait_gathered(x_other, g_other)

    block_step(2 * j, xa, xb, ya, yb, gsem.at[0], gsem.at[1], ssem.at[0], ssem.at[1], True)
    block_step(2 * j + 1, xb, xa, yb, ya, gsem.at[1], gsem.at[0], ssem.at[1], ssem.at[0], False)


def moe_experts(h1, ids, block_e, n_used, n_valid, first, w1, w3, w2, layer, blk):
    T, D = h1.shape
    Fd = w1.shape[-1]
    nb = block_e.shape[0]
    assert nb % 2 == 0
    wspec = lambda s: pl.BlockSpec((1, 1) + s, lambda j, be, *_: (layer, be[2 * j], 0, 0))
    return pl.pallas_call(
        functools.partial(_expert_kernel, n_tok=T),
        grid_spec=pltpu.PrefetchScalarGridSpec(
            num_scalar_prefetch=5,
            grid=(nb // 2,),
            in_specs=[pl.BlockSpec(memory_space=pl.ANY), wspec((D, Fd)), wspec((D, Fd)), wspec((Fd, D))],
            out_specs=pl.BlockSpec(memory_space=pl.ANY),
            scratch_shapes=[pltpu.VMEM((blk, D), F32), pltpu.VMEM((blk, D), F32),
                            pltpu.VMEM((blk, D), F32), pltpu.VMEM((blk, D), F32),
                            pltpu.VMEM((D, Fd), BF16), pltpu.VMEM((D, Fd), BF16), pltpu.VMEM((Fd, D), BF16),
                            pltpu.SemaphoreType.DMA((2,)), pltpu.SemaphoreType.DMA((2,))],
        ),
        out_shape=jax.ShapeDtypeStruct((2 * T + blk, D), F32),
        compiler_params=pltpu.CompilerParams(dimension_semantics=("arbitrary",), vmem_limit_bytes=VMEM_LIMIT,
                                             has_side_effects=True),
        name="moe_experts",
    )(block_e, n_used, n_valid, first, ids, h1, w1, w3, w2)


def _combine_kernel(y1_ref, y2_ref, h_ref, info_ref, g_ref, b_ref, o_ref):
    info = info_ref[...]
    y = info[:, 2:3] * y1_ref[...] + info[:, 3:4] * y2_ref[...]
    o_ref[...] = _layer_norm(ALPHA * h_ref[...] + y, g_ref[...], b_ref[...])


def moe_combine(ys, h, info, ln_g, ln_b, tm=512):
    T, D = h.shape
    nt = T // tm
    return pl.pallas_call(
        _combine_kernel,
        grid=(nt,),
        in_specs=[pl.BlockSpec((tm, D), lambda i: (i, 0)),
                  pl.BlockSpec((tm, D), lambda i: (nt + i, 0)),
                  pl.BlockSpec((tm, D), lambda i: (i, 0)),
                  pl.BlockSpec((tm, LANES), lambda i: (i, 0)),
                  _const_spec((1, D)), _const_spec((1, D))],
        out_specs=pl.BlockSpec((tm, D), lambda i: (i, 0)),
        out_shape=jax.ShapeDtypeStruct((T, D), F32),
        compiler_params=_params("parallel"),
        name="moe_combine",
    )(ys, ys, h, info, ln_g.reshape(1, D), ln_b.reshape(1, D))


def route_tables(info_t, counts, blk, n_blocks):
    experts = info_t[0:2].reshape(-1).astype(jnp.int32)
    ids = jnp.argsort(experts, stable=True).astype(jnp.int32)
    cnt = counts[0, :N_EXPERTS].astype(jnp.int32)
    pair = 2 * blk
    padded = (cnt + pair - 1) // pair * pair
    pends = jnp.cumsum(padded)
    block_row = jnp.arange(n_blocks, dtype=jnp.int32) * blk
    block_e = jnp.minimum(jnp.sum(block_row[:, None] >= pends[None, :], axis=1), N_EXPERTS - 1).astype(jnp.int32)
    of_block = block_e[:, None] == jnp.arange(N_EXPERTS, dtype=jnp.int32)[None, :]

    def per_block(table):
        return jnp.sum(jnp.where(of_block, table[None, :], 0), axis=1)

    row_in_expert = block_row - per_block(pends - padded)
    n_valid = jnp.clip(per_block(cnt) - row_in_expert, 0, blk).astype(jnp.int32)
    first = (per_block(jnp.cumsum(cnt) - cnt) + row_in_expert).astype(jnp.int32)
    n_used = (pends[-1:] // blk).astype(jnp.int32)
    return ids, block_e, n_used, n_valid, first


def grouped_moe_norm(h1, info, info_t, counts, w1, w3, w2, layer, ln_g, ln_b, blk=512):
    T, D = h1.shape
    n_blocks = (2 * T) // blk + 2 * N_EXPERTS
    ids, block_e, n_used, n_valid, first = route_tables(info_t, counts, blk, n_blocks)
    ys = moe_experts(h1, ids, block_e, n_used, n_valid, first, w1, w3, w2, layer, blk)
    return moe_combine(ys, h1, info, ln_g, ln_b)


def _in_proj_layout(d_model):
    s5w = d_model // 4
    sbw = SB_HEADS * SB_HEAD_DIM
    rqk = RET_HEADS * RET_DK
    rv = RET_HEADS * RET_DV
    sizes = (s5w, sbw, sbw, sbw, rqk, rqk, rv, rv, d_model, d_model, d_model)
    off = np.concatenate([[0], np.cumsum(sizes)]).tolist()

    def permute(w):
        u_a, q_b, k_b, v_b, q_c, k_c, v_c, g_c, ga, gb, gc = [w[:, off[i]:off[i + 1]] for i in range(11)]

        def rope_perm(m):
            m = m.reshape(d_model, RET_HEADS, 2, RET_DK // 2)
            return jnp.swapaxes(m, 1, 2).reshape(d_model, rqk)

        parts = [ga, gb, gc, rope_perm(q_c), rope_perm(k_c) * RET_DK ** -0.5, v_c, g_c,
                 q_b * SB_HEAD_DIM ** -0.5, k_b, v_b, u_a]
        return jnp.concatenate([m.astype(BF16) for m in parts], axis=1)

    o_qk = 3 * d_model
    o_v = o_qk + 2 * rqk
    o_g = o_v + rv
    o_sb = o_g + rv
    o_u = o_sb + 3 * sbw
    blocks = dict(qk=o_qk // (2 * rqk), v=o_v // rv, g=o_g // rv, sb_q=o_sb // LANES,
                  sb_k=(o_sb + sbw) // LANES, sb_v=(o_sb + 2 * sbw) // LANES, u=o_u // s5w)
    assert o_qk % (2 * rqk) == 0 and o_v % rv == 0 and o_g % rv == 0 and o_sb % LANES == 0 and o_u % s5w == 0
    return permute, blocks


def _layer(h, batch, seq, p, ln0=None):
    D = h.shape[1]
    permute, blk = _in_proj_layout(D)
    hp, h = in_projection(h, permute(p['w_in']), n_gate_cols=3 * D, ln=ln0)
    s5_tile = 512
    bblk, lam, lamk, cblk = s5_tables(p['lam_re'], p['lam_im'], p['log_dt'], p['b_re'], p['b_im'],
                                      p['c_re'], p['c_im'], s5_tile // SUBLANES)
    ya = s5_mixer(hp, blk['u'], batch, seq, bblk, lam, lamk, cblk, p['d_skip'], p['w_glu'].astype(BF16),
                  p['b_glu'], tm=s5_tile)
    yb = stick_breaking(hp, blk['sb_q'], blk['sb_k'], blk['sb_v'], batch, seq)
    yc = retention(hp, blk['qk'], blk['v'], blk['g'], batch, seq, p['gn_g'])
    rw = jnp.zeros((D, LANES), F32).at[:, :N_EXPERTS].set(p['router_w'])
    rw_hi = rw.astype(BF16)
    rw = jnp.stack([rw_hi, (rw - rw_hi.astype(F32)).astype(BF16)])
    rb = jnp.full((1, LANES), NEG_BIG, F32).at[0, :N_EXPERTS].set(p['router_b'])
    h1, info, info_t, counts = merge_norm_route(
        hp, ya, yb, yc, h, p['w_up_a'].astype(BF16), p['w_up_b'].astype(BF16), p['w_up_c'].astype(BF16),
        p['w_out'].astype(BF16), p['ln1_g'], p['ln1_b'], rw, rb)
    return grouped_moe_norm(h1, info, info_t, counts, p['moe_w1'], p['moe_w3'], p['moe_w2'], p['layer'],
                            p['ln2_g'], p['ln2_b'])


def kernel(x, ln0_g, ln0_b, w_in, s5_lambda_re, s5_lambda_im, s5_log_dt, s5_b_re, s5_b_im, s5_c_re, s5_c_im,
           s5_d, s5_w_glu, s5_b_glu, ret_gn_g, w_up_a, w_up_b, w_up_c, w_out, ln1_g, ln1_b, router_w, router_b,
           moe_w1, moe_w3, moe_w2, ln2_g, ln2_b):
    batch, seq, D = x.shape
    h = x.reshape(batch * seq, D)
    for l in range(w_in.shape[0]):
        p = dict(w_in=w_in[l], lam_re=s5_lambda_re[l], lam_im=s5_lambda_im[l], log_dt=s5_log_dt[l],
                 b_re=s5_b_re[l], b_im=s5_b_im[l], c_re=s5_c_re[l], c_im=s5_c_im[l], d_skip=s5_d[l],
                 w_glu=s5_w_glu[l], b_glu=s5_b_glu[l], gn_g=ret_gn_g[l], w_up_a=w_up_a[l], w_up_b=w_up_b[l],
                 w_up_c=w_up_c[l], w_out=w_out[l], ln1_g=ln1_g[l], ln1_b=ln1_b[l], router_w=router_w,
                 router_b=router_b, moe_w1=moe_w1, moe_w3=moe_w3, moe_w2=moe_w2, layer=l, ln2_g=ln2_g[l],
                 ln2_b=ln2_b[l])
        h = _layer(h, batch, seq, p, ln0=(ln0_g, ln0_b) if l == 0 else None)
    return h.reshape(batch, seq, D)
```

```python
import functools
import math

import numpy as np
import jax
import jax.numpy as jnp
from jax import lax
from jax.experimental import pallas as pl
from jax.experimental.pallas import tpu as pltpu

F32 = jnp.float32
BF16 = jnp.bfloat16

LANES = 128
SUBLANES = 8
VMEM_LIMIT = 56 * 1024 * 1024

DEPTH = 2
CHUNK = 64
S5_GROUP_CH = 16
S5_STATE = 64
SB_HEADS = 4
SB_HEAD_DIM = 64
RET_HEADS = 4
RET_DK = 64
RET_DV = 128
ROPE_BASE = 10000.0
N_EXPERTS = 16
EXPERTS_PER_GROUP = 4
ALPHA = (2 * DEPTH) ** 0.25
LN_EPS = 1e-5

GATE_OFF = 0
NEG_BIG = -1e30


def _params(*sem):
    return pltpu.CompilerParams(dimension_semantics=sem, vmem_limit_bytes=VMEM_LIMIT)


def _const_spec(shape):
    zeros = (0,) * len(shape)
    return pl.BlockSpec(shape, lambda *_: zeros, pipeline_mode=pl.Buffered(1))


def _layer_norm(x, g, b):
    mu = jnp.mean(x, axis=-1, keepdims=True)
    xc = x - mu
    var = jnp.mean(xc * xc, axis=-1, keepdims=True)
    return xc * lax.rsqrt(var + LN_EPS) * g + b


def _inproj_kernel(*refs, col_chunk, n_gate_cols, with_ln):
    if with_ln:
        x_ref, g_ref, b_ref, w_ref, o_ref, h_ref = refs
        x = _layer_norm(x_ref[...], g_ref[...], b_ref[...])
        h_ref[...] = x
    else:
        x_ref, w_ref, o_ref = refs
        x = x_ref[...]
    xb = x.astype(BF16)
    n = w_ref.shape[1]
    for c in range(0, n, col_chunk):
        acc = jnp.dot(xb, w_ref[:, c:c + col_chunk], preferred_element_type=F32)
        if c < n_gate_cols:
            acc = jax.nn.sigmoid(acc)
        o_ref[:, c:c + col_chunk] = acc.astype(BF16)


def in_projection(h, w_bf16, n_gate_cols, ln=None, tm=512, col_chunk=512):
    T, D = h.shape
    N = w_bf16.shape[1]
    assert n_gate_cols % col_chunk == 0
    rows = pl.BlockSpec((tm, D), lambda i: (i, 0))
    proj = pl.BlockSpec((tm, N), lambda i: (i, 0))
    args, in_specs = [h], [rows]
    out_specs, out_shape = [proj], [jax.ShapeDtypeStruct((T, N), BF16)]
    if ln is not None:
        args += [ln[0].reshape(1, D), ln[1].reshape(1, D)]
        in_specs += [_const_spec((1, D)), _const_spec((1, D))]
        out_specs.append(rows)
        out_shape.append(jax.ShapeDtypeStruct((T, D), F32))
    outs = pl.pallas_call(
        functools.partial(_inproj_kernel, col_chunk=col_chunk, n_gate_cols=n_gate_cols, with_ln=ln is not None),
        grid=(T // tm,),
        in_specs=in_specs + [_const_spec((D, N))],
        out_specs=out_specs,
        out_shape=out_shape,
        compiler_params=_params("parallel"),
        name="in_proj",
    )(*args, w_bf16)
    return (outs[0], outs[1]) if ln is not None else (outs[0], h)


def _s5_kernel(u_ref, bblk_ref, lam_ref, lamk_ref, cblk_ref, d_ref, wglu_ref, bglu_ref,
               o_ref, io_ref, s_ref, carry_ref, *, n_state):
    tm, width = u_ref.shape
    K = tm // SUBLANES
    ncol = n_state // LANES
    nio = width // LANES

    @pl.when(pl.program_id(1) == 0)
    def _():
        carry_ref[...] = jnp.zeros_like(carry_ref)

    def sub_chunk_rows(k):
        return pl.ds(k, SUBLANES, stride=K)

    for c in range(nio):
        io_ref[c] = u_ref[:, c * LANES:(c + 1) * LANES].astype(F32)
    u = jnp.concatenate(
        [jnp.concatenate([io_ref[c, sub_chunk_rows(k), :] for k in range(K)], axis=0) for c in range(nio)],
        axis=1)
    s_ref[...] = jnp.dot(u.astype(BF16), bblk_ref[...], preferred_element_type=F32)

    def cols(j):
        return pl.ds(j * LANES, LANES), pl.ds(n_state + j * LANES, LANES)

    def rows(k):
        return pl.ds(pl.multiple_of(k * SUBLANES, SUBLANES), SUBLANES)

    a_re = [jnp.broadcast_to(lam_ref[0:1, pl.ds(j * LANES, LANES)], (SUBLANES, LANES)) for j in range(ncol)]
    a_im = [jnp.broadcast_to(lam_ref[1:2, pl.ds(j * LANES, LANES)], (SUBLANES, LANES)) for j in range(ncol)]

    def pass1(k, st):
        out = []
        for j in range(ncol):
            cr, ci = cols(j)
            sr, si = st[2 * j], st[2 * j + 1]
            nr = a_re[j] * sr - a_im[j] * si + s_ref[rows(k), cr]
            ni = a_re[j] * si + a_im[j] * sr + s_ref[rows(k), ci]
            s_ref[rows(k), cr] = nr
            s_ref[rows(k), ci] = ni
            out += [nr, ni]
        return tuple(out)

    zero = jnp.zeros((SUBLANES, LANES), F32)
    ends = lax.fori_loop(0, K, pass1, (zero,) * (2 * ncol))

    carries = []
    for j in range(ncol):
        cr, ci = cols(j)
        kr = lamk_ref[0:1, pl.ds(j * LANES, LANES)]
        ki = lamk_ref[1:2, pl.ds(j * LANES, LANES)]
        er, ei = ends[2 * j], ends[2 * j + 1]
        c_r = [carry_ref[0:1, cr]]
        c_i = [carry_ref[0:1, ci]]
        for r in range(1, SUBLANES + 1):
            pr, pi = c_r[-1], c_i[-1]
            c_r.append(kr * pr - ki * pi + er[r - 1:r, :])
            c_i.append(kr * pi + ki * pr + ei[r - 1:r, :])
        carry_ref[0:1, cr] = c_r[SUBLANES]
        carry_ref[0:1, ci] = c_i[SUBLANES]
        carries += [jnp.concatenate(c_r[:SUBLANES], axis=0), jnp.concatenate(c_i[:SUBLANES], axis=0)]

    def pass2(k, st):
        out = []
        for j in range(ncol):
            cr, ci = cols(j)
            dr, di = st[2 * j], st[2 * j + 1]
            nr = a_re[j] * dr - a_im[j] * di
            ni = a_re[j] * di + a_im[j] * dr
            s_ref[rows(k), cr] = s_ref[rows(k), cr] + nr
            s_ref[rows(k), ci] = s_ref[rows(k), ci] + ni
            out += [nr, ni]
        return tuple(out)

    lax.fori_loop(0, K, pass2, tuple(carries))

    y = jnp.dot(s_ref[...].astype(BF16), cblk_ref[...], preferred_element_type=F32)
    y = y + d_ref[...] * u
    y = jax.nn.gelu(y, approximate=True)
    gate = jnp.dot(y.astype(BF16), wglu_ref[...], preferred_element_type=F32) + bglu_ref[...]
    out = y * jax.nn.sigmoid(gate)
    for c in range(nio):
        for k in range(K):
            io_ref[c, sub_chunk_rows(k), :] = out[k * SUBLANES:(k + 1) * SUBLANES, c * LANES:(c + 1) * LANES]
    o_ref[...] = jnp.concatenate([io_ref[c] for c in range(nio)], axis=1).astype(BF16)


def s5_mixer(hp, u_col_block, batch, seq, bblk, lam, lamk, cblk, d_skip, w_glu, b_glu, tm=512):
    width = bblk.shape[0]
    n_state = bblk.shape[1] // 2
    nt = seq // tm
    return pl.pallas_call(
        functools.partial(_s5_kernel, n_state=n_state),
        grid=(batch, nt),
        in_specs=[
            pl.BlockSpec((tm, width), lambda b, t: (b * nt + t, u_col_block)),
            _const_spec(bblk.shape), _const_spec(lam.shape), _const_spec(lamk.shape),
            _const_spec(cblk.shape), _const_spec((1, width)), _const_spec(w_glu.shape),
            _const_spec((1, width)),
        ],
        out_specs=pl.BlockSpec((tm, width), lambda b, t: (b * nt + t, 0)),
        out_shape=jax.ShapeDtypeStruct((batch * seq, width), BF16),
        scratch_shapes=[pltpu.VMEM((width // LANES, tm, LANES), F32),
                        pltpu.VMEM((tm, 2 * n_state), F32),
                        pltpu.VMEM((SUBLANES, 2 * n_state), F32)],
        compiler_params=_params("parallel", "arbitrary"),
        name="s5_mixer",
    )(hp, bblk, lam, lamk, cblk, d_skip.reshape(1, width), w_glu, b_glu.reshape(1, width))


def s5_tables(lam_re, lam_im, log_dt, b_re, b_im, c_re, c_im, sub_chunk):
    G, P = lam_re.shape
    dt = jnp.exp(log_dt.astype(F32))[:, None]

    def lam_pow(n):
        mag = jnp.exp(lam_re * dt * n)
        return mag * jnp.cos(lam_im * dt * n), mag * jnp.sin(lam_im * dt * n)

    lr, li = lam_pow(1.0)
    kr, ki = lam_pow(float(sub_chunk))
    nr, ni = lr - 1.0, li
    den = lam_re * lam_re + lam_im * lam_im
    zr = (nr * lam_re + ni * lam_im) / den
    zi = (ni * lam_re - nr * lam_im) / den
    bbr = b_re * zr[..., None] - b_im * zi[..., None]
    bbi = b_re * zi[..., None] + b_im * zr[..., None]
    eye = jnp.eye(G, dtype=F32)
    Cg = b_re.shape[-1]

    def in_blk(m):
        return jnp.einsum('gpc,gh->gchp', m, eye).reshape(G * Cg, G * P)

    def out_blk(m):
        return jnp.einsum('gcp,gh->gphc', m, eye).reshape(G * P, G * Cg)

    bblk = jnp.concatenate([in_blk(bbr), in_blk(bbi)], axis=1).astype(BF16)
    cblk = jnp.concatenate([out_blk(c_re), -out_blk(c_im)], axis=0).astype(BF16)
    lam = jnp.stack([lr.reshape(-1), li.reshape(-1)])
    lamk = jnp.stack([kr.reshape(-1), ki.reshape(-1)])
    return bblk, lam, lamk, cblk


def _log_sigmoid_pair(z):
    soft = jnp.log(1.0 + jnp.exp(-jnp.abs(z)))
    lp = jnp.minimum(z, 0.0) - soft
    return lp, lp - z


SB_EXIT = -110.0


def _sb_kernel(q_ref, k_ref, v_ref, tri_ref, o_ref, acc_ref, carry_ref, *, blk, group, nsub):
    i = pl.program_id(2)
    lane = lax.broadcasted_iota(jnp.int32, (1, LANES), 1)
    head_lanes = [lane < SB_HEAD_DIM, lane >= SB_HEAD_DIM]
    zero = jnp.zeros((), BF16)
    qh = []
    for s in range(nsub):
        q = q_ref[s * blk:(s + 1) * blk, :]
        qh.append([jnp.where(m, q, zero) for m in head_lanes])
    row = lax.broadcasted_iota(jnp.int32, (blk, blk), 0)
    col = lax.broadcasted_iota(jnp.int32, (blk, blk), 1)
    strictly_earlier = col < row

    acc_ref[...] = jnp.zeros_like(acc_ref)
    carry_ref[...] = jnp.zeros_like(carry_ref)

    def group_step(g, first):
        ks, vs = {}, {}
        for d in range(1 - group, nsub):
            j = i * nsub - g * group + d
            start = pl.multiple_of(jnp.maximum(j, 0) * blk, blk)
            ks[d] = k_ref[pl.ds(start, blk), :]
            vj = v_ref[pl.ds(start, blk), :]
            vs[d] = [jnp.where(m & (j >= 0), vj, zero) for m in head_lanes]
        chains = [(s, u, h) for s in range(nsub) for u in range(group) for h in range(2)]
        zs = {c: lax.dot_general(qh[c[0]][c[2]], ks[c[0] - c[1]], (((1,), (1,)), ((), ())),
                                 preferred_element_type=F32) for c in chains}
        lps, rs = {}, {}
        for c in chains:
            lp, l1m = _log_sigmoid_pair(zs[c])
            if first and c[1] == 0:
                l1m = jnp.where(strictly_earlier, l1m, 0.0)
            lps[c] = lp
            rs[c] = jnp.dot(l1m.astype(BF16), tri_ref[...], preferred_element_type=F32)
        ws = {}
        top = None
        for s in range(nsub):
            for h in range(2):
                cum = carry_ref[s, h]
                for u in range(group):
                    c = (s, u, h)
                    w = jnp.exp(lps[c] + rs[c][:, :blk] + cum)
                    if first and u == 0:
                        w = jnp.where(strictly_earlier, w, 0.0)
                    ws[c] = w.astype(BF16)
                    cum = cum + rs[c][:, blk:]
                carry_ref[s, h] = cum
                top = cum if top is None else jnp.maximum(top, cum)
        for s in range(nsub):
            acc = acc_ref[s]
            for u in range(group):
                for h in range(2):
                    acc = acc + jnp.dot(ws[(s, u, h)], vs[s - u][h], preferred_element_type=F32)
            acc_ref[s] = acc
        return jnp.max(top)

    def more(state):
        g, top = state
        return (g * group <= i * nsub + nsub - 1) & (top > SB_EXIT)

    def step(state):
        g, _ = state
        return g + 1, group_step(g, False)

    lax.while_loop(more, step, (1, group_step(0, True)))
    for s in range(nsub):
        o_ref[s * blk:(s + 1) * blk, :] = acc_ref[s].astype(BF16)


def stick_breaking(hp, q_blk0, k_blk0, v_blk0, batch, seq, blk=128, group=3, nsub=4):
    assert blk == LANES
    tile = nsub * blk
    nq = seq // tile
    n_pairs = SB_HEADS * SB_HEAD_DIM // LANES
    r = np.arange(blk)
    tri = np.concatenate([(r[:, None] > r[None, :]), np.ones((blk, blk), bool)], axis=1)
    tri = jnp.asarray(tri, BF16)
    return pl.pallas_call(
        functools.partial(_sb_kernel, blk=blk, group=group, nsub=nsub),
        grid=(batch, n_pairs, nq),
        in_specs=[
            pl.BlockSpec((tile, LANES), lambda b, p, i: (b * nq + i, q_blk0 + p)),
            pl.BlockSpec((seq, LANES), lambda b, p, i: (b, k_blk0 + p)),
            pl.BlockSpec((seq, LANES), lambda b, p, i: (b, v_blk0 + p)),
            _const_spec(tri.shape),
        ],
        out_specs=pl.BlockSpec((tile, LANES), lambda b, p, i: (b * nq + i, p)),
        out_shape=jax.ShapeDtypeStruct((batch * seq, n_pairs * LANES), BF16),
        scratch_shapes=[pltpu.VMEM((nsub, blk, LANES), F32), pltpu.VMEM((nsub, 2, blk, blk), F32)],
        compiler_params=_params("parallel", "parallel", "arbitrary"),
        name="stick_breaking",
    )(hp, hp, hp, tri)


def _ret_kernel(qk_ref, v_ref, g_ref, cos_ref, sin_ref, mask_ref, rdec_ref, cdec_ref, tdec_ref, gn_ref,
                o_ref, state_ref):
    @pl.when(pl.program_id(1) == 0)
    def _():
        state_ref[...] = jnp.zeros_like(state_ref)

    half = RET_HEADS * RET_DK // 2
    qk = qk_ref[...].astype(F32)
    cos, sin = cos_ref[...], sin_ref[...]

    def rope(t):
        t1, t2 = t[:, :half], t[:, half:]
        return jnp.concatenate([t1 * cos - t2 * sin, t1 * sin + t2 * cos], axis=1)

    qr = rope(qk[:, :2 * half])
    kr = rope(qk[:, 2 * half:])
    kb = kr.astype(BF16)
    kt = kr.T
    lane = lax.broadcasted_iota(jnp.int32, (1, 2 * half), 1)
    head_of_lane = (lane % half) // (RET_DK // 2)
    for h in range(RET_HEADS):
        qh = jnp.where(head_of_lane == h, qr, 0.0)
        vh = v_ref[:, h * RET_DV:(h + 1) * RET_DV]
        scores = lax.dot_general(qh.astype(BF16), kb, (((1,), (1,)), ((), ())), preferred_element_type=F32)
        scores = scores * mask_ref[h]
        o = jnp.dot(scores.astype(BF16), vh, preferred_element_type=F32)
        o = o + jnp.dot((qh * rdec_ref[h]).astype(BF16), state_ref[h].astype(BF16), preferred_element_type=F32)
        kv = jnp.dot((kt * cdec_ref[h]).astype(BF16), vh, preferred_element_type=F32)
        state_ref[h] = tdec_ref[h] * state_ref[h] + kv
        mu = jnp.mean(o, axis=-1, keepdims=True)
        oc = o - mu
        var = jnp.mean(oc * oc, axis=-1, keepdims=True)
        on = oc * lax.rsqrt(var + LN_EPS) * gn_ref[:, h * RET_DV:(h + 1) * RET_DV]
        g = g_ref[:, h * RET_DV:(h + 1) * RET_DV].astype(F32)
        o_ref[:, h * RET_DV:(h + 1) * RET_DV] = (g * jax.nn.sigmoid(g) * on).astype(BF16)


def retention_tables(seq, tile):
    halfdim = RET_DK // 2
    inv_freq = ROPE_BASE ** (-jnp.arange(halfdim, dtype=F32) / halfdim)
    ang = jnp.arange(seq, dtype=F32)[:, None] * inv_freq[None, :]
    cos = jnp.tile(jnp.cos(ang), (1, RET_HEADS))
    sin = jnp.tile(jnp.sin(ang), (1, RET_HEADS))
    log_gamma = jnp.log(1.0 - 2.0 ** (-5.0 - jnp.arange(RET_HEADS, dtype=F32)))
    t = jnp.arange(tile, dtype=F32)
    same_or_earlier_chunk = (t[None, :] // CHUNK) <= (t[:, None] // CHUNK)
    mask = jnp.exp(log_gamma[:, None, None] * jnp.abs(t[:, None] - t[None, :]))
    mask = jnp.where(same_or_earlier_chunk[None], mask, 0.0)
    rdec = jnp.exp(log_gamma[:, None, None] * (t[None, :, None] + 1.0))
    cdec = jnp.exp(log_gamma[:, None, None] * (tile - 1.0 - t[None, None, :]))
    tdec = jnp.broadcast_to(jnp.exp(log_gamma * tile)[:, None, None], (RET_HEADS, 1, RET_DV))
    return cos, sin, mask, rdec, cdec, tdec


def retention(hp, qk_blk, v_blk, g_blk, batch, seq, gn_g, tile=256):
    cos, sin, mask, rdec, cdec, tdec = retention_tables(seq, tile)
    nt = seq // tile
    qkw = 2 * RET_HEADS * RET_DK
    vw = RET_HEADS * RET_DV
    half = RET_HEADS * RET_DK // 2
    return pl.pallas_call(
        _ret_kernel,
        grid=(batch, nt),
        in_specs=[
            pl.BlockSpec((tile, qkw), lambda b, t: (b * nt + t, qk_blk)),
            pl.BlockSpec((tile, vw), lambda b, t: (b * nt + t, v_blk)),
            pl.BlockSpec((tile, vw), lambda b, t: (b * nt + t, g_blk)),
            pl.BlockSpec((tile, half), lambda b, t: (t, 0)),
            pl.BlockSpec((tile, half), lambda b, t: (t, 0)),
            _const_spec(mask.shape), _const_spec(rdec.shape), _const_spec(cdec.shape),
            _const_spec(tdec.shape), _const_spec((1, vw)),
        ],
        out_specs=pl.BlockSpec((tile, vw), lambda b, t: (b * nt + t, 0)),
        out_shape=jax.ShapeDtypeStruct((batch * seq, vw), BF16),
        scratch_shapes=[pltpu.VMEM((RET_HEADS, 2 * half, RET_DV), F32)],
        compiler_params=_params("parallel", "arbitrary"),
        name="retention",
    )(hp, hp, hp, cos, sin, mask, rdec, cdec, tdec, gn_g.reshape(1, vw))


def _first_max(vals, lane):
    m = jnp.max(vals, axis=-1, keepdims=True)
    idx = jnp.min(jnp.where(vals == m, lane, float(LANES)), axis=-1, keepdims=True)
    return m, idx


def _merge_kernel(ga_ref, gb_ref, gc_ref, ya_ref, yb_ref, yc_ref, h_ref, wa_ref, wb_ref, wc_ref, wo_ref,
                  g1_ref, b1_ref, rw_ref, rb_ref, h1_ref, info_ref, infot_ref, cnt_ref, run_ref, merged_ref,
                  *, col_chunk):
    @pl.when(pl.program_id(0) == 0)
    def _():
        run_ref[...] = jnp.zeros_like(run_ref)

    d_model = h_ref.shape[1]
    for c in range(0, d_model, col_chunk):
        cs = slice(c, c + col_chunk)
        acc = ga_ref[:, cs].astype(F32) * jnp.dot(ya_ref[...], wa_ref[:, cs], preferred_element_type=F32)
        acc = acc + gb_ref[:, cs].astype(F32) * jnp.dot(yb_ref[...], wb_ref[:, cs], preferred_element_type=F32)
        acc = acc + gc_ref[:, cs].astype(F32) * jnp.dot(yc_ref[...], wc_ref[:, cs], preferred_element_type=F32)
        merged_ref[:, cs] = acc.astype(BF16)
    for c in range(0, d_model, col_chunk):
        cs = slice(c, c + col_chunk)
        h1_ref[:, cs] = ALPHA * h_ref[:, cs] + jnp.dot(merged_ref[...], wo_ref[:, cs], preferred_element_type=F32)
    h1 = _layer_norm(h1_ref[...], g1_ref[...], b1_ref[...])
    h1_ref[...] = h1

    h_hi = h1.astype(BF16)
    h_lo = (h1 - h_hi.astype(F32)).astype(BF16)
    logits = (jnp.dot(h_hi, rw_ref[0], preferred_element_type=F32)
              + jnp.dot(h_lo, rw_ref[0], preferred_element_type=F32)
              + jnp.dot(h_hi, rw_ref[1], preferred_element_type=F32)) + rb_ref[...]
    tm = logits.shape[0]
    lane_i = lax.broadcasted_iota(jnp.int32, (tm, LANES), 1)
    lane = lane_i.astype(F32)
    ex = jnp.exp(logits - jnp.max(logits, axis=-1, keepdims=True))
    scores = ex / jnp.sum(ex, axis=-1, keepdims=True)
    group = (lane_i // EXPERTS_PER_GROUP).astype(F32)
    best = jnp.zeros((tm, 1), F32)
    best_score = jnp.full((tm, 1), -1.0, F32)
    for gi in range(N_EXPERTS // EXPERTS_PER_GROUP):
        sg = jnp.where(group == float(gi), scores, -1.0)
        m1, i1 = _first_max(sg, lane)
        m2 = jnp.max(jnp.where(lane == i1, -1.0, sg), axis=-1, keepdims=True)
        gs = m1 + m2
        better = gs > best_score
        best = jnp.where(better, float(gi), best)
        best_score = jnp.where(better, gs, best_score)
    masked = jnp.where(group == best, scores, -1.0)
    w1, e1 = _first_max(masked, lane)
    w2, e2 = _first_max(jnp.where(lane == e1, -2.0, masked), lane)
    den = w1 + w2
    gate1, gate2 = w1 / den, w2 / den

    onehot = jnp.where((lane == e1) | (lane == e2), 1.0, 0.0)
    run_ref[0:1, :] = run_ref[0:1, :] + jnp.sum(onehot, axis=0, keepdims=True)
    cnt_ref[...] = run_ref[...]

    info = jnp.zeros((tm, LANES), F32)
    for k, val in enumerate((e1, e2, gate1, gate2)):
        info = jnp.where(lane_i == k, val, info)
    info_ref[...] = info
    infot_ref[...] = info.T[:SUBLANES]


def merge_norm_route(hp, ya, yb, yc, h, wa, wb, wc, wo, ln_g, ln_b, rw, rb, tm=512, col_chunk=512):
    T, D = h.shape
    row = lambda i: (i, 0)
    return pl.pallas_call(
        functools.partial(_merge_kernel, col_chunk=col_chunk),
        grid=(T // tm,),
        in_specs=[
            pl.BlockSpec((tm, D), lambda i: (i, 0)),
            pl.BlockSpec((tm, D), lambda i: (i, 1)),
            pl.BlockSpec((tm, D), lambda i: (i, 2)),
            pl.BlockSpec((tm, ya.shape[1]), row), pl.BlockSpec((tm, yb.shape[1]), row),
            pl.BlockSpec((tm, yc.shape[1]), row), pl.BlockSpec((tm, D), row),
            _const_spec(wa.shape), _const_spec(wb.shape), _const_spec(wc.shape), _const_spec(wo.shape),
            _const_spec((1, D)), _const_spec((1, D)), _const_spec(rw.shape), _const_spec(rb.shape),
        ],
        out_specs=[pl.BlockSpec((tm, D), row), pl.BlockSpec((tm, LANES), row),
                   pl.BlockSpec((SUBLANES, tm), lambda i: (0, i)),
                   pl.BlockSpec((SUBLANES, LANES), lambda i: (0, 0))],
        out_shape=[jax.ShapeDtypeStruct((T, D), F32), jax.ShapeDtypeStruct((T, LANES), F32),
                   jax.ShapeDtypeStruct((SUBLANES, T), F32), jax.ShapeDtypeStruct((SUBLANES, LANES), F32)],
        scratch_shapes=[pltpu.VMEM((SUBLANES, LANES), F32), pltpu.VMEM((tm, D), BF16)],
        compiler_params=_params("arbitrary"),
        name="merge_norm_route",
    )(hp, hp, hp, ya, yb, yc, h, wa, wb, wc, wo, ln_g.reshape(1, D), ln_b.reshape(1, D), rw, rb)


ROW_UNROLL = 8


def _expert_kernel(be_ref, nused_ref, rows_ref, h_ref, w1_ref, w3_ref, w2_ref, ys_ref,
                   xa, xb, ya, yb, w1b, w3b, w2b, gsem, ssem, *, n_tok):
    j = pl.program_id(0)
    n_used = nused_ref[0]
    blk = xa.shape[0]
    n_blocks = rows_ref.shape[0] // blk - 1
    tok_mask = n_tok - 1
    assert n_tok & tok_mask == 0

    def gather_copy(base, r, x_dst, sem):
        tok = rows_ref[base + r] & tok_mask
        return pltpu.make_async_copy(h_ref.at[pl.ds(tok, 1)], x_dst.at[pl.ds(r, 1)], sem)

    def scatter_copy(base, r, y_src, sem):
        return pltpu.make_async_copy(y_src.at[pl.ds(r, 1)], ys_ref.at[pl.ds(rows_ref[base + r], 1)], sem)

    def wait_gathered(x_dst, sem):
        pltpu.make_async_copy(h_ref.at[pl.ds(0, blk)], x_dst, sem).wait()

    def wait_scattered(y_src, sem):
        pltpu.make_async_copy(y_src, ys_ref.at[pl.ds(0, blk)], sem).wait()

    def rolled(start_row_copy):
        def body(c, carry):
            for k in range(ROW_UNROLL):
                start_row_copy(c * ROW_UNROLL + k)
            return carry
        lax.fori_loop(0, blk // ROW_UNROLL, body, 0)

    def block_step(b, x_own, x_other, y_own, y_other, g_own, g_other, s_own, s_other, leads_pair):
        @pl.when(b < n_used)
        def _():
            if leads_pair:
                @pl.when(b == 0)
                def _():
                    rolled(lambda r: gather_copy(0, r, x_own, g_own).start())
                    y_other[...] = jnp.zeros_like(y_other)

            wait_gathered(x_own, g_own)

            @pl.when(b >= 1)
            def _():
                wait_scattered(y_own, s_own)

            if leads_pair:
                @pl.when((b == 0) | (be_ref[b] != be_ref[jnp.maximum(b - 2, 0)]))
                def _():
                    w1b[...] = w1_ref[0, 0].astype(BF16)
                    w3b[...] = w3_ref[0, 0].astype(BF16)
                    w2b[...] = w2_ref[0, 0].astype(BF16)

            base_next = jnp.minimum(b + 1, n_blocks) * blk
            base_prev = jnp.where(b >= 1, b - 1, n_blocks) * blk
            for r in range(blk):
                gather_copy(base_next, r, x_other, g_other).start()
                scatter_copy(base_prev, r, y_other, s_other).start()
            x = x_own[...].astype(BF16)
            a = jnp.dot(x, w1b[...], preferred_element_type=F32)
            g = jnp.dot(x, w3b[...], preferred_element_type=F32)
            hb = (a * jax.nn.sigmoid(a) * g).astype(BF16)
            y_own[...] = jnp.dot(hb, w2b[...], preferred_element_type=F32)

            @pl.when(b == n_used - 1)
            def _():
                wait_scattered(y_other, s_other)
                rolled(lambda r: scatter_copy(b * blk, r, y_own, s_own).start())
                wait_scattered(y_own, s_own)
                wait_gathered(x_other, g_other)

    block_step(2 * j, xa, xb, ya, yb, gsem.at[0], gsem.at[1], ssem.at[0], ssem.at[1], True)
    block_step(2 * j + 1, xb, xa, yb, ya, gsem.at[1], gsem.at[0], ssem.at[1], ssem.at[0], False)


def moe_experts(h1, rows, block_e, n_used, w1, w3, w2, layer, blk):
    T, D = h1.shape
    Fd = w1.shape[-1]
    nb = block_e.shape[0]
    assert nb % 2 == 0
    wspec = lambda s: pl.BlockSpec((1, 1) + s, lambda j, be, *_: (layer, be[2 * j], 0, 0))
    return pl.pallas_call(
        functools.partial(_expert_kernel, n_tok=T),
        grid_spec=pltpu.PrefetchScalarGridSpec(
            num_scalar_prefetch=3,
            grid=(nb // 2,),
            in_specs=[pl.BlockSpec(memory_space=pl.ANY), wspec((D, Fd)), wspec((D, Fd)), wspec((Fd, D))],
            out_specs=pl.BlockSpec(memory_space=pl.ANY),
            scratch_shapes=[pltpu.VMEM((blk, D), F32), pltpu.VMEM((blk, D), F32),
                            pltpu.VMEM((blk, D), F32), pltpu.VMEM((blk, D), F32),
                            pltpu.VMEM((D, Fd), BF16), pltpu.VMEM((D, Fd), BF16), pltpu.VMEM((Fd, D), BF16),
                            pltpu.SemaphoreType.DMA((2,)), pltpu.SemaphoreType.DMA((2,))],
        ),
        out_shape=jax.ShapeDtypeStruct((2 * T + blk, D), F32),
        compiler_params=pltpu.CompilerParams(dimension_semantics=("arbitrary",), vmem_limit_bytes=VMEM_LIMIT,
                                             has_side_effects=True),
        name="moe_experts",
    )(block_e, n_used, rows, h1, w1, w3, w2)


def _combine_kernel(y1_ref, y2_ref, h_ref, info_ref, g_ref, b_ref, o_ref):
    info = info_ref[...]
    y = info[:, 2:3] * y1_ref[...] + info[:, 3:4] * y2_ref[...]
    o_ref[...] = _layer_norm(ALPHA * h_ref[...] + y, g_ref[...], b_ref[...])


def moe_combine(ys, h, info, ln_g, ln_b, tm=512):
    T, D = h.shape
    nt = T // tm
    return pl.pallas_call(
        _combine_kernel,
        grid=(nt,),
        in_specs=[pl.BlockSpec((tm, D), lambda i: (i, 0)),
                  pl.BlockSpec((tm, D), lambda i: (nt + i, 0)),
                  pl.BlockSpec((tm, D), lambda i: (i, 0)),
                  pl.BlockSpec((tm, LANES), lambda i: (i, 0)),
                  _const_spec((1, D)), _const_spec((1, D))],
        out_specs=pl.BlockSpec((tm, D), lambda i: (i, 0)),
        out_shape=jax.ShapeDtypeStruct((T, D), F32),
        compiler_params=_params("parallel"),
        name="moe_combine",
    )(ys, ys, h, info, ln_g.reshape(1, D), ln_b.reshape(1, D))


def route_tables(info_t, counts, blk, n_blocks):
    n_assign = info_t.shape[1] * 2
    experts = info_t[0:2].reshape(-1).astype(jnp.int32)
    ids = jnp.argsort(experts, stable=True).astype(jnp.int32)
    cnt = counts[0, :N_EXPERTS].astype(jnp.int32)
    pair = 2 * blk
    padded = (cnt + pair - 1) // pair * pair
    pends = jnp.cumsum(padded)
    block_row = jnp.arange(n_blocks, dtype=jnp.int32) * blk
    block_e = jnp.minimum(jnp.sum(block_row[:, None] >= pends[None, :], axis=1), N_EXPERTS - 1).astype(jnp.int32)
    of_block = block_e[:, None] == jnp.arange(N_EXPERTS, dtype=jnp.int32)[None, :]

    def per_block(table):
        return jnp.sum(jnp.where(of_block, table[None, :], 0), axis=1)

    row_in_expert = block_row - per_block(pends - padded)
    n_valid = jnp.clip(per_block(cnt) - row_in_expert, 0, blk).astype(jnp.int32)
    first = (per_block(jnp.cumsum(cnt) - cnt) + row_in_expert).astype(jnp.int32)
    n_used = (pends[-1:] // blk).astype(jnp.int32)
    offset = jnp.arange(blk, dtype=jnp.int32)[None, :]
    held = ids[jnp.clip(first[:, None] + offset, 0, n_assign - 1)]
    rows = jnp.where(offset < n_valid[:, None], held, n_assign + offset)
    rows = jnp.concatenate([rows, n_assign + offset]).reshape(-1)
    return rows, block_e, n_used


def grouped_moe_norm(h1, info, info_t, counts, w1, w3, w2, layer, ln_g, ln_b, blk=512):
    T, D = h1.shape
    n_blocks = (2 * T) // blk + 2 * N_EXPERTS
    rows, block_e, n_used = route_tables(info_t, counts, blk, n_blocks)
    ys = moe_experts(h1, rows, block_e, n_used, w1, w3, w2, layer, blk)
    return moe_combine(ys, h1, info, ln_g, ln_b)


def _in_proj_layout(d_model):
    s5w = d_model // 4
    sbw = SB_HEADS * SB_HEAD_DIM
    rqk = RET_HEADS * RET_DK
    rv = RET_HEADS * RET_DV
    sizes = (s5w, sbw, sbw, sbw, rqk, rqk, rv, rv, d_model, d_model, d_model)
    off = np.concatenate([[0], np.cumsum(sizes)]).tolist()

    def permute(w):
        u_a, q_b, k_b, v_b, q_c, k_c, v_c, g_c, ga, gb, gc = [w[:, off[i]:off[i + 1]] for i in range(11)]

        def rope_perm(m):
            m = m.reshape(d_model, RET_HEADS, 2, RET_DK // 2)
            return jnp.swapaxes(m, 1, 2).reshape(d_model, rqk)

        parts = [ga, gb, gc, rope_perm(q_c), rope_perm(k_c) * RET_DK ** -0.5, v_c, g_c,
                 q_b * SB_HEAD_DIM ** -0.5, k_b, v_b, u_a]
        return jnp.concatenate([m.astype(BF16) for m in parts], axis=1)

    o_qk = 3 * d_model
    o_v = o_qk + 2 * rqk
    o_g = o_v + rv
    o_sb = o_g + rv
    o_u = o_sb + 3 * sbw
    blocks = dict(qk=o_qk // (2 * rqk), v=o_v // rv, g=o_g // rv, sb_q=o_sb // LANES,
                  sb_k=(o_sb + sbw) // LANES, sb_v=(o_sb + 2 * sbw) // LANES, u=o_u // s5w)
    assert o_qk % (2 * rqk) == 0 and o_v % rv == 0 and o_g % rv == 0 and o_sb % LANES == 0 and o_u % s5w == 0
    return permute, blocks


def _layer(h, batch, seq, p, ln0=None):
    D = h.shape[1]
    permute, blk = _in_proj_layout(D)
    hp, h = in_projection(h, permute(p['w_in']), n_gate_cols=3 * D, ln=ln0)
    s5_tile = 512
    bblk, lam, lamk, cblk = s5_tables(p['lam_re'], p['lam_im'], p['log_dt'], p['b_re'], p['b_im'],
                                      p['c_re'], p['c_im'], s5_tile // SUBLANES)
    ya = s5_mixer(hp, blk['u'], batch, seq, bblk, lam, lamk, cblk, p['d_skip'], p['w_glu'].astype(BF16),
                  p['b_glu'], tm=s5_tile)
    yb = stick_breaking(hp, blk['sb_q'], blk['sb_k'], blk['sb_v'], batch, seq)
    yc = retention(hp, blk['qk'], blk['v'], blk['g'], batch, seq, p['gn_g'])
    rw = jnp.zeros((D, LANES), F32).at[:, :N_EXPERTS].set(p['router_w'])
    rw_hi = rw.astype(BF16)
    rw = jnp.stack([rw_hi, (rw - rw_hi.astype(F32)).astype(BF16)])
    rb = jnp.full((1, LANES), NEG_BIG, F32).at[0, :N_EXPERTS].set(p['router_b'])
    h1, info, info_t, counts = merge_norm_route(
        hp, ya, yb, yc, h, p['w_up_a'].astype(BF16), p['w_up_b'].astype(BF16), p['w_up_c'].astype(BF16),
        p['w_out'].astype(BF16), p['ln1_g'], p['ln1_b'], rw, rb)
    return grouped_moe_norm(h1, info, info_t, counts, p['moe_w1'], p['moe_w3'], p['moe_w2'], p['layer'],
                            p['ln2_g'], p['ln2_b'])


def kernel(x, ln0_g, ln0_b, w_in, s5_lambda_re, s5_lambda_im, s5_log_dt, s5_b_re, s5_b_im, s5_c_re, s5_c_im,
           s5_d, s5_w_glu, s5_b_glu, ret_gn_g, w_up_a, w_up_b, w_up_c, w_out, ln1_g, ln1_b, router_w, router_b,
           moe_w1, moe_w3, moe_w2, ln2_g, ln2_b):
    batch, seq, D = x.shape
    h = x.reshape(batch * seq, D)
    for l in range(w_in.shape[0]):
        p = dict(w_in=w_in[l], lam_re=s5_lambda_re[l], lam_im=s5_lambda_im[l], log_dt=s5_log_dt[l],
                 b_re=s5_b_re[l], b_im=s5_b_im[l], c_re=s5_c_re[l], c_im=s5_c_im[l], d_skip=s5_d[l],
                 w_glu=s5_w_glu[l], b_glu=s5_b_glu[l], gn_g=ret_gn_g[l], w_up_a=w_up_a[l], w_up_b=w_up_b[l],
                 w_up_c=w_up_c[l], w_out=w_out[l], ln1_g=ln1_g[l], ln1_b=ln1_b[l], router_w=router_w,
                 router_b=router_b, moe_w1=moe_w1, moe_w3=moe_w3, moe_w2=moe_w2, layer=l, ln2_g=ln2_g[l],
                 ln2_b=ln2_b[l])
        h = _layer(h, batch, seq, p, ln0=(ln0_g, ln0_b) if l == 0 else None)
    return h.reshape(batch, seq, D)
```

```python
import functools
import math

import numpy as np
import jax
import jax.numpy as jnp
from jax import lax
from jax.experimental import pallas as pl
from jax.experimental.pallas import tpu as pltpu

F32 = jnp.float32
BF16 = jnp.bfloat16

LANES = 128
SUBLANES = 8
VMEM_LIMIT = 56 * 1024 * 1024

DEPTH = 2
CHUNK = 64
S5_GROUP_CH = 16
S5_STATE = 64
SB_HEADS = 4
SB_HEAD_DIM = 64
RET_HEADS = 4
RET_DK = 64
RET_DV = 128
ROPE_BASE = 10000.0
N_EXPERTS = 16
EXPERTS_PER_GROUP = 4
ALPHA = (2 * DEPTH) ** 0.25
LN_EPS = 1e-5

GATE_OFF = 0
NEG_BIG = -1e30


def _params(*sem):
    return pltpu.CompilerParams(dimension_semantics=sem, vmem_limit_bytes=VMEM_LIMIT)


def _const_spec(shape):
    zeros = (0,) * len(shape)
    return pl.BlockSpec(shape, lambda *_: zeros, pipeline_mode=pl.Buffered(1))


def _layer_norm(x, g, b):
    mu = jnp.mean(x, axis=-1, keepdims=True)
    xc = x - mu
    var = jnp.mean(xc * xc, axis=-1, keepdims=True)
    return xc * lax.rsqrt(var + LN_EPS) * g + b


def _inproj_kernel(*refs, col_chunk, n_gate_cols, with_ln):
    if with_ln:
        x_ref, g_ref, b_ref, w_ref, o_ref, h_ref = refs
        x = _layer_norm(x_ref[...], g_ref[...], b_ref[...])
        h_ref[...] = x
    else:
        x_ref, w_ref, o_ref = refs
        x = x_ref[...]
    xb = x.astype(BF16)
    n = w_ref.shape[1]
    for c in range(0, n, col_chunk):
        acc = jnp.dot(xb, w_ref[:, c:c + col_chunk], preferred_element_type=F32)
        if c < n_gate_cols:
            acc = jax.nn.sigmoid(acc)
        o_ref[:, c:c + col_chunk] = acc.astype(BF16)


def in_projection(h, w_bf16, n_gate_cols, ln=None, tm=512, col_chunk=512):
    T, D = h.shape
    N = w_bf16.shape[1]
    assert n_gate_cols % col_chunk == 0
    rows = pl.BlockSpec((tm, D), lambda i: (i, 0))
    proj = pl.BlockSpec((tm, N), lambda i: (i, 0))
    args, in_specs = [h], [rows]
    out_specs, out_shape = [proj], [jax.ShapeDtypeStruct((T, N), BF16)]
    if ln is not None:
        args += [ln[0].reshape(1, D), ln[1].reshape(1, D)]
        in_specs += [_const_spec((1, D)), _const_spec((1, D))]
        out_specs.append(rows)
        out_shape.append(jax.ShapeDtypeStruct((T, D), F32))
    outs = pl.pallas_call(
        functools.partial(_inproj_kernel, col_chunk=col_chunk, n_gate_cols=n_gate_cols, with_ln=ln is not None),
        grid=(T // tm,),
        in_specs=in_specs + [_const_spec((D, N))],
        out_specs=out_specs,
        out_shape=out_shape,
        compiler_params=_params("parallel"),
        name="in_proj",
    )(*args, w_bf16)
    return (outs[0], outs[1]) if ln is not None else (outs[0], h)


def _s5_kernel(u_ref, bblk_ref, lam_ref, lamk_ref, cblk_ref, d_ref, wglu_ref, bglu_ref,
               o_ref, io_ref, s_ref, carry_ref, *, n_state):
    tm, width = u_ref.shape
    K = tm // SUBLANES
    ncol = n_state // LANES
    nio = width // LANES

    @pl.when(pl.program_id(1) == 0)
    def _():
        carry_ref[...] = jnp.zeros_like(carry_ref)

    def sub_chunk_rows(k):
        return pl.ds(k, SUBLANES, stride=K)

    for c in range(nio):
        io_ref[c] = u_ref[:, c * LANES:(c + 1) * LANES].astype(F32)
    u = jnp.concatenate(
        [jnp.concatenate([io_ref[c, sub_chunk_rows(k), :] for k in range(K)], axis=0) for c in range(nio)],
        axis=1)
    s_ref[...] = jnp.dot(u.astype(BF16), bblk_ref[...], preferred_element_type=F32)

    def cols(j):
        return pl.ds(j * LANES, LANES), pl.ds(n_state + j * LANES, LANES)

    def rows(k):
        return pl.ds(pl.multiple_of(k * SUBLANES, SUBLANES), SUBLANES)

    a_re = [jnp.broadcast_to(lam_ref[0:1, pl.ds(j * LANES, LANES)], (SUBLANES, LANES)) for j in range(ncol)]
    a_im = [jnp.broadcast_to(lam_ref[1:2, pl.ds(j * LANES, LANES)], (SUBLANES, LANES)) for j in range(ncol)]

    def pass1(k, st):
        out = []
        for j in range(ncol):
            cr, ci = cols(j)
            sr, si = st[2 * j], st[2 * j + 1]
            nr = a_re[j] * sr - a_im[j] * si + s_ref[rows(k), cr]
            ni = a_re[j] * si + a_im[j] * sr + s_ref[rows(k), ci]
            s_ref[rows(k), cr] = nr
            s_ref[rows(k), ci] = ni
            out += [nr, ni]
        return tuple(out)

    zero = jnp.zeros((SUBLANES, LANES), F32)
    ends = lax.fori_loop(0, K, pass1, (zero,) * (2 * ncol))

    carries = []
    for j in range(ncol):
        cr, ci = cols(j)
        kr = lamk_ref[0:1, pl.ds(j * LANES, LANES)]
        ki = lamk_ref[1:2, pl.ds(j * LANES, LANES)]
        er, ei = ends[2 * j], ends[2 * j + 1]
        c_r = [carry_ref[0:1, cr]]
        c_i = [carry_ref[0:1, ci]]
        for r in range(1, SUBLANES + 1):
            pr, pi = c_r[-1], c_i[-1]
            c_r.append(kr * pr - ki * pi + er[r - 1:r, :])
            c_i.append(kr * pi + ki * pr + ei[r - 1:r, :])
        carry_ref[0:1, cr] = c_r[SUBLANES]
        carry_ref[0:1, ci] = c_i[SUBLANES]
        carries += [jnp.concatenate(c_r[:SUBLANES], axis=0), jnp.concatenate(c_i[:SUBLANES], axis=0)]

    def pass2(k, st):
        out = []
        for j in range(ncol):
            cr, ci = cols(j)
            dr, di = st[2 * j], st[2 * j + 1]
            nr = a_re[j] * dr - a_im[j] * di
            ni = a_re[j] * di + a_im[j] * dr
            s_ref[rows(k), cr] = s_ref[rows(k), cr] + nr
            s_ref[rows(k), ci] = s_ref[rows(k), ci] + ni
            out += [nr, ni]
        return tuple(out)

    lax.fori_loop(0, K, pass2, tuple(carries))

    y = jnp.dot(s_ref[...].astype(BF16), cblk_ref[...], preferred_element_type=F32)
    y = y + d_ref[...] * u
    y = jax.nn.gelu(y, approximate=True)
    gate = jnp.dot(y.astype(BF16), wglu_ref[...], preferred_element_type=F32) + bglu_ref[...]
    out = y * jax.nn.sigmoid(gate)
    for c in range(nio):
        for k in range(K):
            io_ref[c, sub_chunk_rows(k), :] = out[k * SUBLANES:(k + 1) * SUBLANES, c * LANES:(c + 1) * LANES]
    o_ref[...] = jnp.concatenate([io_ref[c] for c in range(nio)], axis=1).astype(BF16)


def s5_mixer(hp, u_col_block, batch, seq, bblk, lam, lamk, cblk, d_skip, w_glu, b_glu, tm=512):
    width = bblk.shape[0]
    n_state = bblk.shape[1] // 2
    nt = seq // tm
    return pl.pallas_call(
        functools.partial(_s5_kernel, n_state=n_state),
        grid=(batch, nt),
        in_specs=[
            pl.BlockSpec((tm, width), lambda b, t: (b * nt + t, u_col_block)),
            _const_spec(bblk.shape), _const_spec(lam.shape), _const_spec(lamk.shape),
            _const_spec(cblk.shape), _const_spec((1, width)), _const_spec(w_glu.shape),
            _const_spec((1, width)),
        ],
        out_specs=pl.BlockSpec((tm, width), lambda b, t: (b * nt + t, 0)),
        out_shape=jax.ShapeDtypeStruct((batch * seq, width), BF16),
        scratch_shapes=[pltpu.VMEM((width // LANES, tm, LANES), F32),
                        pltpu.VMEM((tm, 2 * n_state), F32),
                        pltpu.VMEM((SUBLANES, 2 * n_state), F32)],
        compiler_params=_params("parallel", "arbitrary"),
        name="s5_mixer",
    )(hp, bblk, lam, lamk, cblk, d_skip.reshape(1, width), w_glu, b_glu.reshape(1, width))


def s5_tables(lam_re, lam_im, log_dt, b_re, b_im, c_re, c_im, sub_chunk):
    G, P = lam_re.shape
    dt = jnp.exp(log_dt.astype(F32))[:, None]

    def lam_pow(n):
        mag = jnp.exp(lam_re * dt * n)
        return mag * jnp.cos(lam_im * dt * n), mag * jnp.sin(lam_im * dt * n)

    lr, li = lam_pow(1.0)
    kr, ki = lam_pow(float(sub_chunk))
    nr, ni = lr - 1.0, li
    den = lam_re * lam_re + lam_im * lam_im
    zr = (nr * lam_re + ni * lam_im) / den
    zi = (ni * lam_re - nr * lam_im) / den
    bbr = b_re * zr[..., None] - b_im * zi[..., None]
    bbi = b_re * zi[..., None] + b_im * zr[..., None]
    eye = jnp.eye(G, dtype=F32)
    Cg = b_re.shape[-1]

    def in_blk(m):
        return jnp.einsum('gpc,gh->gchp', m, eye).reshape(G * Cg, G * P)

    def out_blk(m):
        return jnp.einsum('gcp,gh->gphc', m, eye).reshape(G * P, G * Cg)

    bblk = jnp.concatenate([in_blk(bbr), in_blk(bbi)], axis=1).astype(BF16)
    cblk = jnp.concatenate([out_blk(c_re), -out_blk(c_im)], axis=0).astype(BF16)
    lam = jnp.stack([lr.reshape(-1), li.reshape(-1)])
    lamk = jnp.stack([kr.reshape(-1), ki.reshape(-1)])
    return bblk, lam, lamk, cblk


def _log_sigmoid_pair(z):
    soft = jnp.log(1.0 + jnp.exp(-jnp.abs(z)))
    lp = jnp.minimum(z, 0.0) - soft
    return lp, lp - z


SB_EXIT = -110.0


def _sb_kernel(q_ref, k_ref, v_ref, tri_ref, o_ref, acc_ref, carry_ref, *, blk, group, nsub):
    i = pl.program_id(2)
    lane = lax.broadcasted_iota(jnp.int32, (1, LANES), 1)
    head_lanes = [lane < SB_HEAD_DIM, lane >= SB_HEAD_DIM]
    zero = jnp.zeros((), BF16)
    qh = []
    for s in range(nsub):
        q = q_ref[s * blk:(s + 1) * blk, :]
        qh.append([jnp.where(m, q, zero) for m in head_lanes])
    row = lax.broadcasted_iota(jnp.int32, (blk, blk), 0)
    col = lax.broadcasted_iota(jnp.int32, (blk, blk), 1)
    strictly_earlier = col < row

    acc_ref[...] = jnp.zeros_like(acc_ref)
    carry_ref[...] = jnp.zeros_like(carry_ref)

    def group_step(g, first):
        ks, vs = {}, {}
        for d in range(1 - group, nsub):
            j = i * nsub - g * group + d
            start = pl.multiple_of(jnp.maximum(j, 0) * blk, blk)
            ks[d] = k_ref[pl.ds(start, blk), :]
            vj = v_ref[pl.ds(start, blk), :]
            vs[d] = [jnp.where(m & (j >= 0), vj, zero) for m in head_lanes]
        chains = [(s, u, h) for s in range(nsub) for u in range(group) for h in range(2)]
        zs = {c: lax.dot_general(qh[c[0]][c[2]], ks[c[0] - c[1]], (((1,), (1,)), ((), ())),
                                 preferred_element_type=F32) for c in chains}
        lps, rs = {}, {}
        for c in chains:
            lp, l1m = _log_sigmoid_pair(zs[c])
            if first and c[1] == 0:
                l1m = jnp.where(strictly_earlier, l1m, 0.0)
            lps[c] = lp
            rs[c] = jnp.dot(l1m.astype(BF16), tri_ref[...], preferred_element_type=F32)
        ws = {}
        top = None
        for s in range(nsub):
            for h in range(2):
                cum = carry_ref[s, h]
                for u in range(group):
                    c = (s, u, h)
                    w = jnp.exp(lps[c] + rs[c][:, :blk] + cum)
                    if first and u == 0:
                        w = jnp.where(strictly_earlier, w, 0.0)
                    ws[c] = w.astype(BF16)
                    cum = cum + rs[c][:, blk:]
                carry_ref[s, h] = cum
                top = cum if top is None else jnp.maximum(top, cum)
        for s in range(nsub):
            acc = acc_ref[s]
            for u in range(group):
                for h in range(2):
                    acc = acc + jnp.dot(ws[(s, u, h)], vs[s - u][h], preferred_element_type=F32)
            acc_ref[s] = acc
        return jnp.max(top)

    def more(state):
        g, top = state
        return (g * group <= i * nsub + nsub - 1) & (top > SB_EXIT)

    def step(state):
        g, _ = state
        return g + 1, group_step(g, False)

    lax.while_loop(more, step, (1, group_step(0, True)))
    for s in range(nsub):
        o_ref[s * blk:(s + 1) * blk, :] = acc_ref[s].astype(BF16)


def stick_breaking(hp, q_blk0, k_blk0, v_blk0, batch, seq, blk=128, group=3, nsub=8):
    assert blk == LANES
    tile = nsub * blk
    nq = seq // tile
    n_pairs = SB_HEADS * SB_HEAD_DIM // LANES
    r = np.arange(blk)
    tri = np.concatenate([(r[:, None] > r[None, :]), np.ones((blk, blk), bool)], axis=1)
    tri = jnp.asarray(tri, BF16)
    return pl.pallas_call(
        functools.partial(_sb_kernel, blk=blk, group=group, nsub=nsub),
        grid=(batch, n_pairs, nq),
        in_specs=[
            pl.BlockSpec((tile, LANES), lambda b, p, i: (b * nq + i, q_blk0 + p)),
            pl.BlockSpec((seq, LANES), lambda b, p, i: (b, k_blk0 + p)),
            pl.BlockSpec((seq, LANES), lambda b, p, i: (b, v_blk0 + p)),
            _const_spec(tri.shape),
        ],
        out_specs=pl.BlockSpec((tile, LANES), lambda b, p, i: (b * nq + i, p)),
        out_shape=jax.ShapeDtypeStruct((batch * seq, n_pairs * LANES), BF16),
        scratch_shapes=[pltpu.VMEM((nsub, blk, LANES), F32), pltpu.VMEM((nsub, 2, blk, blk), F32)],
        compiler_params=_params("parallel", "parallel", "arbitrary"),
        name="stick_breaking",
    )(hp, hp, hp, tri)


def _ret_kernel(qk_ref, v_ref, g_ref, cos_ref, sin_ref, mask_ref, rdec_ref, cdec_ref, tdec_ref, gn_ref,
                o_ref, state_ref):
    @pl.when(pl.program_id(1) == 0)
    def _():
        state_ref[...] = jnp.zeros_like(state_ref)

    half = RET_HEADS * RET_DK // 2
    qk = qk_ref[...].astype(F32)
    cos, sin = cos_ref[...], sin_ref[...]

    def rope(t):
        t1, t2 = t[:, :half], t[:, half:]
        return jnp.concatenate([t1 * cos - t2 * sin, t1 * sin + t2 * cos], axis=1)

    qr = rope(qk[:, :2 * half])
    kr = rope(qk[:, 2 * half:])
    kb = kr.astype(BF16)
    kt = kr.T
    lane = lax.broadcasted_iota(jnp.int32, (1, 2 * half), 1)
    head_of_lane = (lane % half) // (RET_DK // 2)
    for h in range(RET_HEADS):
        qh = jnp.where(head_of_lane == h, qr, 0.0)
        vh = v_ref[:, h * RET_DV:(h + 1) * RET_DV]
        scores = lax.dot_general(qh.astype(BF16), kb, (((1,), (1,)), ((), ())), preferred_element_type=F32)
        scores = scores * mask_ref[h]
        o = jnp.dot(scores.astype(BF16), vh, preferred_element_type=F32)
        o = o + jnp.dot((qh * rdec_ref[h]).astype(BF16), state_ref[h].astype(BF16), preferred_element_type=F32)
        kv = jnp.dot((kt * cdec_ref[h]).astype(BF16), vh, preferred_element_type=F32)
        state_ref[h] = tdec_ref[h] * state_ref[h] + kv
        mu = jnp.mean(o, axis=-1, keepdims=True)
        oc = o - mu
        var = jnp.mean(oc * oc, axis=-1, keepdims=True)
        on = oc * lax.rsqrt(var + LN_EPS) * gn_ref[:, h * RET_DV:(h + 1) * RET_DV]
        g = g_ref[:, h * RET_DV:(h + 1) * RET_DV].astype(F32)
        o_ref[:, h * RET_DV:(h + 1) * RET_DV] = (g * jax.nn.sigmoid(g) * on).astype(BF16)


def retention_tables(seq, tile):
    halfdim = RET_DK // 2
    inv_freq = ROPE_BASE ** (-jnp.arange(halfdim, dtype=F32) / halfdim)
    ang = jnp.arange(seq, dtype=F32)[:, None] * inv_freq[None, :]
    cos = jnp.tile(jnp.cos(ang), (1, RET_HEADS))
    sin = jnp.tile(jnp.sin(ang), (1, RET_HEADS))
    log_gamma = jnp.log(1.0 - 2.0 ** (-5.0 - jnp.arange(RET_HEADS, dtype=F32)))
    t = jnp.arange(tile, dtype=F32)
    same_or_earlier_chunk = (t[None, :] // CHUNK) <= (t[:, None] // CHUNK)
    mask = jnp.exp(log_gamma[:, None, None] * jnp.abs(t[:, None] - t[None, :]))
    mask = jnp.where(same_or_earlier_chunk[None], mask, 0.0)
    rdec = jnp.exp(log_gamma[:, None, None] * (t[None, :, None] + 1.0))
    cdec = jnp.exp(log_gamma[:, None, None] * (tile - 1.0 - t[None, None, :]))
    tdec = jnp.broadcast_to(jnp.exp(log_gamma * tile)[:, None, None], (RET_HEADS, 1, RET_DV))
    return cos, sin, mask, rdec, cdec, tdec


def retention(hp, qk_blk, v_blk, g_blk, batch, seq, gn_g, tile=512):
    cos, sin, mask, rdec, cdec, tdec = retention_tables(seq, tile)
    nt = seq // tile
    qkw = 2 * RET_HEADS * RET_DK
    vw = RET_HEADS * RET_DV
    half = RET_HEADS * RET_DK // 2
    return pl.pallas_call(
        _ret_kernel,
        grid=(batch, nt),
        in_specs=[
            pl.BlockSpec((tile, qkw), lambda b, t: (b * nt + t, qk_blk)),
            pl.BlockSpec((tile, vw), lambda b, t: (b * nt + t, v_blk)),
            pl.BlockSpec((tile, vw), lambda b, t: (b * nt + t, g_blk)),
            pl.BlockSpec((tile, half), lambda b, t: (t, 0)),
            pl.BlockSpec((tile, half), lambda b, t: (t, 0)),
            _const_spec(mask.shape), _const_spec(rdec.shape), _const_spec(cdec.shape),
            _const_spec(tdec.shape), _const_spec((1, vw)),
        ],
        out_specs=pl.BlockSpec((tile, vw), lambda b, t: (b * nt + t, 0)),
        out_shape=jax.ShapeDtypeStruct((batch * seq, vw), BF16),
        scratch_shapes=[pltpu.VMEM((RET_HEADS, 2 * half, RET_DV), F32)],
        compiler_params=_params("parallel", "arbitrary"),
        name="retention",
    )(hp, hp, hp, cos, sin, mask, rdec, cdec, tdec, gn_g.reshape(1, vw))


def _first_max(vals, lane):
    m = jnp.max(vals, axis=-1, keepdims=True)
    idx = jnp.min(jnp.where(vals == m, lane, float(LANES)), axis=-1, keepdims=True)
    return m, idx


def _merge_kernel(ga_ref, gb_ref, gc_ref, ya_ref, yb_ref, yc_ref, h_ref, wa_ref, wb_ref, wc_ref, wo_ref,
                  g1_ref, b1_ref, rw_ref, rb_ref, h1_ref, info_ref, infot_ref, cnt_ref, run_ref, merged_ref,
                  hprev_ref, *, col_chunk):
    step = pl.program_id(0)

    @pl.when(step == 0)
    def _():
        run_ref[...] = jnp.zeros_like(run_ref)
        hprev_ref[...] = jnp.zeros_like(hprev_ref)

    _route(hprev_ref[...], rw_ref, rb_ref, info_ref, infot_ref, cnt_ref, run_ref, count=step > 0)

    d_model = h_ref.shape[1]
    for c in range(0, d_model, col_chunk):
        cs = slice(c, c + col_chunk)
        acc = ga_ref[:, cs].astype(F32) * jnp.dot(ya_ref[...], wa_ref[:, cs], preferred_element_type=F32)
        acc = acc + gb_ref[:, cs].astype(F32) * jnp.dot(yb_ref[...], wb_ref[:, cs], preferred_element_type=F32)
        acc = acc + gc_ref[:, cs].astype(F32) * jnp.dot(yc_ref[...], wc_ref[:, cs], preferred_element_type=F32)
        merged_ref[:, cs] = acc.astype(BF16)
    for c in range(0, d_model, col_chunk):
        cs = slice(c, c + col_chunk)
        h1_ref[:, cs] = ALPHA * h_ref[:, cs] + jnp.dot(merged_ref[...], wo_ref[:, cs], preferred_element_type=F32)
    h1 = _layer_norm(h1_ref[...], g1_ref[...], b1_ref[...])
    h1_ref[...] = h1
    hprev_ref[...] = h1


def _route(h1, rw_ref, rb_ref, info_ref, infot_ref, cnt_ref, run_ref, count):
    h_hi = h1.astype(BF16)
    h_lo = (h1 - h_hi.astype(F32)).astype(BF16)
    logits = (jnp.dot(h_hi, rw_ref[0], preferred_element_type=F32)
              + jnp.dot(h_lo, rw_ref[0], preferred_element_type=F32)
              + jnp.dot(h_hi, rw_ref[1], preferred_element_type=F32)) + rb_ref[...]
    tm = logits.shape[0]
    lane_i = lax.broadcasted_iota(jnp.int32, (tm, LANES), 1)
    lane = lane_i.astype(F32)
    ex = jnp.exp(logits - jnp.max(logits, axis=-1, keepdims=True))
    scores = ex / jnp.sum(ex, axis=-1, keepdims=True)
    group = (lane_i // EXPERTS_PER_GROUP).astype(F32)
    best = jnp.zeros((tm, 1), F32)
    best_score = jnp.full((tm, 1), -1.0, F32)
    for gi in range(N_EXPERTS // EXPERTS_PER_GROUP):
        sg = jnp.where(group == float(gi), scores, -1.0)
        m1, i1 = _first_max(sg, lane)
        m2 = jnp.max(jnp.where(lane == i1, -1.0, sg), axis=-1, keepdims=True)
        gs = m1 + m2
        better = gs > best_score
        best = jnp.where(better, float(gi), best)
        best_score = jnp.where(better, gs, best_score)
    masked = jnp.where(group == best, scores, -1.0)
    w1, e1 = _first_max(masked, lane)
    w2, e2 = _first_max(jnp.where(lane == e1, -2.0, masked), lane)
    den = w1 + w2
    gate1, gate2 = w1 / den, w2 / den

    onehot = jnp.where(((lane == e1) | (lane == e2)) & count, 1.0, 0.0)
    run_ref[0:1, :] = run_ref[0:1, :] + jnp.sum(onehot, axis=0, keepdims=True)
    cnt_ref[...] = run_ref[...]

    info = jnp.zeros((tm, LANES), F32)
    for k, val in enumerate((e1, e2, gate1, gate2)):
        info = jnp.where(lane_i == k, val, info)
    info_ref[...] = info
    infot_ref[...] = info.T[:SUBLANES]


def merge_norm_route(hp, ya, yb, yc, h, wa, wb, wc, wo, ln_g, ln_b, rw, rb, tm=512, col_chunk=512):
    T, D = h.shape
    nt = T // tm

    def tile(col):
        return lambda i: (jnp.minimum(i, nt - 1), col)

    routed = lambda i: jnp.maximum(i - 1, 0)
    return pl.pallas_call(
        functools.partial(_merge_kernel, col_chunk=col_chunk),
        grid=(nt + 1,),
        in_specs=[
            pl.BlockSpec((tm, D), tile(0)), pl.BlockSpec((tm, D), tile(1)), pl.BlockSpec((tm, D), tile(2)),
            pl.BlockSpec((tm, ya.shape[1]), tile(0)), pl.BlockSpec((tm, yb.shape[1]), tile(0)),
            pl.BlockSpec((tm, yc.shape[1]), tile(0)), pl.BlockSpec((tm, D), tile(0)),
            _const_spec(wa.shape), _const_spec(wb.shape), _const_spec(wc.shape), _const_spec(wo.shape),
            _const_spec((1, D)), _const_spec((1, D)), _const_spec(rw.shape), _const_spec(rb.shape),
        ],
        out_specs=[pl.BlockSpec((tm, D), tile(0)), pl.BlockSpec((tm, LANES), lambda i: (routed(i), 0)),
                   pl.BlockSpec((SUBLANES, tm), lambda i: (0, routed(i))),
                   pl.BlockSpec((SUBLANES, LANES), lambda i: (0, 0))],
        out_shape=[jax.ShapeDtypeStruct((T, D), F32), jax.ShapeDtypeStruct((T, LANES), F32),
                   jax.ShapeDtypeStruct((SUBLANES, T), F32), jax.ShapeDtypeStruct((SUBLANES, LANES), F32)],
        scratch_shapes=[pltpu.VMEM((SUBLANES, LANES), F32), pltpu.VMEM((tm, D), BF16), pltpu.VMEM((tm, D), F32)],
        compiler_params=_params("arbitrary"),
        name="merge_norm_route",
    )(hp, hp, hp, ya, yb, yc, h, wa, wb, wc, wo, ln_g.reshape(1, D), ln_b.reshape(1, D), rw, rb)


ROW_UNROLL = 8


def _expert_kernel(be_ref, nused_ref, rows_ref, h_ref, w1_ref, w3_ref, w2_ref, ys_ref,
                   xa, xb, ya, yb, w1b, w3b, w2b, gsem, ssem, *, n_tok):
    j = pl.program_id(0)
    n_used = nused_ref[0]
    blk = xa.shape[0]
    n_blocks = rows_ref.shape[0] // blk - 1
    tok_mask = n_tok - 1
    assert n_tok & tok_mask == 0

    def gather_copy(base, r, x_dst, sem):
        tok = rows_ref[base + r] & tok_mask
        return pltpu.make_async_copy(h_ref.at[pl.ds(tok, 1)], x_dst.at[pl.ds(r, 1)], sem)

    def scatter_copy(base, r, y_src, sem):
        return pltpu.make_async_copy(y_src.at[pl.ds(r, 1)], ys_ref.at[pl.ds(rows_ref[base + r], 1)], sem)

    def wait_gathered(x_dst, sem):
        pltpu.make_async_copy(h_ref.at[pl.ds(0, blk)], x_dst, sem).wait()

    def wait_scattered(y_src, sem):
        pltpu.make_async_copy(y_src, ys_ref.at[pl.ds(0, blk)], sem).wait()

    def rolled(start_row_copy):
        def body(c, carry):
            for k in range(ROW_UNROLL):
                start_row_copy(c * ROW_UNROLL + k)
            return carry
        lax.fori_loop(0, blk // ROW_UNROLL, body, 0)

    def block_step(b, x_own, x_other, y_own, y_other, g_own, g_other, s_own, s_other, leads_pair):
        @pl.when(b < n_used)
        def _():
            if leads_pair:
                @pl.when(b == 0)
                def _():
                    rolled(lambda r: gather_copy(0, r, x_own, g_own).start())
                    y_other[...] = jnp.zeros_like(y_other)

            wait_gathered(x_own, g_own)

            @pl.when(b >= 1)
            def _():
                wait_scattered(y_own, s_own)

            if leads_pair:
                @pl.when((b == 0) | (be_ref[b] != be_ref[jnp.maximum(b - 2, 0)]))
                def _():
                    w1b[...] = w1_ref[0, 0].astype(BF16)
                    w3b[...] = w3_ref[0, 0].astype(BF16)
                    w2b[...] = w2_ref[0, 0].astype(BF16)

            base_next = jnp.minimum(b + 1, n_blocks) * blk
            base_prev = jnp.where(b >= 1, b - 1, n_blocks) * blk
            for r in range(blk):
                gather_copy(base_next, r, x_other, g_other).start()
                scatter_copy(base_prev, r, y_other, s_other).start()
            x = x_own[...].astype(BF16)
            a = jnp.dot(x, w1b[...], preferred_element_type=F32)
            g = jnp.dot(x, w3b[...], preferred_element_type=F32)
            hb = (a * jax.nn.sigmoid(a) * g).astype(BF16)
            y_own[...] = jnp.dot(hb, w2b[...], preferred_element_type=F32)

            @pl.when(b == n_used - 1)
            def _():
                wait_scattered(y_other, s_other)
                rolled(lambda r: scatter_copy(b * blk, r, y_own, s_own).start())
                wait_scattered(y_own, s_own)
                wait_gathered(x_other, g_other)

    block_step(2 * j, xa, xb, ya, yb, gsem.at[0], gsem.at[1], ssem.at[0], ssem.at[1], True)
    block_step(2 * j + 1, xb, xa, yb, ya, gsem.at[1], gsem.at[0], ssem.at[1], ssem.at[0], False)


def moe_experts(h1, rows, block_e, n_used, w1, w3, w2, layer, blk):
    T, D = h1.shape
    Fd = w1.shape[-1]
    nb = block_e.shape[0]
    assert nb % 2 == 0
    wspec = lambda s: pl.BlockSpec((1, 1) + s, lambda j, be, *_: (layer, be[2 * j], 0, 0))
    return pl.pallas_call(
        functools.partial(_expert_kernel, n_tok=T),
        grid_spec=pltpu.PrefetchScalarGridSpec(
            num_scalar_prefetch=3,
            grid=(nb // 2,),
            in_specs=[pl.BlockSpec(memory_space=pl.ANY), wspec((D, Fd)), wspec((D, Fd)), wspec((Fd, D))],
            out_specs=pl.BlockSpec(memory_space=pl.ANY),
            scratch_shapes=[pltpu.VMEM((blk, D), F32), pltpu.VMEM((blk, D), F32),
                            pltpu.VMEM((blk, D), F32), pltpu.VMEM((blk, D), F32),
                            pltpu.VMEM((D, Fd), BF16), pltpu.VMEM((D, Fd), BF16), pltpu.VMEM((Fd, D), BF16),
                            pltpu.SemaphoreType.DMA((2,)), pltpu.SemaphoreType.DMA((2,))],
        ),
        out_shape=jax.ShapeDtypeStruct((2 * T + blk, D), F32),
        compiler_params=pltpu.CompilerParams(dimension_semantics=("arbitrary",), vmem_limit_bytes=VMEM_LIMIT,
                                             has_side_effects=True),
        name="moe_experts",
    )(block_e, n_used, rows, h1, w1, w3, w2)


def _combine_kernel(y1_ref, y2_ref, h_ref, info_ref, g_ref, b_ref, o_ref):
    info = info_ref[...]
    y = info[:, 2:3] * y1_ref[...] + info[:, 3:4] * y2_ref[...]
    o_ref[...] = _layer_norm(ALPHA * h_ref[...] + y, g_ref[...], b_ref[...])


def moe_combine(ys, h, info, ln_g, ln_b, tm=512):
    T, D = h.shape
    nt = T // tm
    return pl.pallas_call(
        _combine_kernel,
        grid=(nt,),
        in_specs=[pl.BlockSpec((tm, D), lambda i: (i, 0)),
                  pl.BlockSpec((tm, D), lambda i: (nt + i, 0)),
                  pl.BlockSpec((tm, D), lambda i: (i, 0)),
                  pl.BlockSpec((tm, LANES), lambda i: (i, 0)),
                  _const_spec((1, D)), _const_spec((1, D))],
        out_specs=pl.BlockSpec((tm, D), lambda i: (i, 0)),
        out_shape=jax.ShapeDtypeStruct((T, D), F32),
        compiler_params=_params("parallel"),
        name="moe_combine",
    )(ys, ys, h, info, ln_g.reshape(1, D), ln_b.reshape(1, D))


def route_tables(info_t, counts, blk, n_blocks):
    n_assign = info_t.shape[1] * 2
    experts = info_t[0:2].reshape(-1).astype(jnp.int32)
    ids = jnp.argsort(experts, stable=True).astype(jnp.int32)
    cnt = counts[0, :N_EXPERTS].astype(jnp.int32)
    pair = 2 * blk
    padded = (cnt + pair - 1) // pair * pair
    pends = jnp.cumsum(padded)
    block_row = jnp.arange(n_blocks, dtype=jnp.int32) * blk
    block_e = jnp.minimum(jnp.sum(block_row[:, None] >= pends[None, :], axis=1), N_EXPERTS - 1).astype(jnp.int32)
    of_block = block_e[:, None] == jnp.arange(N_EXPERTS, dtype=jnp.int32)[None, :]

    def per_block(table):
        return jnp.sum(jnp.where(of_block, table[None, :], 0), axis=1)

    row_in_expert = block_row - per_block(pends - padded)
    n_valid = jnp.clip(per_block(cnt) - row_in_expert, 0, blk).astype(jnp.int32)
    first = (per_block(jnp.cumsum(cnt) - cnt) + row_in_expert).astype(jnp.int32)
    n_used = (pends[-1:] // blk).astype(jnp.int32)
    offset = jnp.arange(blk, dtype=jnp.int32)[None, :]
    held = ids[jnp.clip(first[:, None] + offset, 0, n_assign - 1)]
    rows = jnp.where(offset < n_valid[:, None], held, n_assign + offset)
    rows = jnp.concatenate([rows, n_assign + offset]).reshape(-1)
    return rows, block_e, n_used


def grouped_moe_norm(h1, info, info_t, counts, w1, w3, w2, layer, ln_g, ln_b, blk=512):
    T, D = h1.shape
    n_blocks = (2 * T) // blk + 2 * N_EXPERTS
    rows, block_e, n_used = route_tables(info_t, counts, blk, n_blocks)
    ys = moe_experts(h1, rows, block_e, n_used, w1, w3, w2, layer, blk)
    return moe_combine(ys, h1, info, ln_g, ln_b)


def _in_proj_layout(d_model):
    s5w = d_model // 4
    sbw = SB_HEADS * SB_HEAD_DIM
    rqk = RET_HEADS * RET_DK
    rv = RET_HEADS * RET_DV
    sizes = (s5w, sbw, sbw, sbw, rqk, rqk, rv, rv, d_model, d_model, d_model)
    off = np.concatenate([[0], np.cumsum(sizes)]).tolist()

    def permute(w):
        u_a, q_b, k_b, v_b, q_c, k_c, v_c, g_c, ga, gb, gc = [w[:, off[i]:off[i + 1]] for i in range(11)]

        def rope_perm(m):
            m = m.reshape(d_model, RET_HEADS, 2, RET_DK // 2)
            return jnp.swapaxes(m, 1, 2).reshape(d_model, rqk)

        parts = [ga, gb, gc, rope_perm(q_c), rope_perm(k_c) * RET_DK ** -0.5, v_c, g_c,
                 q_b * SB_HEAD_DIM ** -0.5, k_b, v_b, u_a]
        return jnp.concatenate([m.astype(BF16) for m in parts], axis=1)

    o_qk = 3 * d_model
    o_v = o_qk + 2 * rqk
    o_g = o_v + rv
    o_sb = o_g + rv
    o_u = o_sb + 3 * sbw
    blocks = dict(qk=o_qk // (2 * rqk), v=o_v // rv, g=o_g // rv, sb_q=o_sb // LANES,
                  sb_k=(o_sb + sbw) // LANES, sb_v=(o_sb + 2 * sbw) // LANES, u=o_u // s5w)
    assert o_qk % (2 * rqk) == 0 and o_v % rv == 0 and o_g % rv == 0 and o_sb % LANES == 0 and o_u % s5w == 0
    return permute, blocks


def _layer(h, batch, seq, p, ln0=None):
    D = h.shape[1]
    permute, blk = _in_proj_layout(D)
    hp, h = in_projection(h, permute(p['w_in']), n_gate_cols=3 * D, ln=ln0)
    s5_tile = 512
    bblk, lam, lamk, cblk = s5_tables(p['lam_re'], p['lam_im'], p['log_dt'], p['b_re'], p['b_im'],
                                      p['c_re'], p['c_im'], s5_tile // SUBLANES)
    ya = s5_mixer(hp, blk['u'], batch, seq, bblk, lam, lamk, cblk, p['d_skip'], p['w_glu'].astype(BF16),
                  p['b_glu'], tm=s5_tile)
    yb = stick_breaking(hp, blk['sb_q'], blk['sb_k'], blk['sb_v'], batch, seq)
    yc = retention(hp, blk['qk'], blk['v'], blk['g'], batch, seq, p['gn_g'])
    rw = jnp.zeros((D, LANES), F32).at[:, :N_EXPERTS].set(p['router_w'])
    rw_hi = rw.astype(BF16)
    rw = jnp.stack([rw_hi, (rw - rw_hi.astype(F32)).astype(BF16)])
    rb = jnp.full((1, LANES), NEG_BIG, F32).at[0, :N_EXPERTS].set(p['router_b'])
    h1, info, info_t, counts = merge_norm_route(
        hp, ya, yb, yc, h, p['w_up_a'].astype(BF16), p['w_up_b'].astype(BF16), p['w_up_c'].astype(BF16),
        p['w_out'].astype(BF16), p['ln1_g'], p['ln1_b'], rw, rb)
    return grouped_moe_norm(h1, info, info_t, counts, p['moe_w1'], p['moe_w3'], p['moe_w2'], p['layer'],
                            p['ln2_g'], p['ln2_b'])


def kernel(x, ln0_g, ln0_b, w_in, s5_lambda_re, s5_lambda_im, s5_log_dt, s5_b_re, s5_b_im, s5_c_re, s5_c_im,
           s5_d, s5_w_glu, s5_b_glu, ret_gn_g, w_up_a, w_up_b, w_up_c, w_out, ln1_g, ln1_b, router_w, router_b,
           moe_w1, moe_w3, moe_w2, ln2_g, ln2_b):
    batch, seq, D = x.shape
    h = x.reshape(batch * seq, D)
    for l in range(w_in.shape[0]):
        p = dict(w_in=w_in[l], lam_re=s5_lambda_re[l], lam_im=s5_lambda_im[l], log_dt=s5_log_dt[l],
                 b_re=s5_b_re[l], b_im=s5_b_im[l], c_re=s5_c_re[l], c_im=s5_c_im[l], d_skip=s5_d[l],
                 w_glu=s5_w_glu[l], b_glu=s5_b_glu[l], gn_g=ret_gn_g[l], w_up_a=w_up_a[l], w_up_b=w_up_b[l],
                 w_up_c=w_up_c[l], w_out=w_out[l], ln1_g=ln1_g[l], ln1_b=ln1_b[l], router_w=router_w,
                 router_b=router_b, moe_w1=moe_w1, moe_w3=moe_w3, moe_w2=moe_w2, layer=l, ln2_g=ln2_g[l],
                 ln2_b=ln2_b[l])
        h = _layer(h, batch, seq, p, ln0=(ln0_g, ln0_b) if l == 0 else None)
    return h.reshape(batch, seq, D)
```

```python
import functools
import math

import numpy as np
import jax
import jax.numpy as jnp
from jax import lax
from jax.experimental import pallas as pl
from jax.experimental.pallas import tpu as pltpu

F32 = jnp.float32
BF16 = jnp.bfloat16

LANES = 128
SUBLANES = 8
VMEM_LIMIT = 56 * 1024 * 1024

DEPTH = 2
CHUNK = 64
S5_GROUP_CH = 16
S5_STATE = 64
SB_HEADS = 4
SB_HEAD_DIM = 64
RET_HEADS = 4
RET_DK = 64
RET_DV = 128
ROPE_BASE = 10000.0
N_EXPERTS = 16
EXPERTS_PER_GROUP = 4
ALPHA = (2 * DEPTH) ** 0.25
LN_EPS = 1e-5

GATE_OFF = 0
NEG_BIG = -1e30


def _params(*sem):
    return pltpu.CompilerParams(dimension_semantics=sem, vmem_limit_bytes=VMEM_LIMIT)


def _const_spec(shape):
    zeros = (0,) * len(shape)
    return pl.BlockSpec(shape, lambda *_: zeros, pipeline_mode=pl.Buffered(1))


def _layer_norm(x, g, b):
    mu = jnp.mean(x, axis=-1, keepdims=True)
    xc = x - mu
    var = jnp.mean(xc * xc, axis=-1, keepdims=True)
    return xc * lax.rsqrt(var + LN_EPS) * g + b


def _inproj_kernel(*refs, col_chunk, n_gate_cols, with_ln):
    if with_ln:
        x_ref, g_ref, b_ref, w_ref, o_ref, h_ref = refs
        x = _layer_norm(x_ref[...], g_ref[...], b_ref[...])
        h_ref[...] = x
    else:
        x_ref, w_ref, o_ref = refs
        x = x_ref[...]
    xb = x.astype(BF16)
    n = w_ref.shape[1]
    for c in range(0, n, col_chunk):
        acc = jnp.dot(xb, w_ref[:, c:c + col_chunk], preferred_element_type=F32)
        if c < n_gate_cols:
            acc = jax.nn.sigmoid(acc)
        o_ref[:, c:c + col_chunk] = acc.astype(BF16)


def in_projection(h, w_bf16, n_gate_cols, ln=None, tm=512, col_chunk=512):
    T, D = h.shape
    N = w_bf16.shape[1]
    assert n_gate_cols % col_chunk == 0
    rows = pl.BlockSpec((tm, D), lambda i: (i, 0))
    proj = pl.BlockSpec((tm, N), lambda i: (i, 0))
    args, in_specs = [h], [rows]
    out_specs, out_shape = [proj], [jax.ShapeDtypeStruct((T, N), BF16)]
    if ln is not None:
        args += [ln[0].reshape(1, D), ln[1].reshape(1, D)]
        in_specs += [_const_spec((1, D)), _const_spec((1, D))]
        out_specs.append(rows)
        out_shape.append(jax.ShapeDtypeStruct((T, D), F32))
    outs = pl.pallas_call(
        functools.partial(_inproj_kernel, col_chunk=col_chunk, n_gate_cols=n_gate_cols, with_ln=ln is not None),
        grid=(T // tm,),
        in_specs=in_specs + [_const_spec((D, N))],
        out_specs=out_specs,
        out_shape=out_shape,
        compiler_params=_params("parallel"),
        name="in_proj",
    )(*args, w_bf16)
    return (outs[0], outs[1]) if ln is not None else (outs[0], h)


def _s5_kernel(u_ref, bblk_ref, lam_ref, lamk_ref, cblk_ref, d_ref, wglu_ref, bglu_ref,
               o_ref, io_ref, s_ref, carry_ref, *, n_state):
    tm, width = u_ref.shape
    K = tm // SUBLANES
    ncol = n_state // LANES
    nio = width // LANES

    @pl.when(pl.program_id(1) == 0)
    def _():
        carry_ref[...] = jnp.zeros_like(carry_ref)

    def sub_chunk_rows(k):
        return pl.ds(k, SUBLANES, stride=K)

    for c in range(nio):
        io_ref[c] = u_ref[:, c * LANES:(c + 1) * LANES].astype(F32)
    u = jnp.concatenate(
        [jnp.concatenate([io_ref[c, sub_chunk_rows(k), :] for k in range(K)], axis=0) for c in range(nio)],
        axis=1)
    s_ref[...] = jnp.dot(u.astype(BF16), bblk_ref[...], preferred_element_type=F32)

    def cols(j):
        return pl.ds(j * LANES, LANES), pl.ds(n_state + j * LANES, LANES)

    def rows(k):
        return pl.ds(pl.multiple_of(k * SUBLANES, SUBLANES), SUBLANES)

    a_re = [jnp.broadcast_to(lam_ref[0:1, pl.ds(j * LANES, LANES)], (SUBLANES, LANES)) for j in range(ncol)]
    a_im = [jnp.broadcast_to(lam_ref[1:2, pl.ds(j * LANES, LANES)], (SUBLANES, LANES)) for j in range(ncol)]

    def pass1(k, st):
        out = []
        for j in range(ncol):
            cr, ci = cols(j)
            sr, si = st[2 * j], st[2 * j + 1]
            nr = a_re[j] * sr - a_im[j] * si + s_ref[rows(k), cr]
            ni = a_re[j] * si + a_im[j] * sr + s_ref[rows(k), ci]
            s_ref[rows(k), cr] = nr
            s_ref[rows(k), ci] = ni
            out += [nr, ni]
        return tuple(out)

    zero = jnp.zeros((SUBLANES, LANES), F32)
    ends = lax.fori_loop(0, K, pass1, (zero,) * (2 * ncol))

    carries = []
    for j in range(ncol):
        cr, ci = cols(j)
        kr = lamk_ref[0:1, pl.ds(j * LANES, LANES)]
        ki = lamk_ref[1:2, pl.ds(j * LANES, LANES)]
        er, ei = ends[2 * j], ends[2 * j + 1]
        c_r = [carry_ref[0:1, cr]]
        c_i = [carry_ref[0:1, ci]]
        for r in range(1, SUBLANES + 1):
            pr, pi = c_r[-1], c_i[-1]
            c_r.append(kr * pr - ki * pi + er[r - 1:r, :])
            c_i.append(kr * pi + ki * pr + ei[r - 1:r, :])
        carry_ref[0:1, cr] = c_r[SUBLANES]
        carry_ref[0:1, ci] = c_i[SUBLANES]
        carries += [jnp.concatenate(c_r[:SUBLANES], axis=0), jnp.concatenate(c_i[:SUBLANES], axis=0)]

    def pass2(k, st):
        out = []
        for j in range(ncol):
            cr, ci = cols(j)
            dr, di = st[2 * j], st[2 * j + 1]
            nr = a_re[j] * dr - a_im[j] * di
            ni = a_re[j] * di + a_im[j] * dr
            s_ref[rows(k), cr] = s_ref[rows(k), cr] + nr
            s_ref[rows(k), ci] = s_ref[rows(k), ci] + ni
            out += [nr, ni]
        return tuple(out)

    lax.fori_loop(0, K, pass2, tuple(carries))

    y = jnp.dot(s_ref[...].astype(BF16), cblk_ref[...], preferred_element_type=F32)
    y = y + d_ref[...] * u
    y = jax.nn.gelu(y, approximate=True)
    gate = jnp.dot(y.astype(BF16), wglu_ref[...], preferred_element_type=F32) + bglu_ref[...]
    out = y * jax.nn.sigmoid(gate)
    for c in range(nio):
        for k in range(K):
            io_ref[c, sub_chunk_rows(k), :] = out[k * SUBLANES:(k + 1) * SUBLANES, c * LANES:(c + 1) * LANES]
    o_ref[...] = jnp.concatenate([io_ref[c] for c in range(nio)], axis=1).astype(BF16)


def s5_mixer(hp, u_col_block, batch, seq, bblk, lam, lamk, cblk, d_skip, w_glu, b_glu, tm=512):
    width = bblk.shape[0]
    n_state = bblk.shape[1] // 2
    nt = seq // tm
    return pl.pallas_call(
        functools.partial(_s5_kernel, n_state=n_state),
        grid=(batch, nt),
        in_specs=[
            pl.BlockSpec((tm, width), lambda b, t: (b * nt + t, u_col_block)),
            _const_spec(bblk.shape), _const_spec(lam.shape), _const_spec(lamk.shape),
            _const_spec(cblk.shape), _const_spec((1, width)), _const_spec(w_glu.shape),
            _const_spec((1, width)),
        ],
        out_specs=pl.BlockSpec((tm, width), lambda b, t: (b * nt + t, 0)),
        out_shape=jax.ShapeDtypeStruct((batch * seq, width), BF16),
        scratch_shapes=[pltpu.VMEM((width // LANES, tm, LANES), F32),
                        pltpu.VMEM((tm, 2 * n_state), F32),
                        pltpu.VMEM((SUBLANES, 2 * n_state), F32)],
        compiler_params=_params("parallel", "arbitrary"),
        name="s5_mixer",
    )(hp, bblk, lam, lamk, cblk, d_skip.reshape(1, width), w_glu, b_glu.reshape(1, width))


def s5_tables(lam_re, lam_im, log_dt, b_re, b_im, c_re, c_im, sub_chunk):
    G, P = lam_re.shape
    dt = jnp.exp(log_dt.astype(F32))[:, None]

    def lam_pow(n):
        mag = jnp.exp(lam_re * dt * n)
        return mag * jnp.cos(lam_im * dt * n), mag * jnp.sin(lam_im * dt * n)

    lr, li = lam_pow(1.0)
    kr, ki = lam_pow(float(sub_chunk))
    nr, ni = lr - 1.0, li
    den = lam_re * lam_re + lam_im * lam_im
    zr = (nr * lam_re + ni * lam_im) / den
    zi = (ni * lam_re - nr * lam_im) / den
    bbr = b_re * zr[..., None] - b_im * zi[..., None]
    bbi = b_re * zi[..., None] + b_im * zr[..., None]
    eye = jnp.eye(G, dtype=F32)
    Cg = b_re.shape[-1]

    def in_blk(m):
        return jnp.einsum('gpc,gh->gchp', m, eye).reshape(G * Cg, G * P)

    def out_blk(m):
        return jnp.einsum('gcp,gh->gphc', m, eye).reshape(G * P, G * Cg)

    bblk = jnp.concatenate([in_blk(bbr), in_blk(bbi)], axis=1).astype(BF16)
    cblk = jnp.concatenate([out_blk(c_re), -out_blk(c_im)], axis=0).astype(BF16)
    lam = jnp.stack([lr.reshape(-1), li.reshape(-1)])
    lamk = jnp.stack([kr.reshape(-1), ki.reshape(-1)])
    return bblk, lam, lamk, cblk


def _log_sigmoid_pair(z):
    soft = jnp.log(1.0 + jnp.exp(-jnp.abs(z)))
    lp = jnp.minimum(z, 0.0) - soft
    return lp, lp - z


SB_EXIT = -110.0


def _sb_kernel(q_ref, k_ref, v_ref, tri_ref, o_ref, acc_ref, carry_ref, *, blk, group, nsub):
    i = pl.program_id(2)
    lane = lax.broadcasted_iota(jnp.int32, (1, LANES), 1)
    head_lanes = [lane < SB_HEAD_DIM, lane >= SB_HEAD_DIM]
    zero = jnp.zeros((), BF16)
    qh = []
    for s in range(nsub):
        q = q_ref[s * blk:(s + 1) * blk, :]
        qh.append([jnp.where(m, q, zero) for m in head_lanes])
    row = lax.broadcasted_iota(jnp.int32, (blk, blk), 0)
    col = lax.broadcasted_iota(jnp.int32, (blk, blk), 1)
    strictly_earlier = col < row

    acc_ref[...] = jnp.zeros_like(acc_ref)
    carry_ref[...] = jnp.zeros_like(carry_ref)

    def group_step(g, first):
        ks, vs = {}, {}
        for d in range(1 - group, nsub):
            j = i * nsub - g * group + d
            start = pl.multiple_of(jnp.maximum(j, 0) * blk, blk)
            ks[d] = k_ref[pl.ds(start, blk), :]
            vj = v_ref[pl.ds(start, blk), :]
            vs[d] = [jnp.where(m & (j >= 0), vj, zero) for m in head_lanes]
        chains = [(s, u, h) for s in range(nsub) for u in range(group) for h in range(2)]
        zs = {c: lax.dot_general(qh[c[0]][c[2]], ks[c[0] - c[1]], (((1,), (1,)), ((), ())),
                                 preferred_element_type=F32) for c in chains}
        lps, rs = {}, {}
        for c in chains:
            lp, l1m = _log_sigmoid_pair(zs[c])
            if first and c[1] == 0:
                l1m = jnp.where(strictly_earlier, l1m, 0.0)
            lps[c] = lp
            rs[c] = jnp.dot(l1m.astype(BF16), tri_ref[...], preferred_element_type=F32)
        ws = {}
        top = None
        for s in range(nsub):
            for h in range(2):
                cum = carry_ref[s, h]
                for u in range(group):
                    c = (s, u, h)
                    w = jnp.exp(lps[c] + rs[c][:, :blk] + cum)
                    if first and u == 0:
                        w = jnp.where(strictly_earlier, w, 0.0)
                    ws[c] = w.astype(BF16)
                    cum = cum + rs[c][:, blk:]
                carry_ref[s, h] = cum
                top = cum if top is None else jnp.maximum(top, cum)
        for s in range(nsub):
            acc = acc_ref[s]
            for u in range(group):
                for h in range(2):
                    acc = acc + jnp.dot(ws[(s, u, h)], vs[s - u][h], preferred_element_type=F32)
            acc_ref[s] = acc
        return jnp.max(top)

    def more(state):
        g, top = state
        return (g * group <= i * nsub + nsub - 1) & (top > SB_EXIT)

    def step(state):
        g, _ = state
        return g + 1, group_step(g, False)

    lax.while_loop(more, step, (1, group_step(0, True)))
    for s in range(nsub):
        o_ref[s * blk:(s + 1) * blk, :] = acc_ref[s].astype(BF16)


def stick_breaking(hp, q_blk0, k_blk0, v_blk0, batch, seq, blk=128, group=3, nsub=8):
    assert blk == LANES
    tile = nsub * blk
    nq = seq // tile
    n_pairs = SB_HEADS * SB_HEAD_DIM // LANES
    r = np.arange(blk)
    tri = np.concatenate([(r[:, None] > r[None, :]), np.ones((blk, blk), bool)], axis=1)
    tri = jnp.asarray(tri, BF16)
    return pl.pallas_call(
        functools.partial(_sb_kernel, blk=blk, group=group, nsub=nsub),
        grid=(batch, n_pairs, nq),
        in_specs=[
            pl.BlockSpec((tile, LANES), lambda b, p, i: (b * nq + i, q_blk0 + p)),
            pl.BlockSpec((seq, LANES), lambda b, p, i: (b, k_blk0 + p)),
            pl.BlockSpec((seq, LANES), lambda b, p, i: (b, v_blk0 + p)),
            _const_spec(tri.shape),
        ],
        out_specs=pl.BlockSpec((tile, LANES), lambda b, p, i: (b * nq + i, p)),
        out_shape=jax.ShapeDtypeStruct((batch * seq, n_pairs * LANES), BF16),
        scratch_shapes=[pltpu.VMEM((nsub, blk, LANES), F32), pltpu.VMEM((nsub, 2, blk, blk), F32)],
        compiler_params=_params("parallel", "parallel", "arbitrary"),
        name="stick_breaking",
    )(hp, hp, hp, tri)


def _ret_kernel(qk_ref, v_ref, g_ref, cos_ref, sin_ref, mask_ref, rdec_ref, cdec_ref, tdec_ref, gn_ref,
                o_ref, state_ref):
    @pl.when(pl.program_id(1) == 0)
    def _():
        state_ref[...] = jnp.zeros_like(state_ref)

    half = RET_HEADS * RET_DK // 2
    qk = qk_ref[...].astype(F32)
    cos, sin = cos_ref[...], sin_ref[...]

    def rope(t):
        t1, t2 = t[:, :half], t[:, half:]
        return jnp.concatenate([t1 * cos - t2 * sin, t1 * sin + t2 * cos], axis=1)

    qr = rope(qk[:, :2 * half])
    kr = rope(qk[:, 2 * half:])
    kb = kr.astype(BF16)
    kt = kr.T
    lane = lax.broadcasted_iota(jnp.int32, (1, 2 * half), 1)
    head_of_lane = (lane % half) // (RET_DK // 2)
    for h in range(RET_HEADS):
        qh = jnp.where(head_of_lane == h, qr, 0.0)
        vh = v_ref[:, h * RET_DV:(h + 1) * RET_DV]
        scores = lax.dot_general(qh.astype(BF16), kb, (((1,), (1,)), ((), ())), preferred_element_type=F32)
        scores = scores * mask_ref[h]
        o = jnp.dot(scores.astype(BF16), vh, preferred_element_type=F32)
        o = o + jnp.dot((qh * rdec_ref[h]).astype(BF16), state_ref[h].astype(BF16), preferred_element_type=F32)
        kv = jnp.dot((kt * cdec_ref[h]).astype(BF16), vh, preferred_element_type=F32)
        state_ref[h] = tdec_ref[h] * state_ref[h] + kv
        mu = jnp.mean(o, axis=-1, keepdims=True)
        oc = o - mu
        var = jnp.mean(oc * oc, axis=-1, keepdims=True)
        on = oc * lax.rsqrt(var + LN_EPS) * gn_ref[:, h * RET_DV:(h + 1) * RET_DV]
        g = g_ref[:, h * RET_DV:(h + 1) * RET_DV].astype(F32)
        o_ref[:, h * RET_DV:(h + 1) * RET_DV] = (g * jax.nn.sigmoid(g) * on).astype(BF16)


def retention_tables(seq, tile):
    halfdim = RET_DK // 2
    inv_freq = ROPE_BASE ** (-jnp.arange(halfdim, dtype=F32) / halfdim)
    ang = jnp.arange(seq, dtype=F32)[:, None] * inv_freq[None, :]
    cos = jnp.tile(jnp.cos(ang), (1, RET_HEADS))
    sin = jnp.tile(jnp.sin(ang), (1, RET_HEADS))
    log_gamma = jnp.log(1.0 - 2.0 ** (-5.0 - jnp.arange(RET_HEADS, dtype=F32)))
    t = jnp.arange(tile, dtype=F32)
    same_or_earlier_chunk = (t[None, :] // CHUNK) <= (t[:, None] // CHUNK)
    mask = jnp.exp(log_gamma[:, None, None] * jnp.abs(t[:, None] - t[None, :]))
    mask = jnp.where(same_or_earlier_chunk[None], mask, 0.0)
    rdec = jnp.exp(log_gamma[:, None, None] * (t[None, :, None] + 1.0))
    cdec = jnp.exp(log_gamma[:, None, None] * (tile - 1.0 - t[None, None, :]))
    tdec = jnp.broadcast_to(jnp.exp(log_gamma * tile)[:, None, None], (RET_HEADS, 1, RET_DV))
    return cos, sin, mask, rdec, cdec, tdec


def retention(hp, qk_blk, v_blk, g_blk, batch, seq, gn_g, tile=512):
    cos, sin, mask, rdec, cdec, tdec = retention_tables(seq, tile)
    nt = seq // tile
    qkw = 2 * RET_HEADS * RET_DK
    vw = RET_HEADS * RET_DV
    half = RET_HEADS * RET_DK // 2
    return pl.pallas_call(
        _ret_kernel,
        grid=(batch, nt),
        in_specs=[
            pl.BlockSpec((tile, qkw), lambda b, t: (b * nt + t, qk_blk)),
            pl.BlockSpec((tile, vw), lambda b, t: (b * nt + t, v_blk)),
            pl.BlockSpec((tile, vw), lambda b, t: (b * nt + t, g_blk)),
            pl.BlockSpec((tile, half), lambda b, t: (t, 0)),
            pl.BlockSpec((tile, half), lambda b, t: (t, 0)),
            _const_spec(mask.shape), _const_spec(rdec.shape), _const_spec(cdec.shape),
            _const_spec(tdec.shape), _const_spec((1, vw)),
        ],
        out_specs=pl.BlockSpec((tile, vw), lambda b, t: (b * nt + t, 0)),
        out_shape=jax.ShapeDtypeStruct((batch * seq, vw), BF16),
        scratch_shapes=[pltpu.VMEM((RET_HEADS, 2 * half, RET_DV), F32)],
        compiler_params=_params("parallel", "arbitrary"),
        name="retention",
    )(hp, hp, hp, cos, sin, mask, rdec, cdec, tdec, gn_g.reshape(1, vw))


def _first_max(vals, lane):
    m = jnp.max(vals, axis=-1, keepdims=True)
    idx = jnp.min(jnp.where(vals == m, lane, float(LANES)), axis=-1, keepdims=True)
    return m, idx


def _merge_kernel(ga_ref, gb_ref, gc_ref, ya_ref, yb_ref, yc_ref, h_ref, wa_ref, wb_ref, wc_ref, wo_ref,
                  g1_ref, b1_ref, rw_ref, rb_ref, h1_ref, info_ref, infot_ref, cnt_ref, run_ref, merged_ref,
                  hprev_ref, *, col_chunk):
    step = pl.program_id(0)

    @pl.when(step == 0)
    def _():
        run_ref[...] = jnp.zeros_like(run_ref)
        hprev_ref[...] = jnp.zeros_like(hprev_ref)

    _route(hprev_ref[...], rw_ref, rb_ref, info_ref, infot_ref, cnt_ref, run_ref, count=step > 0)

    d_model = h_ref.shape[1]
    for c in range(0, d_model, col_chunk):
        cs = slice(c, c + col_chunk)
        acc = ga_ref[:, cs].astype(F32) * jnp.dot(ya_ref[...], wa_ref[:, cs], preferred_element_type=F32)
        acc = acc + gb_ref[:, cs].astype(F32) * jnp.dot(yb_ref[...], wb_ref[:, cs], preferred_element_type=F32)
        acc = acc + gc_ref[:, cs].astype(F32) * jnp.dot(yc_ref[...], wc_ref[:, cs], preferred_element_type=F32)
        merged_ref[:, cs] = acc.astype(BF16)
    for c in range(0, d_model, col_chunk):
        cs = slice(c, c + col_chunk)
        h1_ref[:, cs] = ALPHA * h_ref[:, cs] + jnp.dot(merged_ref[...], wo_ref[:, cs], preferred_element_type=F32)
    h1 = _layer_norm(h1_ref[...], g1_ref[...], b1_ref[...])
    h1_ref[...] = h1
    hprev_ref[...] = h1


def _route(h1, rw_ref, rb_ref, info_ref, infot_ref, cnt_ref, run_ref, count):
    h_hi = h1.astype(BF16)
    h_lo = (h1 - h_hi.astype(F32)).astype(BF16)
    logits = (jnp.dot(h_hi, rw_ref[0], preferred_element_type=F32)
              + jnp.dot(h_lo, rw_ref[0], preferred_element_type=F32)
              + jnp.dot(h_hi, rw_ref[1], preferred_element_type=F32)) + rb_ref[...]
    tm = logits.shape[0]
    lane_i = lax.broadcasted_iota(jnp.int32, (tm, LANES), 1)
    lane = lane_i.astype(F32)
    ex = jnp.exp(logits - jnp.max(logits, axis=-1, keepdims=True))
    scores = ex / jnp.sum(ex, axis=-1, keepdims=True)
    group = (lane_i // EXPERTS_PER_GROUP).astype(F32)
    best = jnp.zeros((tm, 1), F32)
    best_score = jnp.full((tm, 1), -1.0, F32)
    for gi in range(N_EXPERTS // EXPERTS_PER_GROUP):
        sg = jnp.where(group == float(gi), scores, -1.0)
        m1, i1 = _first_max(sg, lane)
        m2 = jnp.max(jnp.where(lane == i1, -1.0, sg), axis=-1, keepdims=True)
        gs = m1 + m2
        better = gs > best_score
        best = jnp.where(better, float(gi), best)
        best_score = jnp.where(better, gs, best_score)
    masked = jnp.where(group == best, scores, -1.0)
    w1, e1 = _first_max(masked, lane)
    w2, e2 = _first_max(jnp.where(lane == e1, -2.0, masked), lane)
    den = w1 + w2
    gate1, gate2 = w1 / den, w2 / den

    onehot = jnp.where(((lane == e1) | (lane == e2)) & count, 1.0, 0.0)
    run_ref[0:1, :] = run_ref[0:1, :] + jnp.sum(onehot, axis=0, keepdims=True)
    cnt_ref[...] = run_ref[...]

    info = jnp.zeros((tm, LANES), F32)
    for k, val in enumerate((e1, e2, gate1, gate2)):
        info = jnp.where(lane_i == k, val, info)
    info_ref[...] = info
    infot_ref[...] = info.T[:SUBLANES]


def merge_norm_route(hp, ya, yb, yc, h, wa, wb, wc, wo, ln_g, ln_b, rw, rb, tm=512, col_chunk=512):
    T, D = h.shape
    nt = T // tm

    def tile(col):
        return lambda i: (jnp.minimum(i, nt - 1), col)

    routed = lambda i: jnp.maximum(i - 1, 0)
    return pl.pallas_call(
        functools.partial(_merge_kernel, col_chunk=col_chunk),
        grid=(nt + 1,),
        in_specs=[
            pl.BlockSpec((tm, D), tile(0)), pl.BlockSpec((tm, D), tile(1)), pl.BlockSpec((tm, D), tile(2)),
            pl.BlockSpec((tm, ya.shape[1]), tile(0)), pl.BlockSpec((tm, yb.shape[1]), tile(0)),
            pl.BlockSpec((tm, yc.shape[1]), tile(0)), pl.BlockSpec((tm, D), tile(0)),
            _const_spec(wa.shape), _const_spec(wb.shape), _const_spec(wc.shape), _const_spec(wo.shape),
            _const_spec((1, D)), _const_spec((1, D)), _const_spec(rw.shape), _const_spec(rb.shape),
        ],
        out_specs=[pl.BlockSpec((tm, D), tile(0)), pl.BlockSpec((tm, LANES), lambda i: (routed(i), 0)),
                   pl.BlockSpec((SUBLANES, tm), lambda i: (0, routed(i))),
                   pl.BlockSpec((SUBLANES, LANES), lambda i: (0, 0))],
        out_shape=[jax.ShapeDtypeStruct((T, D), F32), jax.ShapeDtypeStruct((T, LANES), F32),
                   jax.ShapeDtypeStruct((SUBLANES, T), F32), jax.ShapeDtypeStruct((SUBLANES, LANES), F32)],
        scratch_shapes=[pltpu.VMEM((SUBLANES, LANES), F32), pltpu.VMEM((tm, D), BF16), pltpu.VMEM((tm, D), F32)],
        compiler_params=_params("arbitrary"),
        name="merge_norm_route",
    )(hp, hp, hp, ya, yb, yc, h, wa, wb, wc, wo, ln_g.reshape(1, D), ln_b.reshape(1, D), rw, rb)


ROW_UNROLL = 8


def _expert_kernel(be_ref, nused_ref, base_ref, rows_ref, h_ref, w1_ref, w3_ref, w2_ref, ys_ref,
                   xa, xb, ya, yb, w1b, w3b, w2b, gsem, ssem, *, n_tok):
    j = pl.program_id(0)
    n_used = nused_ref[0]
    blk = xa.shape[0]
    n_blocks = base_ref.shape[0] - 1
    tok_mask = n_tok - 1
    assert n_tok & tok_mask == 0

    def gather_copy(base, r, x_dst, sem):
        tok = rows_ref[base + r] & tok_mask
        return pltpu.make_async_copy(h_ref.at[pl.ds(tok, 1)], x_dst.at[pl.ds(r, 1)], sem)

    def scatter_copy(base, r, y_src, sem):
        return pltpu.make_async_copy(y_src.at[pl.ds(r, 1)], ys_ref.at[pl.ds(rows_ref[base + r], 1)], sem)

    def wait_gathered(x_dst, sem):
        pltpu.make_async_copy(h_ref.at[pl.ds(0, blk)], x_dst, sem).wait()

    def wait_scattered(y_src, sem):
        pltpu.make_async_copy(y_src, ys_ref.at[pl.ds(0, blk)], sem).wait()

    def rolled(start_row_copy):
        def body(c, carry):
            for k in range(ROW_UNROLL):
                start_row_copy(c * ROW_UNROLL + k)
            return carry
        lax.fori_loop(0, blk // ROW_UNROLL, body, 0)

    def block_step(b, x_own, x_other, y_own, y_other, g_own, g_other, s_own, s_other, leads_pair):
        @pl.when(b < n_used)
        def _():
            if leads_pair:
                @pl.when(b == 0)
                def _():
                    rolled(lambda r: gather_copy(base_ref[0], r, x_own, g_own).start())
                    y_other[...] = jnp.zeros_like(y_other)

            wait_gathered(x_own, g_own)

            @pl.when(b >= 1)
            def _():
                wait_scattered(y_own, s_own)

            if leads_pair:
                @pl.when((b == 0) | (be_ref[b] != be_ref[jnp.maximum(b - 2, 0)]))
                def _():
                    w1b[...] = w1_ref[0, 0].astype(BF16)
                    w3b[...] = w3_ref[0, 0].astype(BF16)
                    w2b[...] = w2_ref[0, 0].astype(BF16)

            base_next = base_ref[jnp.minimum(b + 1, n_blocks)]
            base_prev = base_ref[jnp.where(b >= 1, b - 1, n_blocks)]
            for r in range(blk):
                gather_copy(base_next, r, x_other, g_other).start()
                scatter_copy(base_prev, r, y_other, s_other).start()
            x = x_own[...].astype(BF16)
            a = jnp.dot(x, w1b[...], preferred_element_type=F32)
            g = jnp.dot(x, w3b[...], preferred_element_type=F32)
            hb = (a * jax.nn.sigmoid(a) * g).astype(BF16)
            y_own[...] = jnp.dot(hb, w2b[...], preferred_element_type=F32)

            @pl.when(b == n_used - 1)
            def _():
                wait_scattered(y_other, s_other)
                rolled(lambda r: scatter_copy(base_ref[b], r, y_own, s_own).start())
                wait_scattered(y_own, s_own)
                wait_gathered(x_other, g_other)

    block_step(2 * j, xa, xb, ya, yb, gsem.at[0], gsem.at[1], ssem.at[0], ssem.at[1], True)
    block_step(2 * j + 1, xb, xa, yb, ya, gsem.at[1], gsem.at[0], ssem.at[1], ssem.at[0], False)


def moe_experts(h1, rows, base, block_e, n_used, w1, w3, w2, layer, blk):
    T, D = h1.shape
    Fd = w1.shape[-1]
    nb = block_e.shape[0]
    assert nb % 2 == 0
    wspec = lambda s: pl.BlockSpec((1, 1) + s, lambda j, be, *_: (layer, be[2 * j], 0, 0))
    return pl.pallas_call(
        functools.partial(_expert_kernel, n_tok=T),
        grid_spec=pltpu.PrefetchScalarGridSpec(
            num_scalar_prefetch=4,
            grid=(nb // 2,),
            in_specs=[pl.BlockSpec(memory_space=pl.ANY), wspec((D, Fd)), wspec((D, Fd)), wspec((Fd, D))],
            out_specs=pl.BlockSpec(memory_space=pl.ANY),
            scratch_shapes=[pltpu.VMEM((blk, D), F32), pltpu.VMEM((blk, D), F32),
                            pltpu.VMEM((blk, D), F32), pltpu.VMEM((blk, D), F32),
                            pltpu.VMEM((D, Fd), BF16), pltpu.VMEM((D, Fd), BF16), pltpu.VMEM((Fd, D), BF16),
                            pltpu.SemaphoreType.DMA((2,)), pltpu.SemaphoreType.DMA((2,))],
        ),
        out_shape=jax.ShapeDtypeStruct((2 * T + blk, D), F32),
        compiler_params=pltpu.CompilerParams(dimension_semantics=("arbitrary",), vmem_limit_bytes=VMEM_LIMIT,
                                             has_side_effects=True),
        name="moe_experts",
    )(block_e, n_used, base, rows, h1, w1, w3, w2)


def _combine_kernel(y1_ref, y2_ref, h_ref, info_ref, g_ref, b_ref, o_ref):
    info = info_ref[...]
    y = info[:, 2:3] * y1_ref[...] + info[:, 3:4] * y2_ref[...]
    o_ref[...] = _layer_norm(ALPHA * h_ref[...] + y, g_ref[...], b_ref[...])


def moe_combine(ys, h, info, ln_g, ln_b, tm=512):
    T, D = h.shape
    nt = T // tm
    return pl.pallas_call(
        _combine_kernel,
        grid=(nt,),
        in_specs=[pl.BlockSpec((tm, D), lambda i: (i, 0)),
                  pl.BlockSpec((tm, D), lambda i: (nt + i, 0)),
                  pl.BlockSpec((tm, D), lambda i: (i, 0)),
                  pl.BlockSpec((tm, LANES), lambda i: (i, 0)),
                  _const_spec((1, D)), _const_spec((1, D))],
        out_specs=pl.BlockSpec((tm, D), lambda i: (i, 0)),
        out_shape=jax.ShapeDtypeStruct((T, D), F32),
        compiler_params=_params("parallel"),
        name="moe_combine",
    )(ys, ys, h, info, ln_g.reshape(1, D), ln_b.reshape(1, D))


def route_tables(info_t, counts, blk, n_blocks):
    n_assign = info_t.shape[1] * 2
    pair = 2 * blk
    experts = info_t[0:2].reshape(-1).astype(jnp.int32)
    pad_experts = jnp.repeat(jnp.arange(N_EXPERTS, dtype=jnp.int32), pair)
    order = jnp.argsort(jnp.concatenate([2 * experts, 2 * pad_experts + 1]), stable=True).astype(jnp.int32)
    rows = jnp.where(order < n_assign, order, n_assign + (order - n_assign) % blk)
    cnt = counts[0, :N_EXPERTS].astype(jnp.int32)
    padded = (cnt + pair - 1) // pair * pair
    pends = jnp.cumsum(padded)
    block_row = jnp.arange(n_blocks, dtype=jnp.int32) * blk
    block_e = jnp.minimum(jnp.sum(block_row[:, None] >= pends[None, :], axis=1), N_EXPERTS - 1).astype(jnp.int32)
    of_block = block_e[:, None] == jnp.arange(N_EXPERTS, dtype=jnp.int32)[None, :]

    def per_block(table):
        return jnp.sum(jnp.where(of_block, table[None, :], 0), axis=1)

    segment_start = jnp.cumsum(cnt + pair) - (cnt + pair)
    base = per_block(segment_start) + block_row - per_block(pends - padded)
    all_padding = rows.shape[0] - blk
    base = jnp.where(block_row < pends[-1], base, all_padding)
    base = jnp.concatenate([base, jnp.full((1,), all_padding, jnp.int32)]).astype(jnp.int32)
    n_used = (pends[-1:] // blk).astype(jnp.int32)
    return rows, base, block_e, n_used


def grouped_moe_norm(h1, info, info_t, counts, w1, w3, w2, layer, ln_g, ln_b, blk=512):
    T, D = h1.shape
    n_blocks = (2 * T) // blk + 2 * N_EXPERTS
    rows, base, block_e, n_used = route_tables(info_t, counts, blk, n_blocks)
    ys = moe_experts(h1, rows, base, block_e, n_used, w1, w3, w2, layer, blk)
    return moe_combine(ys, h1, info, ln_g, ln_b)


def _in_proj_layout(d_model):
    s5w = d_model // 4
    sbw = SB_HEADS * SB_HEAD_DIM
    rqk = RET_HEADS * RET_DK
    rv = RET_HEADS * RET_DV
    sizes = (s5w, sbw, sbw, sbw, rqk, rqk, rv, rv, d_model, d_model, d_model)
    off = np.concatenate([[0], np.cumsum(sizes)]).tolist()

    def permute(w):
        u_a, q_b, k_b, v_b, q_c, k_c, v_c, g_c, ga, gb, gc = [w[:, off[i]:off[i + 1]] for i in range(11)]

        def rope_perm(m):
            m = m.reshape(d_model, RET_HEADS, 2, RET_DK // 2)
            return jnp.swapaxes(m, 1, 2).reshape(d_model, rqk)

        parts = [ga, gb, gc, rope_perm(q_c), rope_perm(k_c) * RET_DK ** -0.5, v_c, g_c,
                 q_b * SB_HEAD_DIM ** -0.5, k_b, v_b, u_a]
        return jnp.concatenate([m.astype(BF16) for m in parts], axis=1)

    o_qk = 3 * d_model
    o_v = o_qk + 2 * rqk
    o_g = o_v + rv
    o_sb = o_g + rv
    o_u = o_sb + 3 * sbw
    blocks = dict(qk=o_qk // (2 * rqk), v=o_v // rv, g=o_g // rv, sb_q=o_sb // LANES,
                  sb_k=(o_sb + sbw) // LANES, sb_v=(o_sb + 2 * sbw) // LANES, u=o_u // s5w)
    assert o_qk % (2 * rqk) == 0 and o_v % rv == 0 and o_g % rv == 0 and o_sb % LANES == 0 and o_u % s5w == 0
    return permute, blocks


def _layer(h, batch, seq, p, ln0=None):
    D = h.shape[1]
    permute, blk = _in_proj_layout(D)
    hp, h = in_projection(h, permute(p['w_in']), n_gate_cols=3 * D, ln=ln0)
    s5_tile = 512
    bblk, lam, lamk, cblk = s5_tables(p['lam_re'], p['lam_im'], p['log_dt'], p['b_re'], p['b_im'],
                                      p['c_re'], p['c_im'], s5_tile // SUBLANES)
    ya = s5_mixer(hp, blk['u'], batch, seq, bblk, lam, lamk, cblk, p['d_skip'], p['w_glu'].astype(BF16),
                  p['b_glu'], tm=s5_tile)
    yb = stick_breaking(hp, blk['sb_q'], blk['sb_k'], blk['sb_v'], batch, seq)
    yc = retention(hp, blk['qk'], blk['v'], blk['g'], batch, seq, p['gn_g'])
    rw = jnp.zeros((D, LANES), F32).at[:, :N_EXPERTS].set(p['router_w'])
    rw_hi = rw.astype(BF16)
    rw = jnp.stack([rw_hi, (rw - rw_hi.astype(F32)).astype(BF16)])
    rb = jnp.full((1, LANES), NEG_BIG, F32).at[0, :N_EXPERTS].set(p['router_b'])
    h1, info, info_t, counts = merge_norm_route(
        hp, ya, yb, yc, h, p['w_up_a'].astype(BF16), p['w_up_b'].astype(BF16), p['w_up_c'].astype(BF16),
        p['w_out'].astype(BF16), p['ln1_g'], p['ln1_b'], rw, rb)
    return grouped_moe_norm(h1, info, info_t, counts, p['moe_w1'], p['moe_w3'], p['moe_w2'], p['layer'],
                            p['ln2_g'], p['ln2_b'])


def kernel(x, ln0_g, ln0_b, w_in, s5_lambda_re, s5_lambda_im, s5_log_dt, s5_b_re, s5_b_im, s5_c_re, s5_c_im,
           s5_d, s5_w_glu, s5_b_glu, ret_gn_g, w_up_a, w_up_b, w_up_c, w_out, ln1_g, ln1_b, router_w, router_b,
           moe_w1, moe_w3, moe_w2, ln2_g, ln2_b):
    batch, seq, D = x.shape
    h = x.reshape(batch * seq, D)
    for l in range(w_in.shape[0]):
        p = dict(w_in=w_in[l], lam_re=s5_lambda_re[l], lam_im=s5_lambda_im[l], log_dt=s5_log_dt[l],
                 b_re=s5_b_re[l], b_im=s5_b_im[l], c_re=s5_c_re[l], c_im=s5_c_im[l], d_skip=s5_d[l],
                 w_glu=s5_w_glu[l], b_glu=s5_b_glu[l], gn_g=ret_gn_g[l], w_up_a=w_up_a[l], w_up_b=w_up_b[l],
                 w_up_c=w_up_c[l], w_out=w_out[l], ln1_g=ln1_g[l], ln1_b=ln1_b[l], router_w=router_w,
                 router_b=router_b, moe_w1=moe_w1, moe_w3=moe_w3, moe_w2=moe_w2, layer=l, ln2_g=ln2_g[l],
                 ln2_b=ln2_b[l])
        h = _layer(h, batch, seq, p, ln0=(ln0_g, ln0_b) if l == 0 else None)
    return h.reshape(batch, seq, D)
```

```python
import functools
import math

import numpy as np
import jax
import jax.numpy as jnp
from jax import lax
from jax.experimental import pallas as pl
from jax.experimental.pallas import tpu as pltpu

F32 = jnp.float32
BF16 = jnp.bfloat16

LANES = 128
SUBLANES = 8
VMEM_LIMIT = 56 * 1024 * 1024

DEPTH = 2
CHUNK = 64
S5_GROUP_CH = 16
S5_STATE = 64
SB_HEADS = 4
SB_HEAD_DIM = 64
RET_HEADS = 4
RET_DK = 64
RET_DV = 128
ROPE_BASE = 10000.0
N_EXPERTS = 16
EXPERTS_PER_GROUP = 4
ALPHA = (2 * DEPTH) ** 0.25
LN_EPS = 1e-5

GATE_OFF = 0
NEG_BIG = -1e30


def _params(*sem):
    return pltpu.CompilerParams(dimension_semantics=sem, vmem_limit_bytes=VMEM_LIMIT)


def _const_spec(shape):
    zeros = (0,) * len(shape)
    return pl.BlockSpec(shape, lambda *_: zeros, pipeline_mode=pl.Buffered(1))


def _layer_norm(x, g, b):
    mu = jnp.mean(x, axis=-1, keepdims=True)
    xc = x - mu
    var = jnp.mean(xc * xc, axis=-1, keepdims=True)
    return xc * lax.rsqrt(var + LN_EPS) * g + b


def _moe_post_norm(y1, y2, h1, info, g, b):
    return _layer_norm(ALPHA * h1 + info[:, 2:3] * y1 + info[:, 3:4] * y2, g, b)


def _inproj_kernel(*refs, col_chunk, n_gate_cols, from_moe):
    if from_moe:
        y1_ref, y2_ref, h1_ref, info_ref, g_ref, b_ref, w_ref, o_ref, h_ref = refs
        x = _moe_post_norm(y1_ref[...], y2_ref[...], h1_ref[...], info_ref[...], g_ref[...], b_ref[...])
    else:
        x_ref, g_ref, b_ref, w_ref, o_ref, h_ref = refs
        x = _layer_norm(x_ref[...], g_ref[...], b_ref[...])
    h_ref[...] = x
    xb = x.astype(BF16)
    n = w_ref.shape[1]
    for c in range(0, n, col_chunk):
        acc = jnp.dot(xb, w_ref[:, c:c + col_chunk], preferred_element_type=F32)
        if c < n_gate_cols:
            acc = jax.nn.sigmoid(acc)
        o_ref[:, c:c + col_chunk] = acc.astype(BF16)


def in_projection(source, w_bf16, n_gate_cols, tm=512, col_chunk=512):
    kind, *operands = source
    gain, bias = operands[-2:]
    T, D = operands[1].shape if kind == "moe" else operands[0].shape
    N = w_bf16.shape[1]
    nt = T // tm
    assert n_gate_cols % col_chunk == 0
    rows = pl.BlockSpec((tm, D), lambda i: (i, 0))
    if kind == "moe":
        ys, h1, info = operands[:3]
        args = [ys, ys, h1, info]
        in_specs = [rows, pl.BlockSpec((tm, D), lambda i: (nt + i, 0)), rows,
                    pl.BlockSpec((tm, LANES), lambda i: (i, 0))]
    else:
        args, in_specs = [operands[0]], [rows]
    return pl.pallas_call(
        functools.partial(_inproj_kernel, col_chunk=col_chunk, n_gate_cols=n_gate_cols, from_moe=kind == "moe"),
        grid=(nt,),
        in_specs=in_specs + [_const_spec((1, D)), _const_spec((1, D)), _const_spec((D, N))],
        out_specs=[pl.BlockSpec((tm, N), lambda i: (i, 0)), rows],
        out_shape=[jax.ShapeDtypeStruct((T, N), BF16), jax.ShapeDtypeStruct((T, D), F32)],
        compiler_params=_params("parallel"),
        name="in_proj",
    )(*args, gain.reshape(1, D), bias.reshape(1, D), w_bf16)


def _s5_kernel(u_ref, bblk_ref, lam_ref, lamk_ref, cblk_ref, d_ref, wglu_ref, bglu_ref,
               o_ref, io_ref, s_ref, carry_ref, *, n_state):
    tm, width = u_ref.shape
    K = tm // SUBLANES
    ncol = n_state // LANES
    nio = width // LANES

    @pl.when(pl.program_id(1) == 0)
    def _():
        carry_ref[...] = jnp.zeros_like(carry_ref)

    def sub_chunk_rows(k):
        return pl.ds(k, SUBLANES, stride=K)

    for c in range(nio):
        io_ref[c] = u_ref[:, c * LANES:(c + 1) * LANES].astype(F32)
    u = jnp.concatenate(
        [jnp.concatenate([io_ref[c, sub_chunk_rows(k), :] for k in range(K)], axis=0) for c in range(nio)],
        axis=1)
    s_ref[...] = jnp.dot(u.astype(BF16), bblk_ref[...], preferred_element_type=F32)

    def cols(j):
        return pl.ds(j * LANES, LANES), pl.ds(n_state + j * LANES, LANES)

    def rows(k):
        return pl.ds(pl.multiple_of(k * SUBLANES, SUBLANES), SUBLANES)

    a_re = [jnp.broadcast_to(lam_ref[0:1, pl.ds(j * LANES, LANES)], (SUBLANES, LANES)) for j in range(ncol)]
    a_im = [jnp.broadcast_to(lam_ref[1:2, pl.ds(j * LANES, LANES)], (SUBLANES, LANES)) for j in range(ncol)]

    def pass1(k, st):
        out = []
        for j in range(ncol):
            cr, ci = cols(j)
            sr, si = st[2 * j], st[2 * j + 1]
            nr = a_re[j] * sr - a_im[j] * si + s_ref[rows(k), cr]
            ni = a_re[j] * si + a_im[j] * sr + s_ref[rows(k), ci]
            s_ref[rows(k), cr] = nr
            s_ref[rows(k), ci] = ni
            out += [nr, ni]
        return tuple(out)

    zero = jnp.zeros((SUBLANES, LANES), F32)
    ends = lax.fori_loop(0, K, pass1, (zero,) * (2 * ncol))

    carries = []
    for j in range(ncol):
        cr, ci = cols(j)
        kr = lamk_ref[0:1, pl.ds(j * LANES, LANES)]
        ki = lamk_ref[1:2, pl.ds(j * LANES, LANES)]
        er, ei = ends[2 * j], ends[2 * j + 1]
        c_r = [carry_ref[0:1, cr]]
        c_i = [carry_ref[0:1, ci]]
        for r in range(1, SUBLANES + 1):
            pr, pi = c_r[-1], c_i[-1]
            c_r.append(kr * pr - ki * pi + er[r - 1:r, :])
            c_i.append(kr * pi + ki * pr + ei[r - 1:r, :])
        carry_ref[0:1, cr] = c_r[SUBLANES]
        carry_ref[0:1, ci] = c_i[SUBLANES]
        carries += [jnp.concatenate(c_r[:SUBLANES], axis=0), jnp.concatenate(c_i[:SUBLANES], axis=0)]

    def pass2(k, st):
        out = []
        for j in range(ncol):
            cr, ci = cols(j)
            dr, di = st[2 * j], st[2 * j + 1]
            nr = a_re[j] * dr - a_im[j] * di
            ni = a_re[j] * di + a_im[j] * dr
            s_ref[rows(k), cr] = s_ref[rows(k), cr] + nr
            s_ref[rows(k), ci] = s_ref[rows(k), ci] + ni
            out += [nr, ni]
        return tuple(out)

    lax.fori_loop(0, K, pass2, tuple(carries))

    y = jnp.dot(s_ref[...].astype(BF16), cblk_ref[...], preferred_element_type=F32)
    y = y + d_ref[...] * u
    y = jax.nn.gelu(y, approximate=True)
    gate = jnp.dot(y.astype(BF16), wglu_ref[...], preferred_element_type=F32) + bglu_ref[...]
    out = y * jax.nn.sigmoid(gate)
    for c in range(nio):
        for k in range(K):
            io_ref[c, sub_chunk_rows(k), :] = out[k * SUBLANES:(k + 1) * SUBLANES, c * LANES:(c + 1) * LANES]
    o_ref[...] = jnp.concatenate([io_ref[c] for c in range(nio)], axis=1).astype(BF16)


def s5_mixer(hp, u_col_block, batch, seq, bblk, lam, lamk, cblk, d_skip, w_glu, b_glu, tm=512):
    width = bblk.shape[0]
    n_state = bblk.shape[1] // 2
    nt = seq // tm
    return pl.pallas_call(
        functools.partial(_s5_kernel, n_state=n_state),
        grid=(batch, nt),
        in_specs=[
            pl.BlockSpec((tm, width), lambda b, t: (b * nt + t, u_col_block)),
            _const_spec(bblk.shape), _const_spec(lam.shape), _const_spec(lamk.shape),
            _const_spec(cblk.shape), _const_spec((1, width)), _const_spec(w_glu.shape),
            _const_spec((1, width)),
        ],
        out_specs=pl.BlockSpec((tm, width), lambda b, t: (b * nt + t, 0)),
        out_shape=jax.ShapeDtypeStruct((batch * seq, width), BF16),
        scratch_shapes=[pltpu.VMEM((width // LANES, tm, LANES), F32),
                        pltpu.VMEM((tm, 2 * n_state), F32),
                        pltpu.VMEM((SUBLANES, 2 * n_state), F32)],
        compiler_params=_params("parallel", "arbitrary"),
        name="s5_mixer",
    )(hp, bblk, lam, lamk, cblk, d_skip.reshape(1, width), w_glu, b_glu.reshape(1, width))


def s5_tables(lam_re, lam_im, log_dt, b_re, b_im, c_re, c_im, sub_chunk):
    G, P = lam_re.shape
    dt = jnp.exp(log_dt.astype(F32))[:, None]

    def lam_pow(n):
        mag = jnp.exp(lam_re * dt * n)
        return mag * jnp.cos(lam_im * dt * n), mag * jnp.sin(lam_im * dt * n)

    lr, li = lam_pow(1.0)
    kr, ki = lam_pow(float(sub_chunk))
    nr, ni = lr - 1.0, li
    den = lam_re * lam_re + lam_im * lam_im
    zr = (nr * lam_re + ni * lam_im) / den
    zi = (ni * lam_re - nr * lam_im) / den
    bbr = b_re * zr[..., None] - b_im * zi[..., None]
    bbi = b_re * zi[..., None] + b_im * zr[..., None]
    eye = jnp.eye(G, dtype=F32)
    Cg = b_re.shape[-1]

    def in_blk(m):
        return jnp.einsum('gpc,gh->gchp', m, eye).reshape(G * Cg, G * P)

    def out_blk(m):
        return jnp.einsum('gcp,gh->gphc', m, eye).reshape(G * P, G * Cg)

    bblk = jnp.concatenate([in_blk(bbr), in_blk(bbi)], axis=1).astype(BF16)
    cblk = jnp.concatenate([out_blk(c_re), -out_blk(c_im)], axis=0).astype(BF16)
    lam = jnp.stack([lr.reshape(-1), li.reshape(-1)])
    lamk = jnp.stack([kr.reshape(-1), ki.reshape(-1)])
    return bblk, lam, lamk, cblk


def _log_sigmoid_pair(z):
    soft = jnp.log(1.0 + jnp.exp(-jnp.abs(z)))
    lp = jnp.minimum(z, 0.0) - soft
    return lp, lp - z


SB_EXIT = -110.0


def _sb_kernel(q_ref, k_ref, v_ref, tri_ref, o_ref, acc_ref, carry_ref, *, blk, group, nsub):
    i = pl.program_id(2)
    lane = lax.broadcasted_iota(jnp.int32, (1, LANES), 1)
    head_lanes = [lane < SB_HEAD_DIM, lane >= SB_HEAD_DIM]
    zero = jnp.zeros((), BF16)
    qh = []
    for s in range(nsub):
        q = q_ref[s * blk:(s + 1) * blk, :]
        qh.append([jnp.where(m, q, zero) for m in head_lanes])
    row = lax.broadcasted_iota(jnp.int32, (blk, blk), 0)
    col = lax.broadcasted_iota(jnp.int32, (blk, blk), 1)
    strictly_earlier = col < row

    acc_ref[...] = jnp.zeros_like(acc_ref)
    carry_ref[...] = jnp.zeros_like(carry_ref)

    def group_step(g, first):
        ks, vs = {}, {}
        for d in range(1 - group, nsub):
            j = i * nsub - g * group + d
            start = pl.multiple_of(jnp.maximum(j, 0) * blk, blk)
            ks[d] = k_ref[pl.ds(start, blk), :]
            vj = v_ref[pl.ds(start, blk), :]
            vs[d] = [jnp.where(m & (j >= 0), vj, zero) for m in head_lanes]
        chains = [(s, u, h) for s in range(nsub) for u in range(group) for h in range(2)]
        zs = {c: lax.dot_general(qh[c[0]][c[2]], ks[c[0] - c[1]], (((1,), (1,)), ((), ())),
                                 preferred_element_type=F32) for c in chains}
        lps, rs = {}, {}
        for c in chains:
            lp, l1m = _log_sigmoid_pair(zs[c])
            if first and c[1] == 0:
                l1m = jnp.where(strictly_earlier, l1m, 0.0)
            lps[c] = lp
            rs[c] = jnp.dot(l1m.astype(BF16), tri_ref[...], preferred_element_type=F32)
        ws = {}
        top = None
        for s in range(nsub):
            for h in range(2):
                cum = carry_ref[s, h]
                for u in range(group):
                    c = (s, u, h)
                    w = jnp.exp(lps[c] + rs[c][:, :blk] + cum)
                    if first and u == 0:
                        w = jnp.where(strictly_earlier, w, 0.0)
                    ws[c] = w.astype(BF16)
                    cum = cum + rs[c][:, blk:]
                carry_ref[s, h] = cum
                top = cum if top is None else jnp.maximum(top, cum)
        for s in range(nsub):
            acc = acc_ref[s]
            for u in range(group):
                for h in range(2):
                    acc = acc + jnp.dot(ws[(s, u, h)], vs[s - u][h], preferred_element_type=F32)
            acc_ref[s] = acc
        return jnp.max(top)

    def more(state):
        g, top = state
        return (g * group <= i * nsub + nsub - 1) & (top > SB_EXIT)

    def step(state):
        g, _ = state
        return g + 1, group_step(g, False)

    lax.while_loop(more, step, (1, group_step(0, True)))
    for s in range(nsub):
        o_ref[s * blk:(s + 1) * blk, :] = acc_ref[s].astype(BF16)


def stick_breaking(hp, q_blk0, k_blk0, v_blk0, batch, seq, blk=128, group=3, nsub=8):
    assert blk == LANES
    tile = nsub * blk
    nq = seq // tile
    n_pairs = SB_HEADS * SB_HEAD_DIM // LANES
    r = np.arange(blk)
    tri = np.concatenate([(r[:, None] > r[None, :]), np.ones((blk, blk), bool)], axis=1)
    tri = jnp.asarray(tri, BF16)
    return pl.pallas_call(
        functools.partial(_sb_kernel, blk=blk, group=group, nsub=nsub),
        grid=(batch, n_pairs, nq),
        in_specs=[
            pl.BlockSpec((tile, LANES), lambda b, p, i: (b * nq + i, q_blk0 + p)),
            pl.BlockSpec((seq, LANES), lambda b, p, i: (b, k_blk0 + p)),
            pl.BlockSpec((seq, LANES), lambda b, p, i: (b, v_blk0 + p)),
            _const_spec(tri.shape),
        ],
        out_specs=pl.BlockSpec((tile, LANES), lambda b, p, i: (b * nq + i, p)),
        out_shape=jax.ShapeDtypeStruct((batch * seq, n_pairs * LANES), BF16),
        scratch_shapes=[pltpu.VMEM((nsub, blk, LANES), F32), pltpu.VMEM((nsub, 2, blk, blk), F32)],
        compiler_params=_params("parallel", "parallel", "arbitrary"),
        name="stick_breaking",
    )(hp, hp, hp, tri)


def _ret_kernel(qk_ref, v_ref, g_ref, cos_ref, sin_ref, mask_ref, rdec_ref, cdec_ref, tdec_ref, gn_ref,
                o_ref, state_ref):
    @pl.when(pl.program_id(1) == 0)
    def _():
        state_ref[...] = jnp.zeros_like(state_ref)

    half = RET_HEADS * RET_DK // 2
    qk = qk_ref[...].astype(F32)
    cos, sin = cos_ref[...], sin_ref[...]

    def rope(t):
        t1, t2 = t[:, :half], t[:, half:]
        return jnp.concatenate([t1 * cos - t2 * sin, t1 * sin + t2 * cos], axis=1)

    qr = rope(qk[:, :2 * half])
    kr = rope(qk[:, 2 * half:])
    kb = kr.astype(BF16)
    kt = kr.T
    lane = lax.broadcasted_iota(jnp.int32, (1, 2 * half), 1)
    head_of_lane = (lane % half) // (RET_DK // 2)
    for h in range(RET_HEADS):
        qh = jnp.where(head_of_lane == h, qr, 0.0)
        vh = v_ref[:, h * RET_DV:(h + 1) * RET_DV]
        scores = lax.dot_general(qh.astype(BF16), kb, (((1,), (1,)), ((), ())), preferred_element_type=F32)
        scores = scores * mask_ref[h]
        o = jnp.dot(scores.astype(BF16), vh, preferred_element_type=F32)
        o = o + jnp.dot((qh * rdec_ref[h]).astype(BF16), state_ref[h].astype(BF16), preferred_element_type=F32)
        kv = jnp.dot((kt * cdec_ref[h]).astype(BF16), vh, preferred_element_type=F32)
        state_ref[h] = tdec_ref[h] * state_ref[h] + kv
        mu = jnp.mean(o, axis=-1, keepdims=True)
        oc = o - mu
        var = jnp.mean(oc * oc, axis=-1, keepdims=True)
        on = oc * lax.rsqrt(var + LN_EPS) * gn_ref[:, h * RET_DV:(h + 1) * RET_DV]
        g = g_ref[:, h * RET_DV:(h + 1) * RET_DV].astype(F32)
        o_ref[:, h * RET_DV:(h + 1) * RET_DV] = (g * jax.nn.sigmoid(g) * on).astype(BF16)


def retention_tables(seq, tile):
    halfdim = RET_DK // 2
    inv_freq = ROPE_BASE ** (-jnp.arange(halfdim, dtype=F32) / halfdim)
    ang = jnp.arange(seq, dtype=F32)[:, None] * inv_freq[None, :]
    cos = jnp.tile(jnp.cos(ang), (1, RET_HEADS))
    sin = jnp.tile(jnp.sin(ang), (1, RET_HEADS))
    log_gamma = jnp.log(1.0 - 2.0 ** (-5.0 - jnp.arange(RET_HEADS, dtype=F32)))
    t = jnp.arange(tile, dtype=F32)
    same_or_earlier_chunk = (t[None, :] // CHUNK) <= (t[:, None] // CHUNK)
    mask = jnp.exp(log_gamma[:, None, None] * jnp.abs(t[:, None] - t[None, :]))
    mask = jnp.where(same_or_earlier_chunk[None], mask, 0.0)
    rdec = jnp.exp(log_gamma[:, None, None] * (t[None, :, None] + 1.0))
    cdec = jnp.exp(log_gamma[:, None, None] * (tile - 1.0 - t[None, None, :]))
    tdec = jnp.broadcast_to(jnp.exp(log_gamma * tile)[:, None, None], (RET_HEADS, 1, RET_DV))
    return cos, sin, mask, rdec, cdec, tdec


def retention(hp, qk_blk, v_blk, g_blk, batch, seq, gn_g, tile=512):
    cos, sin, mask, rdec, cdec, tdec = retention_tables(seq, tile)
    nt = seq // tile
    qkw = 2 * RET_HEADS * RET_DK
    vw = RET_HEADS * RET_DV
    half = RET_HEADS * RET_DK // 2
    return pl.pallas_call(
        _ret_kernel,
        grid=(batch, nt),
        in_specs=[
            pl.BlockSpec((tile, qkw), lambda b, t: (b * nt + t, qk_blk)),
            pl.BlockSpec((tile, vw), lambda b, t: (b * nt + t, v_blk)),
            pl.BlockSpec((tile, vw), lambda b, t: (b * nt + t, g_blk)),
            pl.BlockSpec((tile, half), lambda b, t: (t, 0)),
            pl.BlockSpec((tile, half), lambda b, t: (t, 0)),
            _const_spec(mask.shape), _const_spec(rdec.shape), _const_spec(cdec.shape),
            _const_spec(tdec.shape), _const_spec((1, vw)),
        ],
        out_specs=pl.BlockSpec((tile, vw), lambda b, t: (b * nt + t, 0)),
        out_shape=jax.ShapeDtypeStruct((batch * seq, vw), BF16),
        scratch_shapes=[pltpu.VMEM((RET_HEADS, 2 * half, RET_DV), F32)],
        compiler_params=_params("parallel", "arbitrary"),
        name="retention",
    )(hp, hp, hp, cos, sin, mask, rdec, cdec, tdec, gn_g.reshape(1, vw))


def _first_max(vals, lane):
    m = jnp.max(vals, axis=-1, keepdims=True)
    idx = jnp.min(jnp.where(vals == m, lane, float(LANES)), axis=-1, keepdims=True)
    return m, idx


def _merge_kernel(ga_ref, gb_ref, gc_ref, ya_ref, yb_ref, yc_ref, h_ref, wa_ref, wb_ref, wc_ref, wo_ref,
                  g1_ref, b1_ref, rw_ref, rb_ref, h1_ref, info_ref, infot_ref, cnt_ref, run_ref, merged_ref,
                  hprev_ref, *, col_chunk):
    step = pl.program_id(0)

    @pl.when(step == 0)
    def _():
        run_ref[...] = jnp.zeros_like(run_ref)
        hprev_ref[...] = jnp.zeros_like(hprev_ref)

    _route(hprev_ref[...], rw_ref, rb_ref, info_ref, infot_ref, cnt_ref, run_ref, count=step > 0)

    d_model = h_ref.shape[1]
    for c in range(0, d_model, col_chunk):
        cs = slice(c, c + col_chunk)
        acc = ga_ref[:, cs].astype(F32) * jnp.dot(ya_ref[...], wa_ref[:, cs], preferred_element_type=F32)
        acc = acc + gb_ref[:, cs].astype(F32) * jnp.dot(yb_ref[...], wb_ref[:, cs], preferred_element_type=F32)
        acc = acc + gc_ref[:, cs].astype(F32) * jnp.dot(yc_ref[...], wc_ref[:, cs], preferred_element_type=F32)
        merged_ref[:, cs] = acc.astype(BF16)
    for c in range(0, d_model, col_chunk):
        cs = slice(c, c + col_chunk)
        h1_ref[:, cs] = ALPHA * h_ref[:, cs] + jnp.dot(merged_ref[...], wo_ref[:, cs], preferred_element_type=F32)
    h1 = _layer_norm(h1_ref[...], g1_ref[...], b1_ref[...])
    h1_ref[...] = h1
    hprev_ref[...] = h1


def _route(h1, rw_ref, rb_ref, info_ref, infot_ref, cnt_ref, run_ref, count):
    h_hi = h1.astype(BF16)
    h_lo = (h1 - h_hi.astype(F32)).astype(BF16)
    logits = (jnp.dot(h_hi, rw_ref[0], preferred_element_type=F32)
              + jnp.dot(h_lo, rw_ref[0], preferred_element_type=F32)
              + jnp.dot(h_hi, rw_ref[1], preferred_element_type=F32)) + rb_ref[...]
    tm = logits.shape[0]
    lane_i = lax.broadcasted_iota(jnp.int32, (tm, LANES), 1)
    lane = lane_i.astype(F32)
    ex = jnp.exp(logits - jnp.max(logits, axis=-1, keepdims=True))
    scores = ex / jnp.sum(ex, axis=-1, keepdims=True)
    group = (lane_i // EXPERTS_PER_GROUP).astype(F32)
    best = jnp.zeros((tm, 1), F32)
    best_score = jnp.full((tm, 1), -1.0, F32)
    for gi in range(N_EXPERTS // EXPERTS_PER_GROUP):
        sg = jnp.where(group == float(gi), scores, -1.0)
        m1, i1 = _first_max(sg, lane)
        m2 = jnp.max(jnp.where(lane == i1, -1.0, sg), axis=-1, keepdims=True)
        gs = m1 + m2
        better = gs > best_score
        best = jnp.where(better, float(gi), best)
        best_score = jnp.where(better, gs, best_score)
    masked = jnp.where(group == best, scores, -1.0)
    w1, e1 = _first_max(masked, lane)
    w2, e2 = _first_max(jnp.where(lane == e1, -2.0, masked), lane)
    den = w1 + w2
    gate1, gate2 = w1 / den, w2 / den

    onehot = jnp.where(((lane == e1) | (lane == e2)) & count, 1.0, 0.0)
    run_ref[0:1, :] = run_ref[0:1, :] + jnp.sum(onehot, axis=0, keepdims=True)
    cnt_ref[...] = run_ref[...]

    info = jnp.zeros((tm, LANES), F32)
    for k, val in enumerate((e1, e2, gate1, gate2)):
        info = jnp.where(lane_i == k, val, info)
    info_ref[...] = info
    infot_ref[...] = info.T[:SUBLANES]


def merge_norm_route(hp, ya, yb, yc, h, wa, wb, wc, wo, ln_g, ln_b, rw, rb, tm=512, col_chunk=512):
    T, D = h.shape
    nt = T // tm

    def tile(col):
        return lambda i: (jnp.minimum(i, nt - 1), col)

    routed = lambda i: jnp.maximum(i - 1, 0)
    return pl.pallas_call(
        functools.partial(_merge_kernel, col_chunk=col_chunk),
        grid=(nt + 1,),
        in_specs=[
            pl.BlockSpec((tm, D), tile(0)), pl.BlockSpec((tm, D), tile(1)), pl.BlockSpec((tm, D), tile(2)),
            pl.BlockSpec((tm, ya.shape[1]), tile(0)), pl.BlockSpec((tm, yb.shape[1]), tile(0)),
            pl.BlockSpec((tm, yc.shape[1]), tile(0)), pl.BlockSpec((tm, D), tile(0)),
            _const_spec(wa.shape), _const_spec(wb.shape), _const_spec(wc.shape), _const_spec(wo.shape),
            _const_spec((1, D)), _const_spec((1, D)), _const_spec(rw.shape), _const_spec(rb.shape),
        ],
        out_specs=[pl.BlockSpec((tm, D), tile(0)), pl.BlockSpec((tm, LANES), lambda i: (routed(i), 0)),
                   pl.BlockSpec((SUBLANES, tm), lambda i: (0, routed(i))),
                   pl.BlockSpec((SUBLANES, LANES), lambda i: (0, 0))],
        out_shape=[jax.ShapeDtypeStruct((T, D), F32), jax.ShapeDtypeStruct((T, LANES), F32),
                   jax.ShapeDtypeStruct((SUBLANES, T), F32), jax.ShapeDtypeStruct((SUBLANES, LANES), F32)],
        scratch_shapes=[pltpu.VMEM((SUBLANES, LANES), F32), pltpu.VMEM((tm, D), BF16), pltpu.VMEM((tm, D), F32)],
        compiler_params=_params("arbitrary"),
        name="merge_norm_route",
    )(hp, hp, hp, ya, yb, yc, h, wa, wb, wc, wo, ln_g.reshape(1, D), ln_b.reshape(1, D), rw, rb)


ROW_UNROLL = 8


def _expert_kernel(be_ref, nused_ref, base_ref, rows_ref, h_ref, w1_ref, w3_ref, w2_ref, ys_ref,
                   xa, xb, ya, yb, w1b, w3b, w2b, gsem, ssem, *, n_tok):
    j = pl.program_id(0)
    n_used = nused_ref[0]
    blk = xa.shape[0]
    n_blocks = base_ref.shape[0] - 1
    tok_mask = n_tok - 1
    assert n_tok & tok_mask == 0

    def gather_copy(base, r, x_dst, sem):
        tok = rows_ref[base + r] & tok_mask
        return pltpu.make_async_copy(h_ref.at[pl.ds(tok, 1)], x_dst.at[pl.ds(r, 1)], sem)

    def scatter_copy(base, r, y_src, sem):
        return pltpu.make_async_copy(y_src.at[pl.ds(r, 1)], ys_ref.at[pl.ds(rows_ref[base + r], 1)], sem)

    def wait_gathered(x_dst, sem):
        pltpu.make_async_copy(h_ref.at[pl.ds(0, blk)], x_dst, sem).wait()

    def wait_scattered(y_src, sem):
        pltpu.make_async_copy(y_src, ys_ref.at[pl.ds(0, blk)], sem).wait()

    def rolled(start_row_copy):
        def body(c, carry):
            for k in range(ROW_UNROLL):
                start_row_copy(c * ROW_UNROLL + k)
            return carry
        lax.fori_loop(0, blk // ROW_UNROLL, body, 0)

    def block_step(b, x_own, x_other, y_own, y_other, g_own, g_other, s_own, s_other, leads_pair):
        @pl.when(b < n_used)
        def _():
            if leads_pair:
                @pl.when(b == 0)
                def _():
                    rolled(lambda r: gather_copy(base_ref[0], r, x_own, g_own).start())
                    y_other[...] = jnp.zeros_like(y_other)

            wait_gathered(x_own, g_own)

            @pl.when(b >= 1)
            def _():
                wait_scattered(y_own, s_own)

            if leads_pair:
                @pl.when((b == 0) | (be_ref[b] != be_ref[jnp.maximum(b - 2, 0)]))
                def _():
                    w1b[...] = w1_ref[0, 0].astype(BF16)
                    w3b[...] = w3_ref[0, 0].astype(BF16)
                    w2b[...] = w2_ref[0, 0].astype(BF16)

            base_next = base_ref[jnp.minimum(b + 1, n_blocks)]
            base_prev = base_ref[jnp.where(b >= 1, b - 1, n_blocks)]
            for r in range(blk):
                gather_copy(base_next, r, x_other, g_other).start()
                scatter_copy(base_prev, r, y_other, s_other).start()
            x = x_own[...].astype(BF16)
            a = jnp.dot(x, w1b[...], preferred_element_type=F32)
            g = jnp.dot(x, w3b[...], preferred_element_type=F32)
            hb = (a * jax.nn.sigmoid(a) * g).astype(BF16)
            y_own[...] = jnp.dot(hb, w2b[...], preferred_element_type=F32)

            @pl.when(b == n_used - 1)
            def _():
                wait_scattered(y_other, s_other)
                rolled(lambda r: scatter_copy(base_ref[b], r, y_own, s_own).start())
                wait_scattered(y_own, s_own)
                wait_gathered(x_other, g_other)

    block_step(2 * j, xa, xb, ya, yb, gsem.at[0], gsem.at[1], ssem.at[0], ssem.at[1], True)
    block_step(2 * j + 1, xb, xa, yb, ya, gsem.at[1], gsem.at[0], ssem.at[1], ssem.at[0], False)


def moe_experts(h1, rows, base, block_e, n_used, w1, w3, w2, layer, blk):
    T, D = h1.shape
    Fd = w1.shape[-1]
    nb = block_e.shape[0]
    assert nb % 2 == 0
    wspec = lambda s: pl.BlockSpec((1, 1) + s, lambda j, be, *_: (layer, be[2 * j], 0, 0))
    return pl.pallas_call(
        functools.partial(_expert_kernel, n_tok=T),
        grid_spec=pltpu.PrefetchScalarGridSpec(
            num_scalar_prefetch=4,
            grid=(nb // 2,),
            in_specs=[pl.BlockSpec(memory_space=pl.ANY), wspec((D, Fd)), wspec((D, Fd)), wspec((Fd, D))],
            out_specs=pl.BlockSpec(memory_space=pl.ANY),
            scratch_shapes=[pltpu.VMEM((blk, D), F32), pltpu.VMEM((blk, D), F32),
                            pltpu.VMEM((blk, D), F32), pltpu.VMEM((blk, D), F32),
                            pltpu.VMEM((D, Fd), BF16), pltpu.VMEM((D, Fd), BF16), pltpu.VMEM((Fd, D), BF16),
                            pltpu.SemaphoreType.DMA((2,)), pltpu.SemaphoreType.DMA((2,))],
        ),
        out_shape=jax.ShapeDtypeStruct((2 * T + blk, D), F32),
        compiler_params=pltpu.CompilerParams(dimension_semantics=("arbitrary",), vmem_limit_bytes=VMEM_LIMIT,
                                             has_side_effects=True),
        name="moe_experts",
    )(block_e, n_used, base, rows, h1, w1, w3, w2)


def _combine_kernel(y1_ref, y2_ref, h_ref, info_ref, g_ref, b_ref, o_ref):
    o_ref[...] = _moe_post_norm(y1_ref[...], y2_ref[...], h_ref[...], info_ref[...], g_ref[...], b_ref[...])


def moe_combine(ys, h, info, ln_g, ln_b, tm=512):
    T, D = h.shape
    nt = T // tm
    return pl.pallas_call(
        _combine_kernel,
        grid=(nt,),
        in_specs=[pl.BlockSpec((tm, D), lambda i: (i, 0)),
                  pl.BlockSpec((tm, D), lambda i: (nt + i, 0)),
                  pl.BlockSpec((tm, D), lambda i: (i, 0)),
                  pl.BlockSpec((tm, LANES), lambda i: (i, 0)),
                  _const_spec((1, D)), _const_spec((1, D))],
        out_specs=pl.BlockSpec((tm, D), lambda i: (i, 0)),
        out_shape=jax.ShapeDtypeStruct((T, D), F32),
        compiler_params=_params("parallel"),
        name="moe_combine",
    )(ys, ys, h, info, ln_g.reshape(1, D), ln_b.reshape(1, D))


def route_tables(info_t, counts, blk, n_blocks):
    n_assign = info_t.shape[1] * 2
    pair = 2 * blk
    experts = info_t[0:2].reshape(-1).astype(jnp.int32)
    pad_experts = jnp.repeat(jnp.arange(N_EXPERTS, dtype=jnp.int32), pair)
    order = jnp.argsort(jnp.concatenate([2 * experts, 2 * pad_experts + 1]), stable=True).astype(jnp.int32)
    rows = jnp.where(order < n_assign, order, n_assign + (order - n_assign) % blk)
    cnt = counts[0, :N_EXPERTS].astype(jnp.int32)
    padded = (cnt + pair - 1) // pair * pair
    pends = jnp.cumsum(padded)
    block_row = jnp.arange(n_blocks, dtype=jnp.int32) * blk
    block_e = jnp.minimum(jnp.sum(block_row[:, None] >= pends[None, :], axis=1), N_EXPERTS - 1).astype(jnp.int32)
    of_block = block_e[:, None] == jnp.arange(N_EXPERTS, dtype=jnp.int32)[None, :]

    def per_block(table):
        return jnp.sum(jnp.where(of_block, table[None, :], 0), axis=1)

    segment_start = jnp.cumsum(cnt + pair) - (cnt + pair)
    base = per_block(segment_start) + block_row - per_block(pends - padded)
    all_padding = rows.shape[0] - blk
    base = jnp.where(block_row < pends[-1], base, all_padding)
    base = jnp.concatenate([base, jnp.full((1,), all_padding, jnp.int32)]).astype(jnp.int32)
    n_used = (pends[-1:] // blk).astype(jnp.int32)
    return rows, base, block_e, n_used


def grouped_moe(h1, info_t, counts, w1, w3, w2, layer, blk=512):
    T, D = h1.shape
    n_blocks = (2 * T) // blk + 2 * N_EXPERTS
    rows, base, block_e, n_used = route_tables(info_t, counts, blk, n_blocks)
    return moe_experts(h1, rows, base, block_e, n_used, w1, w3, w2, layer, blk)


def _in_proj_layout(d_model):
    s5w = d_model // 4
    sbw = SB_HEADS * SB_HEAD_DIM
    rqk = RET_HEADS * RET_DK
    rv = RET_HEADS * RET_DV
    sizes = (s5w, sbw, sbw, sbw, rqk, rqk, rv, rv, d_model, d_model, d_model)
    off = np.concatenate([[0], np.cumsum(sizes)]).tolist()

    def permute(w):
        u_a, q_b, k_b, v_b, q_c, k_c, v_c, g_c, ga, gb, gc = [w[:, off[i]:off[i + 1]] for i in range(11)]

        def rope_perm(m):
            m = m.reshape(d_model, RET_HEADS, 2, RET_DK // 2)
            return jnp.swapaxes(m, 1, 2).reshape(d_model, rqk)

        parts = [ga, gb, gc, rope_perm(q_c), rope_perm(k_c) * RET_DK ** -0.5, v_c, g_c,
                 q_b * SB_HEAD_DIM ** -0.5, k_b, v_b, u_a]
        return jnp.concatenate([m.astype(BF16) for m in parts], axis=1)

    o_qk = 3 * d_model
    o_v = o_qk + 2 * rqk
    o_g = o_v + rv
    o_sb = o_g + rv
    o_u = o_sb + 3 * sbw
    blocks = dict(qk=o_qk // (2 * rqk), v=o_v // rv, g=o_g // rv, sb_q=o_sb // LANES,
                  sb_k=(o_sb + sbw) // LANES, sb_v=(o_sb + 2 * sbw) // LANES, u=o_u // s5w)
    assert o_qk % (2 * rqk) == 0 and o_v % rv == 0 and o_g % rv == 0 and o_sb % LANES == 0 and o_u % s5w == 0
    return permute, blocks


def _layer(source, batch, seq, p):
    D = p['w_in'].shape[0]
    permute, blk = _in_proj_layout(D)
    hp, h = in_projection(source, permute(p['w_in']), n_gate_cols=3 * D)
    s5_tile = 512
    bblk, lam, lamk, cblk = s5_tables(p['lam_re'], p['lam_im'], p['log_dt'], p['b_re'], p['b_im'],
                                      p['c_re'], p['c_im'], s5_tile // SUBLANES)
    ya = s5_mixer(hp, blk['u'], batch, seq, bblk, lam, lamk, cblk, p['d_skip'], p['w_glu'].astype(BF16),
                  p['b_glu'], tm=s5_tile)
    yb = stick_breaking(hp, blk['sb_q'], blk['sb_k'], blk['sb_v'], batch, seq)
    yc = retention(hp, blk['qk'], blk['v'], blk['g'], batch, seq, p['gn_g'])
    rw = jnp.zeros((D, LANES), F32).at[:, :N_EXPERTS].set(p['router_w'])
    rw_hi = rw.astype(BF16)
    rw = jnp.stack([rw_hi, (rw - rw_hi.astype(F32)).astype(BF16)])
    rb = jnp.full((1, LANES), NEG_BIG, F32).at[0, :N_EXPERTS].set(p['router_b'])
    h1, info, info_t, counts = merge_norm_route(
        hp, ya, yb, yc, h, p['w_up_a'].astype(BF16), p['w_up_b'].astype(BF16), p['w_up_c'].astype(BF16),
        p['w_out'].astype(BF16), p['ln1_g'], p['ln1_b'], rw, rb)
    ys = grouped_moe(h1, info_t, counts, p['moe_w1'], p['moe_w3'], p['moe_w2'], p['layer'])
    return ("moe", ys, h1, info, p['ln2_g'], p['ln2_b'])


def kernel(x, ln0_g, ln0_b, w_in, s5_lambda_re, s5_lambda_im, s5_log_dt, s5_b_re, s5_b_im, s5_c_re, s5_c_im,
           s5_d, s5_w_glu, s5_b_glu, ret_gn_g, w_up_a, w_up_b, w_up_c, w_out, ln1_g, ln1_b, router_w, router_b,
           moe_w1, moe_w3, moe_w2, ln2_g, ln2_b):
    batch, seq, D = x.shape
    source = ("raw", x.reshape(batch * seq, D), ln0_g, ln0_b)
    for l in range(w_in.shape[0]):
        p = dict(w_in=w_in[l], lam_re=s5_lambda_re[l], lam_im=s5_lambda_im[l], log_dt=s5_log_dt[l],
                 b_re=s5_b_re[l], b_im=s5_b_im[l], c_re=s5_c_re[l], c_im=s5_c_im[l], d_skip=s5_d[l],
                 w_glu=s5_w_glu[l], b_glu=s5_b_glu[l], gn_g=ret_gn_g[l], w_up_a=w_up_a[l], w_up_b=w_up_b[l],
                 w_up_c=w_up_c[l], w_out=w_out[l], ln1_g=ln1_g[l], ln1_b=ln1_b[l], router_w=router_w,
                 router_b=router_b, moe_w1=moe_w1, moe_w3=moe_w3, moe_w2=moe_w2, layer=l, ln2_g=ln2_g[l],
                 ln2_b=ln2_b[l])
        source = _layer(source, batch, seq, p)
    return moe_combine(*source[1:]).reshape(batch, seq, D)
```

```python
import functools
import math

import numpy as np
import jax
import jax.numpy as jnp
from jax import lax
from jax.experimental import pallas as pl
from jax.experimental.pallas import tpu as pltpu

F32 = jnp.float32
BF16 = jnp.bfloat16

LANES = 128
SUBLANES = 8
VMEM_LIMIT = 56 * 1024 * 1024

DEPTH = 2
CHUNK = 64
S5_GROUP_CH = 16
S5_STATE = 64
SB_HEADS = 4
SB_HEAD_DIM = 64
RET_HEADS = 4
RET_DK = 64
RET_DV = 128
ROPE_BASE = 10000.0
N_EXPERTS = 16
EXPERTS_PER_GROUP = 4
ALPHA = (2 * DEPTH) ** 0.25
LN_EPS = 1e-5

GATE_OFF = 0
NEG_BIG = -1e30


def _params(*sem):
    return pltpu.CompilerParams(dimension_semantics=sem, vmem_limit_bytes=VMEM_LIMIT)


def _const_spec(shape):
    zeros = (0,) * len(shape)
    return pl.BlockSpec(shape, lambda *_: zeros, pipeline_mode=pl.Buffered(1))


def _layer_norm(x, g, b):
    mu = jnp.mean(x, axis=-1, keepdims=True)
    xc = x - mu
    var = jnp.mean(xc * xc, axis=-1, keepdims=True)
    return xc * lax.rsqrt(var + LN_EPS) * g + b


def _moe_post_norm(y1, y2, h1, info, g, b):
    return _layer_norm(ALPHA * h1 + info[:, 2:3] * y1 + info[:, 3:4] * y2, g, b)


def _inproj_kernel(*refs, col_chunk, n_gate_cols, from_moe):
    if from_moe:
        y1_ref, y2_ref, h1_ref, info_ref, g_ref, b_ref, w_ref, o_ref, h_ref = refs
        x = _moe_post_norm(y1_ref[...], y2_ref[...], h1_ref[...], info_ref[...], g_ref[...], b_ref[...])
    else:
        x_ref, g_ref, b_ref, w_ref, o_ref, h_ref = refs
        x = _layer_norm(x_ref[...], g_ref[...], b_ref[...])
    h_ref[...] = x
    xb = x.astype(BF16)
    n = w_ref.shape[1]
    for c in range(0, n, col_chunk):
        acc = jnp.dot(xb, w_ref[:, c:c + col_chunk], preferred_element_type=F32)
        if c < n_gate_cols:
            acc = 0.5 * jnp.tanh(0.5 * acc) + 0.5
        o_ref[:, c:c + col_chunk] = acc.astype(BF16)


def in_projection(source, w_bf16, n_gate_cols, tm=512, col_chunk=512):
    kind, *operands = source
    gain, bias = operands[-2:]
    T, D = operands[1].shape if kind == "moe" else operands[0].shape
    N = w_bf16.shape[1]
    nt = T // tm
    assert n_gate_cols % col_chunk == 0
    rows = pl.BlockSpec((tm, D), lambda i: (i, 0))
    if kind == "moe":
        ys, h1, info = operands[:3]
        args = [ys, ys, h1, info]
        in_specs = [rows, pl.BlockSpec((tm, D), lambda i: (nt + i, 0)), rows,
                    pl.BlockSpec((tm, LANES), lambda i: (i, 0))]
    else:
        args, in_specs = [operands[0]], [rows]
    return pl.pallas_call(
        functools.partial(_inproj_kernel, col_chunk=col_chunk, n_gate_cols=n_gate_cols, from_moe=kind == "moe"),
        grid=(nt,),
        in_specs=in_specs + [_const_spec((1, D)), _const_spec((1, D)), _const_spec((D, N))],
        out_specs=[pl.BlockSpec((tm, N), lambda i: (i, 0)), rows],
        out_shape=[jax.ShapeDtypeStruct((T, N), BF16), jax.ShapeDtypeStruct((T, D), F32)],
        compiler_params=_params("parallel"),
        name="in_proj",
    )(*args, gain.reshape(1, D), bias.reshape(1, D), w_bf16)


def _s5_kernel(u_ref, bblk_ref, lam_ref, lamk_ref, cblk_ref, d_ref, wglu_ref, bglu_ref,
               o_ref, io_ref, s_ref, carry_ref, *, n_state):
    tm, width = u_ref.shape
    K = tm // SUBLANES
    ncol = n_state // LANES
    nio = width // LANES

    @pl.when(pl.program_id(1) == 0)
    def _():
        carry_ref[...] = jnp.zeros_like(carry_ref)

    def sub_chunk_rows(k):
        return pl.ds(k, SUBLANES, stride=K)

    for c in range(nio):
        io_ref[c] = u_ref[:, c * LANES:(c + 1) * LANES].astype(F32)
    u = jnp.concatenate(
        [jnp.concatenate([io_ref[c, sub_chunk_rows(k), :] for k in range(K)], axis=0) for c in range(nio)],
        axis=1)
    s_ref[...] = jnp.dot(u.astype(BF16), bblk_ref[...], preferred_element_type=F32)

    def cols(j):
        return pl.ds(j * LANES, LANES), pl.ds(n_state + j * LANES, LANES)

    def rows(k):
        return pl.ds(pl.multiple_of(k * SUBLANES, SUBLANES), SUBLANES)

    a_re = [jnp.broadcast_to(lam_ref[0:1, pl.ds(j * LANES, LANES)], (SUBLANES, LANES)) for j in range(ncol)]
    a_im = [jnp.broadcast_to(lam_ref[1:2, pl.ds(j * LANES, LANES)], (SUBLANES, LANES)) for j in range(ncol)]

    def pass1(k, st):
        out = []
        for j in range(ncol):
            cr, ci = cols(j)
            sr, si = st[2 * j], st[2 * j + 1]
            nr = a_re[j] * sr - a_im[j] * si + s_ref[rows(k), cr]
            ni = a_re[j] * si + a_im[j] * sr + s_ref[rows(k), ci]
            s_ref[rows(k), cr] = nr
            s_ref[rows(k), ci] = ni
            out += [nr, ni]
        return tuple(out)

    zero = jnp.zeros((SUBLANES, LANES), F32)
    ends = lax.fori_loop(0, K, pass1, (zero,) * (2 * ncol))

    carries = []
    for j in range(ncol):
        cr, ci = cols(j)
        kr = lamk_ref[0:1, pl.ds(j * LANES, LANES)]
        ki = lamk_ref[1:2, pl.ds(j * LANES, LANES)]
        er, ei = ends[2 * j], ends[2 * j + 1]
        c_r = [carry_ref[0:1, cr]]
        c_i = [carry_ref[0:1, ci]]
        for r in range(1, SUBLANES + 1):
            pr, pi = c_r[-1], c_i[-1]
            c_r.append(kr * pr - ki * pi + er[r - 1:r, :])
            c_i.append(kr * pi + ki * pr + ei[r - 1:r, :])
        carry_ref[0:1, cr] = c_r[SUBLANES]
        carry_ref[0:1, ci] = c_i[SUBLANES]
        carries += [jnp.concatenate(c_r[:SUBLANES], axis=0), jnp.concatenate(c_i[:SUBLANES], axis=0)]

    def pass2(k, st):
        out = []
        for j in range(ncol):
            cr, ci = cols(j)
            dr, di = st[2 * j], st[2 * j + 1]
            nr = a_re[j] * dr - a_im[j] * di
            ni = a_re[j] * di + a_im[j] * dr
            s_ref[rows(k), cr] = s_ref[rows(k), cr] + nr
            s_ref[rows(k), ci] = s_ref[rows(k), ci] + ni
            out += [nr, ni]
        return tuple(out)

    lax.fori_loop(0, K, pass2, tuple(carries))

    y = jnp.dot(s_ref[...].astype(BF16), cblk_ref[...], preferred_element_type=F32)
    y = y + d_ref[...] * u
    y = jax.nn.gelu(y, approximate=True)
    gate = jnp.dot(y.astype(BF16), wglu_ref[...], preferred_element_type=F32) + bglu_ref[...]
    out = y * jax.nn.sigmoid(gate)
    for c in range(nio):
        for k in range(K):
            io_ref[c, sub_chunk_rows(k), :] = out[k * SUBLANES:(k + 1) * SUBLANES, c * LANES:(c + 1) * LANES]
    o_ref[...] = jnp.concatenate([io_ref[c] for c in range(nio)], axis=1).astype(BF16)


def s5_mixer(hp, u_col_block, batch, seq, bblk, lam, lamk, cblk, d_skip, w_glu, b_glu, tm=512):
    width = bblk.shape[0]
    n_state = bblk.shape[1] // 2
    nt = seq // tm
    return pl.pallas_call(
        functools.partial(_s5_kernel, n_state=n_state),
        grid=(batch, nt),
        in_specs=[
            pl.BlockSpec((tm, width), lambda b, t: (b * nt + t, u_col_block)),
            _const_spec(bblk.shape), _const_spec(lam.shape), _const_spec(lamk.shape),
            _const_spec(cblk.shape), _const_spec((1, width)), _const_spec(w_glu.shape),
            _const_spec((1, width)),
        ],
        out_specs=pl.BlockSpec((tm, width), lambda b, t: (b * nt + t, 0)),
        out_shape=jax.ShapeDtypeStruct((batch * seq, width), BF16),
        scratch_shapes=[pltpu.VMEM((width // LANES, tm, LANES), F32),
                        pltpu.VMEM((tm, 2 * n_state), F32),
                        pltpu.VMEM((SUBLANES, 2 * n_state), F32)],
        compiler_params=_params("parallel", "arbitrary"),
        name="s5_mixer",
    )(hp, bblk, lam, lamk, cblk, d_skip.reshape(1, width), w_glu, b_glu.reshape(1, width))


def s5_tables(lam_re, lam_im, log_dt, b_re, b_im, c_re, c_im, sub_chunk):
    G, P = lam_re.shape
    dt = jnp.exp(log_dt.astype(F32))[:, None]

    def lam_pow(n):
        mag = jnp.exp(lam_re * dt * n)
        return mag * jnp.cos(lam_im * dt * n), mag * jnp.sin(lam_im * dt * n)

    lr, li = lam_pow(1.0)
    kr, ki = lam_pow(float(sub_chunk))
    nr, ni = lr - 1.0, li
    den = lam_re * lam_re + lam_im * lam_im
    zr = (nr * lam_re + ni * lam_im) / den
    zi = (ni * lam_re - nr * lam_im) / den
    bbr = b_re * zr[..., None] - b_im * zi[..., None]
    bbi = b_re * zi[..., None] + b_im * zr[..., None]
    eye = jnp.eye(G, dtype=F32)
    Cg = b_re.shape[-1]

    def in_blk(m):
        return jnp.einsum('gpc,gh->gchp', m, eye).reshape(G * Cg, G * P)

    def out_blk(m):
        return jnp.einsum('gcp,gh->gphc', m, eye).reshape(G * P, G * Cg)

    bblk = jnp.concatenate([in_blk(bbr), in_blk(bbi)], axis=1).astype(BF16)
    cblk = jnp.concatenate([out_blk(c_re), -out_blk(c_im)], axis=0).astype(BF16)
    lam = jnp.stack([lr.reshape(-1), li.reshape(-1)])
    lamk = jnp.stack([kr.reshape(-1), ki.reshape(-1)])
    return bblk, lam, lamk, cblk


def _log_sigmoid_pair(z):
    soft = jnp.log(1.0 + jnp.exp(-jnp.abs(z)))
    lp = jnp.minimum(z, 0.0) - soft
    return lp, lp - z


SB_EXIT = -110.0


def _sb_kernel(q_ref, k_ref, v_ref, tri_ref, o_ref, acc_ref, carry_ref, *, blk, group, nsub):
    i = pl.program_id(2)
    lane = lax.broadcasted_iota(jnp.int32, (1, LANES), 1)
    head_lanes = [lane < SB_HEAD_DIM, lane >= SB_HEAD_DIM]
    zero = jnp.zeros((), BF16)
    qh = []
    for s in range(nsub):
        q = q_ref[s * blk:(s + 1) * blk, :]
        qh.append([jnp.where(m, q, zero) for m in head_lanes])
    row = lax.broadcasted_iota(jnp.int32, (blk, blk), 0)
    col = lax.broadcasted_iota(jnp.int32, (blk, blk), 1)
    strictly_earlier = col < row

    acc_ref[...] = jnp.zeros_like(acc_ref)
    carry_ref[...] = jnp.zeros_like(carry_ref)

    def group_step(g, first):
        ks, vs = {}, {}
        for d in range(1 - group, nsub):
            j = i * nsub - g * group + d
            start = pl.multiple_of(jnp.maximum(j, 0) * blk, blk)
            ks[d] = k_ref[pl.ds(start, blk), :]
            vj = v_ref[pl.ds(start, blk), :]
            vs[d] = [jnp.where(m & (j >= 0), vj, zero) for m in head_lanes]
        chains = [(s, u, h) for s in range(nsub) for u in range(group) for h in range(2)]
        zs = {c: lax.dot_general(qh[c[0]][c[2]], ks[c[0] - c[1]], (((1,), (1,)), ((), ())),
                                 preferred_element_type=F32) for c in chains}
        lps, rs = {}, {}
        for c in chains:
            lp, l1m = _log_sigmoid_pair(zs[c])
            if first and c[1] == 0:
                l1m = jnp.where(strictly_earlier, l1m, 0.0)
            lps[c] = lp
            rs[c] = jnp.dot(l1m.astype(BF16), tri_ref[...], preferred_element_type=F32)
        ws = {}
        top = None
        for s in range(nsub):
            for h in range(2):
                cum = carry_ref[s, h]
                for u in range(group):
                    c = (s, u, h)
                    w = jnp.exp(lps[c] + rs[c][:, :blk] + cum)
                    if first and u == 0:
                        w = jnp.where(strictly_earlier, w, 0.0)
                    ws[c] = w.astype(BF16)
                    cum = cum + rs[c][:, blk:]
                carry_ref[s, h] = cum
                top = cum if top is None else jnp.maximum(top, cum)
        for s in range(nsub):
            acc = acc_ref[s]
            for u in range(group):
                for h in range(2):
                    acc = acc + jnp.dot(ws[(s, u, h)], vs[s - u][h], preferred_element_type=F32)
            acc_ref[s] = acc
        return jnp.max(top)

    def more(state):
        g, top = state
        return (g * group <= i * nsub + nsub - 1) & (top > SB_EXIT)

    def step(state):
        g, _ = state
        return g + 1, group_step(g, False)

    lax.while_loop(more, step, (1, group_step(0, True)))
    for s in range(nsub):
        o_ref[s * blk:(s + 1) * blk, :] = acc_ref[s].astype(BF16)


def stick_breaking(hp, q_blk0, k_blk0, v_blk0, batch, seq, blk=128, group=3, nsub=8):
    assert blk == LANES
    tile = nsub * blk
    nq = seq // tile
    n_pairs = SB_HEADS * SB_HEAD_DIM // LANES
    r = np.arange(blk)
    tri = np.concatenate([(r[:, None] > r[None, :]), np.ones((blk, blk), bool)], axis=1)
    tri = jnp.asarray(tri, BF16)
    return pl.pallas_call(
        functools.partial(_sb_kernel, blk=blk, group=group, nsub=nsub),
        grid=(batch, n_pairs, nq),
        in_specs=[
            pl.BlockSpec((tile, LANES), lambda b, p, i: (b * nq + i, q_blk0 + p)),
            pl.BlockSpec((seq, LANES), lambda b, p, i: (b, k_blk0 + p)),
            pl.BlockSpec((seq, LANES), lambda b, p, i: (b, v_blk0 + p)),
            _const_spec(tri.shape),
        ],
        out_specs=pl.BlockSpec((tile, LANES), lambda b, p, i: (b * nq + i, p)),
        out_shape=jax.ShapeDtypeStruct((batch * seq, n_pairs * LANES), BF16),
        scratch_shapes=[pltpu.VMEM((nsub, blk, LANES), F32), pltpu.VMEM((nsub, 2, blk, blk), F32)],
        compiler_params=_params("parallel", "parallel", "arbitrary"),
        name="stick_breaking",
    )(hp, hp, hp, tri)


def _ret_kernel(qk_ref, v_ref, g_ref, cos_ref, sin_ref, mask_ref, rdec_ref, cdec_ref, tdec_ref, gn_ref,
                o_ref, state_ref):
    @pl.when(pl.program_id(1) == 0)
    def _():
        state_ref[...] = jnp.zeros_like(state_ref)

    half = RET_HEADS * RET_DK // 2
    qk = qk_ref[...].astype(F32)
    cos, sin = cos_ref[...], sin_ref[...]

    def rope(t):
        t1, t2 = t[:, :half], t[:, half:]
        return jnp.concatenate([t1 * cos - t2 * sin, t1 * sin + t2 * cos], axis=1)

    qr = rope(qk[:, :2 * half])
    kr = rope(qk[:, 2 * half:])
    kb = kr.astype(BF16)
    kt = kr.T
    lane = lax.broadcasted_iota(jnp.int32, (1, 2 * half), 1)
    head_of_lane = (lane % half) // (RET_DK // 2)
    for h in range(RET_HEADS):
        qh = jnp.where(head_of_lane == h, qr, 0.0)
        vh = v_ref[:, h * RET_DV:(h + 1) * RET_DV]
        scores = lax.dot_general(qh.astype(BF16), kb, (((1,), (1,)), ((), ())), preferred_element_type=F32)
        scores = scores * mask_ref[h]
        o = jnp.dot(scores.astype(BF16), vh, preferred_element_type=F32)
        o = o + jnp.dot((qh * rdec_ref[h]).astype(BF16), state_ref[h].astype(BF16), preferred_element_type=F32)
        kv = jnp.dot((kt * cdec_ref[h]).astype(BF16), vh, preferred_element_type=F32)
        state_ref[h] = tdec_ref[h] * state_ref[h] + kv
        mu = jnp.mean(o, axis=-1, keepdims=True)
        oc = o - mu
        var = jnp.mean(oc * oc, axis=-1, keepdims=True)
        on = oc * lax.rsqrt(var + LN_EPS) * gn_ref[:, h * RET_DV:(h + 1) * RET_DV]
        g = g_ref[:, h * RET_DV:(h + 1) * RET_DV].astype(F32)
        o_ref[:, h * RET_DV:(h + 1) * RET_DV] = (g * jax.nn.sigmoid(g) * on).astype(BF16)


def retention_tables(seq, tile):
    halfdim = RET_DK // 2
    inv_freq = ROPE_BASE ** (-jnp.arange(halfdim, dtype=F32) / halfdim)
    ang = jnp.arange(seq, dtype=F32)[:, None] * inv_freq[None, :]
    cos = jnp.tile(jnp.cos(ang), (1, RET_HEADS))
    sin = jnp.tile(jnp.sin(ang), (1, RET_HEADS))
    log_gamma = jnp.log(1.0 - 2.0 ** (-5.0 - jnp.arange(RET_HEADS, dtype=F32)))
    t = jnp.arange(tile, dtype=F32)
    same_or_earlier_chunk = (t[None, :] // CHUNK) <= (t[:, None] // CHUNK)
    mask = jnp.exp(log_gamma[:, None, None] * jnp.abs(t[:, None] - t[None, :]))
    mask = jnp.where(same_or_earlier_chunk[None], mask, 0.0)
    rdec = jnp.exp(log_gamma[:, None, None] * (t[None, :, None] + 1.0))
    cdec = jnp.exp(log_gamma[:, None, None] * (tile - 1.0 - t[None, None, :]))
    tdec = jnp.broadcast_to(jnp.exp(log_gamma * tile)[:, None, None], (RET_HEADS, 1, RET_DV))
    return cos, sin, mask, rdec, cdec, tdec


def retention(hp, qk_blk, v_blk, g_blk, batch, seq, gn_g, tile=512):
    cos, sin, mask, rdec, cdec, tdec = retention_tables(seq, tile)
    nt = seq // tile
    qkw = 2 * RET_HEADS * RET_DK
    vw = RET_HEADS * RET_DV
    half = RET_HEADS * RET_DK // 2
    return pl.pallas_call(
        _ret_kernel,
        grid=(batch, nt),
        in_specs=[
            pl.BlockSpec((tile, qkw), lambda b, t: (b * nt + t, qk_blk)),
            pl.BlockSpec((tile, vw), lambda b, t: (b * nt + t, v_blk)),
            pl.BlockSpec((tile, vw), lambda b, t: (b * nt + t, g_blk)),
            pl.BlockSpec((tile, half), lambda b, t: (t, 0)),
            pl.BlockSpec((tile, half), lambda b, t: (t, 0)),
            _const_spec(mask.shape), _const_spec(rdec.shape), _const_spec(cdec.shape),
            _const_spec(tdec.shape), _const_spec((1, vw)),
        ],
        out_specs=pl.BlockSpec((tile, vw), lambda b, t: (b * nt + t, 0)),
        out_shape=jax.ShapeDtypeStruct((batch * seq, vw), BF16),
        scratch_shapes=[pltpu.VMEM((RET_HEADS, 2 * half, RET_DV), F32)],
        compiler_params=_params("parallel", "arbitrary"),
        name="retention",
    )(hp, hp, hp, cos, sin, mask, rdec, cdec, tdec, gn_g.reshape(1, vw))


def _first_max(vals, lane):
    m = jnp.max(vals, axis=-1, keepdims=True)
    idx = jnp.min(jnp.where(vals == m, lane, float(LANES)), axis=-1, keepdims=True)
    return m, idx


def _merge_kernel(ga_ref, gb_ref, gc_ref, ya_ref, yb_ref, yc_ref, h_ref, wa_ref, wb_ref, wc_ref, wo_ref,
                  g1_ref, b1_ref, rw_ref, rb_ref, h1_ref, info_ref, infot_ref, cnt_ref, run_ref, merged_ref,
                  hprev_ref, *, col_chunk):
    step = pl.program_id(0)

    @pl.when(step == 0)
    def _():
        run_ref[...] = jnp.zeros_like(run_ref)
        hprev_ref[...] = jnp.zeros_like(hprev_ref)

    _route(hprev_ref[...], rw_ref, rb_ref, info_ref, infot_ref, cnt_ref, run_ref, count=step > 0)

    d_model = h_ref.shape[1]
    for c in range(0, d_model, col_chunk):
        cs = slice(c, c + col_chunk)
        acc = ga_ref[:, cs].astype(F32) * jnp.dot(ya_ref[...], wa_ref[:, cs], preferred_element_type=F32)
        acc = acc + gb_ref[:, cs].astype(F32) * jnp.dot(yb_ref[...], wb_ref[:, cs], preferred_element_type=F32)
        acc = acc + gc_ref[:, cs].astype(F32) * jnp.dot(yc_ref[...], wc_ref[:, cs], preferred_element_type=F32)
        merged_ref[:, cs] = acc.astype(BF16)
    for c in range(0, d_model, col_chunk):
        cs = slice(c, c + col_chunk)
        h1_ref[:, cs] = ALPHA * h_ref[:, cs] + jnp.dot(merged_ref[...], wo_ref[:, cs], preferred_element_type=F32)
    h1 = _layer_norm(h1_ref[...], g1_ref[...], b1_ref[...])
    h1_ref[...] = h1
    hprev_ref[...] = h1


def _route(h1, rw_ref, rb_ref, info_ref, infot_ref, cnt_ref, run_ref, count):
    h_hi = h1.astype(BF16)
    h_lo = (h1 - h_hi.astype(F32)).astype(BF16)
    logits = (jnp.dot(h_hi, rw_ref[0], preferred_element_type=F32)
              + jnp.dot(h_lo, rw_ref[0], preferred_element_type=F32)
              + jnp.dot(h_hi, rw_ref[1], preferred_element_type=F32)) + rb_ref[...]
    tm = logits.shape[0]
    lane_i = lax.broadcasted_iota(jnp.int32, (tm, LANES), 1)
    lane = lane_i.astype(F32)
    ex = jnp.exp(logits - jnp.max(logits, axis=-1, keepdims=True))
    scores = ex / jnp.sum(ex, axis=-1, keepdims=True)
    group = (lane_i // EXPERTS_PER_GROUP).astype(F32)
    best = jnp.zeros((tm, 1), F32)
    best_score = jnp.full((tm, 1), -1.0, F32)
    for gi in range(N_EXPERTS // EXPERTS_PER_GROUP):
        sg = jnp.where(group == float(gi), scores, -1.0)
        m1, i1 = _first_max(sg, lane)
        m2 = jnp.max(jnp.where(lane == i1, -1.0, sg), axis=-1, keepdims=True)
        gs = m1 + m2
        better = gs > best_score
        best = jnp.where(better, float(gi), best)
        best_score = jnp.where(better, gs, best_score)
    masked = jnp.where(group == best, scores, -1.0)
    w1, e1 = _first_max(masked, lane)
    w2, e2 = _first_max(jnp.where(lane == e1, -2.0, masked), lane)
    den = w1 + w2
    gate1, gate2 = w1 / den, w2 / den

    onehot = jnp.where(((lane == e1) | (lane == e2)) & count, 1.0, 0.0)
    run_ref[0:1, :] = run_ref[0:1, :] + jnp.sum(onehot, axis=0, keepdims=True)
    cnt_ref[...] = run_ref[...]

    info = jnp.zeros((tm, LANES), F32)
    for k, val in enumerate((e1, e2, gate1, gate2)):
        info = jnp.where(lane_i == k, val, info)
    info_ref[...] = info
    infot_ref[...] = info.T[:SUBLANES]


def merge_norm_route(hp, ya, yb, yc, h, wa, wb, wc, wo, ln_g, ln_b, rw, rb, tm=512, col_chunk=512):
    T, D = h.shape
    nt = T // tm

    def tile(col):
        return lambda i: (jnp.minimum(i, nt - 1), col)

    routed = lambda i: jnp.maximum(i - 1, 0)
    return pl.pallas_call(
        functools.partial(_merge_kernel, col_chunk=col_chunk),
        grid=(nt + 1,),
        in_specs=[
            pl.BlockSpec((tm, D), tile(0)), pl.BlockSpec((tm, D), tile(1)), pl.BlockSpec((tm, D), tile(2)),
            pl.BlockSpec((tm, ya.shape[1]), tile(0)), pl.BlockSpec((tm, yb.shape[1]), tile(0)),
            pl.BlockSpec((tm, yc.shape[1]), tile(0)), pl.BlockSpec((tm, D), tile(0)),
            _const_spec(wa.shape), _const_spec(wb.shape), _const_spec(wc.shape), _const_spec(wo.shape),
            _const_spec((1, D)), _const_spec((1, D)), _const_spec(rw.shape), _const_spec(rb.shape),
        ],
        out_specs=[pl.BlockSpec((tm, D), tile(0)), pl.BlockSpec((tm, LANES), lambda i: (routed(i), 0)),
                   pl.BlockSpec((SUBLANES, tm), lambda i: (0, routed(i))),
                   pl.BlockSpec((SUBLANES, LANES), lambda i: (0, 0))],
        out_shape=[jax.ShapeDtypeStruct((T, D), F32), jax.ShapeDtypeStruct((T, LANES), F32),
                   jax.ShapeDtypeStruct((SUBLANES, T), F32), jax.ShapeDtypeStruct((SUBLANES, LANES), F32)],
        scratch_shapes=[pltpu.VMEM((SUBLANES, LANES), F32), pltpu.VMEM((tm, D), BF16), pltpu.VMEM((tm, D), F32)],
        compiler_params=_params("arbitrary"),
        name="merge_norm_route",
    )(hp, hp, hp, ya, yb, yc, h, wa, wb, wc, wo, ln_g.reshape(1, D), ln_b.reshape(1, D), rw, rb)


ROW_UNROLL = 8


def _expert_kernel(be_ref, nused_ref, base_ref, rows_ref, h_ref, w1_ref, w3_ref, w2_ref, ys_ref,
                   xa, xb, ya, yb, w1b, w3b, w2b, gsem, ssem, *, n_tok):
    j = pl.program_id(0)
    n_used = nused_ref[0]
    blk = xa.shape[0]
    n_blocks = base_ref.shape[0] - 1
    tok_mask = n_tok - 1
    assert n_tok & tok_mask == 0

    def gather_copy(base, r, x_dst, sem):
        tok = rows_ref[base + r] & tok_mask
        return pltpu.make_async_copy(h_ref.at[pl.ds(tok, 1)], x_dst.at[pl.ds(r, 1)], sem)

    def scatter_copy(base, r, y_src, sem):
        return pltpu.make_async_copy(y_src.at[pl.ds(r, 1)], ys_ref.at[pl.ds(rows_ref[base + r], 1)], sem)

    def wait_gathered(x_dst, sem):
        pltpu.make_async_copy(h_ref.at[pl.ds(0, blk)], x_dst, sem).wait()

    def wait_scattered(y_src, sem):
        pltpu.make_async_copy(y_src, ys_ref.at[pl.ds(0, blk)], sem).wait()

    def rolled(start_row_copy):
        def body(c, carry):
            for k in range(ROW_UNROLL):
                start_row_copy(c * ROW_UNROLL + k)
            return carry
        lax.fori_loop(0, blk // ROW_UNROLL, body, 0)

    def block_step(b, x_own, x_other, y_own, y_other, g_own, g_other, s_own, s_other, leads_pair):
        @pl.when(b < n_used)
        def _():
            if leads_pair:
                @pl.when(b == 0)
                def _():
                    rolled(lambda r: gather_copy(base_ref[0], r, x_own, g_own).start())
                    y_other[...] = jnp.zeros_like(y_other)

            wait_gathered(x_own, g_own)

            @pl.when(b >= 1)
            def _():
                wait_scattered(y_own, s_own)

            if leads_pair:
                @pl.when((b == 0) | (be_ref[b] != be_ref[jnp.maximum(b - 2, 0)]))
                def _():
                    w1b[...] = w1_ref[0, 0].astype(BF16)
                    w3b[...] = w3_ref[0, 0].astype(BF16)
                    w2b[...] = w2_ref[0, 0].astype(BF16)

            base_next = base_ref[jnp.minimum(b + 1, n_blocks)]
            base_prev = base_ref[jnp.where(b >= 1, b - 1, n_blocks)]
            for r in range(blk):
                gather_copy(base_next, r, x_other, g_other).start()
                scatter_copy(base_prev, r, y_other, s_other).start()
            x = x_own[...].astype(BF16)
            a = jnp.dot(x, w1b[...], preferred_element_type=F32)
            g = jnp.dot(x, w3b[...], preferred_element_type=F32)
            hb = (a * jax.nn.sigmoid(a) * g).astype(BF16)
            y_own[...] = jnp.dot(hb, w2b[...], preferred_element_type=F32)

            @pl.when(b == n_used - 1)
            def _():
                wait_scattered(y_other, s_other)
                rolled(lambda r: scatter_copy(base_ref[b], r, y_own, s_own).start())
                wait_scattered(y_own, s_own)
                wait_gathered(x_other, g_other)

    block_step(2 * j, xa, xb, ya, yb, gsem.at[0], gsem.at[1], ssem.at[0], ssem.at[1], True)
    block_step(2 * j + 1, xb, xa, yb, ya, gsem.at[1], gsem.at[0], ssem.at[1], ssem.at[0], False)


def moe_experts(h1, rows, base, block_e, n_used, w1, w3, w2, layer, blk):
    T, D = h1.shape
    Fd = w1.shape[-1]
    nb = block_e.shape[0]
    assert nb % 2 == 0
    wspec = lambda s: pl.BlockSpec((1, 1) + s, lambda j, be, *_: (layer, be[2 * j], 0, 0))
    return pl.pallas_call(
        functools.partial(_expert_kernel, n_tok=T),
        grid_spec=pltpu.PrefetchScalarGridSpec(
            num_scalar_prefetch=4,
            grid=(nb // 2,),
            in_specs=[pl.BlockSpec(memory_space=pl.ANY), wspec((D, Fd)), wspec((D, Fd)), wspec((Fd, D))],
            out_specs=pl.BlockSpec(memory_space=pl.ANY),
            scratch_shapes=[pltpu.VMEM((blk, D), F32), pltpu.VMEM((blk, D), F32),
                            pltpu.VMEM((blk, D), F32), pltpu.VMEM((blk, D), F32),
                            pltpu.VMEM((D, Fd), BF16), pltpu.VMEM((D, Fd), BF16), pltpu.VMEM((Fd, D), BF16),
                            pltpu.SemaphoreType.DMA((2,)), pltpu.SemaphoreType.DMA((2,))],
        ),
        out_shape=jax.ShapeDtypeStruct((2 * T + blk, D), F32),
        compiler_params=pltpu.CompilerParams(dimension_semantics=("arbitrary",), vmem_limit_bytes=VMEM_LIMIT,
                                             has_side_effects=True),
        name="moe_experts",
    )(block_e, n_used, base, rows, h1, w1, w3, w2)


def _combine_kernel(y1_ref, y2_ref, h_ref, info_ref, g_ref, b_ref, o_ref):
    o_ref[...] = _moe_post_norm(y1_ref[...], y2_ref[...], h_ref[...], info_ref[...], g_ref[...], b_ref[...])


def moe_combine(ys, h, info, ln_g, ln_b, tm=512):
    T, D = h.shape
    nt = T // tm
    return pl.pallas_call(
        _combine_kernel,
        grid=(nt,),
        in_specs=[pl.BlockSpec((tm, D), lambda i: (i, 0)),
                  pl.BlockSpec((tm, D), lambda i: (nt + i, 0)),
                  pl.BlockSpec((tm, D), lambda i: (i, 0)),
                  pl.BlockSpec((tm, LANES), lambda i: (i, 0)),
                  _const_spec((1, D)), _const_spec((1, D))],
        out_specs=pl.BlockSpec((tm, D), lambda i: (i, 0)),
        out_shape=jax.ShapeDtypeStruct((T, D), F32),
        compiler_params=_params("parallel"),
        name="moe_combine",
    )(ys, ys, h, info, ln_g.reshape(1, D), ln_b.reshape(1, D))


def route_tables(info_t, counts, blk, n_blocks):
    n_assign = info_t.shape[1] * 2
    pair = 2 * blk
    experts = info_t[0:2].reshape(-1).astype(jnp.int32)
    pad_experts = jnp.repeat(jnp.arange(N_EXPERTS, dtype=jnp.int32), pair)
    keys = jnp.concatenate([2 * experts, 2 * pad_experts + 1])
    index_bits = int(keys.shape[0] - 1).bit_length()
    packed = jnp.sort((keys << index_bits) | jnp.arange(keys.shape[0], dtype=jnp.int32))
    order = packed & ((1 << index_bits) - 1)
    rows = jnp.where(order < n_assign, order, n_assign + (order - n_assign) % blk)
    cnt = counts[0, :N_EXPERTS].astype(jnp.int32)
    padded = (cnt + pair - 1) // pair * pair
    pends = jnp.cumsum(padded)
    block_row = jnp.arange(n_blocks, dtype=jnp.int32) * blk
    block_e = jnp.minimum(jnp.sum(block_row[:, None] >= pends[None, :], axis=1), N_EXPERTS - 1).astype(jnp.int32)
    of_block = block_e[:, None] == jnp.arange(N_EXPERTS, dtype=jnp.int32)[None, :]

    def per_block(table):
        return jnp.sum(jnp.where(of_block, table[None, :], 0), axis=1)

    segment_start = jnp.cumsum(cnt + pair) - (cnt + pair)
    base = per_block(segment_start) + block_row - per_block(pends - padded)
    all_padding = rows.shape[0] - blk
    base = jnp.where(block_row < pends[-1], base, all_padding)
    base = jnp.concatenate([base, jnp.full((1,), all_padding, jnp.int32)]).astype(jnp.int32)
    n_used = (pends[-1:] // blk).astype(jnp.int32)
    return rows, base, block_e, n_used


def grouped_moe(h1, info_t, counts, w1, w3, w2, layer, blk=512):
    T, D = h1.shape
    n_blocks = (2 * T) // blk + 2 * N_EXPERTS
    rows, base, block_e, n_used = route_tables(info_t, counts, blk, n_blocks)
    return moe_experts(h1, rows, base, block_e, n_used, w1, w3, w2, layer, blk)


def _in_proj_layout(d_model):
    s5w = d_model // 4
    sbw = SB_HEADS * SB_HEAD_DIM
    rqk = RET_HEADS * RET_DK
    rv = RET_HEADS * RET_DV
    sizes = (s5w, sbw, sbw, sbw, rqk, rqk, rv, rv, d_model, d_model, d_model)
    off = np.concatenate([[0], np.cumsum(sizes)]).tolist()

    def permute(w):
        u_a, q_b, k_b, v_b, q_c, k_c, v_c, g_c, ga, gb, gc = [w[:, off[i]:off[i + 1]] for i in range(11)]

        def rope_perm(m):
            m = m.reshape(d_model, RET_HEADS, 2, RET_DK // 2)
            return jnp.swapaxes(m, 1, 2).reshape(d_model, rqk)

        parts = [ga, gb, gc, rope_perm(q_c), rope_perm(k_c) * RET_DK ** -0.5, v_c, g_c,
                 q_b * SB_HEAD_DIM ** -0.5, k_b, v_b, u_a]
        return jnp.concatenate([m.astype(BF16) for m in parts], axis=1)

    o_qk = 3 * d_model
    o_v = o_qk + 2 * rqk
    o_g = o_v + rv
    o_sb = o_g + rv
    o_u = o_sb + 3 * sbw
    blocks = dict(qk=o_qk // (2 * rqk), v=o_v // rv, g=o_g // rv, sb_q=o_sb // LANES,
                  sb_k=(o_sb + sbw) // LANES, sb_v=(o_sb + 2 * sbw) // LANES, u=o_u // s5w)
    assert o_qk % (2 * rqk) == 0 and o_v % rv == 0 and o_g % rv == 0 and o_sb % LANES == 0 and o_u % s5w == 0
    return permute, blocks


def _layer(source, batch, seq, p):
    D = p['w_in'].shape[0]
    permute, blk = _in_proj_layout(D)
    hp, h = in_projection(source, permute(p['w_in']), n_gate_cols=3 * D)
    s5_tile = 512
    bblk, lam, lamk, cblk = s5_tables(p['lam_re'], p['lam_im'], p['log_dt'], p['b_re'], p['b_im'],
                                      p['c_re'], p['c_im'], s5_tile // SUBLANES)
    ya = s5_mixer(hp, blk['u'], batch, seq, bblk, lam, lamk, cblk, p['d_skip'], p['w_glu'].astype(BF16),
                  p['b_glu'], tm=s5_tile)
    yb = stick_breaking(hp, blk['sb_q'], blk['sb_k'], blk['sb_v'], batch, seq)
    yc = retention(hp, blk['qk'], blk['v'], blk['g'], batch, seq, p['gn_g'])
    rw = jnp.zeros((D, LANES), F32).at[:, :N_EXPERTS].set(p['router_w'])
    rw_hi = rw.astype(BF16)
    rw = jnp.stack([rw_hi, (rw - rw_hi.astype(F32)).astype(BF16)])
    rb = jnp.full((1, LANES), NEG_BIG, F32).at[0, :N_EXPERTS].set(p['router_b'])
    h1, info, info_t, counts = merge_norm_route(
        hp, ya, yb, yc, h, p['w_up_a'].astype(BF16), p['w_up_b'].astype(BF16), p['w_up_c'].astype(BF16),
        p['w_out'].astype(BF16), p['ln1_g'], p['ln1_b'], rw, rb)
    ys = grouped_moe(h1, info_t, counts, p['moe_w1'], p['moe_w3'], p['moe_w2'], p['layer'])
    return ("moe", ys, h1, info, p['ln2_g'], p['ln2_b'])


def kernel(x, ln0_g, ln0_b, w_in, s5_lambda_re, s5_lambda_im, s5_log_dt, s5_b_re, s5_b_im, s5_c_re, s5_c_im,
           s5_d, s5_w_glu, s5_b_glu, ret_gn_g, w_up_a, w_up_b, w_up_c, w_out, ln1_g, ln1_b, router_w, router_b,
           moe_w1, moe_w3, moe_w2, ln2_g, ln2_b):
    batch, seq, D = x.shape
    source = ("raw", x.reshape(batch * seq, D), ln0_g, ln0_b)
    for l in range(w_in.shape[0]):
        p = dict(w_in=w_in[l], lam_re=s5_lambda_re[l], lam_im=s5_lambda_im[l], log_dt=s5_log_dt[l],
                 b_re=s5_b_re[l], b_im=s5_b_im[l], c_re=s5_c_re[l], c_im=s5_c_im[l], d_skip=s5_d[l],
                 w_glu=s5_w_glu[l], b_glu=s5_b_glu[l], gn_g=ret_gn_g[l], w_up_a=w_up_a[l], w_up_b=w_up_b[l],
                 w_up_c=w_up_c[l], w_out=w_out[l], ln1_g=ln1_g[l], ln1_b=ln1_b[l], router_w=router_w,
                 router_b=router_b, moe_w1=moe_w1, moe_w3=moe_w3, moe_w2=moe_w2, layer=l, ln2_g=ln2_g[l],
                 ln2_b=ln2_b[l])
        source = _layer(source, batch, seq, p)
    return moe_combine(*source[1:]).reshape(batch, seq, D)
```

```python
import functools
import math

import numpy as np
import jax
import jax.numpy as jnp
from jax import lax
from jax.experimental import pallas as pl
from jax.experimental.pallas import tpu as pltpu

F32 = jnp.float32
BF16 = jnp.bfloat16

LANES = 128
SUBLANES = 8
VMEM_LIMIT = 56 * 1024 * 1024

DEPTH = 2
CHUNK = 64
S5_GROUP_CH = 16
S5_STATE = 64
SB_HEADS = 4
SB_HEAD_DIM = 64
RET_HEADS = 4
RET_DK = 64
RET_DV = 128
ROPE_BASE = 10000.0
N_EXPERTS = 16
EXPERTS_PER_GROUP = 4
ALPHA = (2 * DEPTH) ** 0.25
LN_EPS = 1e-5

GATE_OFF = 0
NEG_BIG = -1e30


def _params(*sem):
    return pltpu.CompilerParams(dimension_semantics=sem, vmem_limit_bytes=VMEM_LIMIT)


def _const_spec(shape):
    zeros = (0,) * len(shape)
    return pl.BlockSpec(shape, lambda *_: zeros, pipeline_mode=pl.Buffered(1))


def _layer_norm(x, g, b):
    mu = jnp.mean(x, axis=-1, keepdims=True)
    xc = x - mu
    var = jnp.mean(xc * xc, axis=-1, keepdims=True)
    return xc * lax.rsqrt(var + LN_EPS) * g + b


def _moe_post_norm(y1, y2, h1, info, g, b):
    return _layer_norm(ALPHA * h1 + info[:, 2:3] * y1 + info[:, 3:4] * y2, g, b)


def _inproj_kernel(*refs, col_chunk, n_gate_cols, from_moe):
    *ins, w_ref, o_ref, h_ref, xb_ref = refs

    @pl.when(pl.program_id(0) == 0)
    def _():
        xb_ref[...] = jnp.zeros_like(xb_ref)

    n = w_ref.shape[1]
    for c in range(0, n, col_chunk):
        acc = jnp.dot(xb_ref[...], w_ref[:, c:c + col_chunk], preferred_element_type=F32)
        if c < n_gate_cols:
            acc = 0.5 * jnp.tanh(0.5 * acc) + 0.5
        o_ref[:, c:c + col_chunk] = acc.astype(BF16)

    if from_moe:
        y1_ref, y2_ref, h1_ref, info_ref, g_ref, b_ref = ins
        x = _moe_post_norm(y1_ref[...], y2_ref[...], h1_ref[...], info_ref[...], g_ref[...], b_ref[...])
    else:
        x_ref, g_ref, b_ref = ins
        x = _layer_norm(x_ref[...], g_ref[...], b_ref[...])
    h_ref[...] = x
    xb_ref[...] = x.astype(BF16)


def in_projection(source, w_bf16, n_gate_cols, tm=512, col_chunk=512):
    kind, *operands = source
    gain, bias = operands[-2:]
    T, D = operands[1].shape if kind == "moe" else operands[0].shape
    N = w_bf16.shape[1]
    nt = T // tm
    assert n_gate_cols % col_chunk == 0
    formed = lambda i: jnp.minimum(i, nt - 1)
    rows = pl.BlockSpec((tm, D), lambda i: (formed(i), 0))
    if kind == "moe":
        ys, h1, info = operands[:3]
        args = [ys, ys, h1, info]
        in_specs = [rows, pl.BlockSpec((tm, D), lambda i: (nt + formed(i), 0)), rows,
                    pl.BlockSpec((tm, LANES), lambda i: (formed(i), 0))]
    else:
        args, in_specs = [operands[0]], [rows]
    return pl.pallas_call(
        functools.partial(_inproj_kernel, col_chunk=col_chunk, n_gate_cols=n_gate_cols, from_moe=kind == "moe"),
        grid=(nt + 1,),
        in_specs=in_specs + [_const_spec((1, D)), _const_spec((1, D)), _const_spec((D, N))],
        out_specs=[pl.BlockSpec((tm, N), lambda i: (jnp.maximum(i - 1, 0), 0)), rows],
        out_shape=[jax.ShapeDtypeStruct((T, N), BF16), jax.ShapeDtypeStruct((T, D), F32)],
        scratch_shapes=[pltpu.VMEM((tm, D), BF16)],
        compiler_params=_params("arbitrary"),
        name="in_proj",
    )(*args, gain.reshape(1, D), bias.reshape(1, D), w_bf16)


def _s5_kernel(u_ref, bblk_ref, lam_ref, lamk_ref, cblk_ref, d_ref, wglu_ref, bglu_ref,
               o_ref, io_ref, s_ref, carry_ref, *, n_state):
    tm, width = u_ref.shape
    K = tm // SUBLANES
    ncol = n_state // LANES
    nio = width // LANES

    @pl.when(pl.program_id(1) == 0)
    def _():
        carry_ref[...] = jnp.zeros_like(carry_ref)

    def sub_chunk_rows(k):
        return pl.ds(k, SUBLANES, stride=K)

    for c in range(nio):
        io_ref[c] = u_ref[:, c * LANES:(c + 1) * LANES].astype(F32)
    u = jnp.concatenate(
        [jnp.concatenate([io_ref[c, sub_chunk_rows(k), :] for k in range(K)], axis=0) for c in range(nio)],
        axis=1)
    s_ref[...] = jnp.dot(u.astype(BF16), bblk_ref[...], preferred_element_type=F32)

    def cols(j):
        return pl.ds(j * LANES, LANES), pl.ds(n_state + j * LANES, LANES)

    def rows(k):
        return pl.ds(pl.multiple_of(k * SUBLANES, SUBLANES), SUBLANES)

    a_re = [jnp.broadcast_to(lam_ref[0:1, pl.ds(j * LANES, LANES)], (SUBLANES, LANES)) for j in range(ncol)]
    a_im = [jnp.broadcast_to(lam_ref[1:2, pl.ds(j * LANES, LANES)], (SUBLANES, LANES)) for j in range(ncol)]

    def pass1(k, st):
        out = []
        for j in range(ncol):
            cr, ci = cols(j)
            sr, si = st[2 * j], st[2 * j + 1]
            nr = a_re[j] * sr - a_im[j] * si + s_ref[rows(k), cr]
            ni = a_re[j] * si + a_im[j] * sr + s_ref[rows(k), ci]
            s_ref[rows(k), cr] = nr
            s_ref[rows(k), ci] = ni
            out += [nr, ni]
        return tuple(out)

    zero = jnp.zeros((SUBLANES, LANES), F32)
    ends = lax.fori_loop(0, K, pass1, (zero,) * (2 * ncol))

    carries = []
    for j in range(ncol):
        cr, ci = cols(j)
        kr = lamk_ref[0:1, pl.ds(j * LANES, LANES)]
        ki = lamk_ref[1:2, pl.ds(j * LANES, LANES)]
        er, ei = ends[2 * j], ends[2 * j + 1]
        c_r = [carry_ref[0:1, cr]]
        c_i = [carry_ref[0:1, ci]]
        for r in range(1, SUBLANES + 1):
            pr, pi = c_r[-1], c_i[-1]
            c_r.append(kr * pr - ki * pi + er[r - 1:r, :])
            c_i.append(kr * pi + ki * pr + ei[r - 1:r, :])
        carry_ref[0:1, cr] = c_r[SUBLANES]
        carry_ref[0:1, ci] = c_i[SUBLANES]
        carries += [jnp.concatenate(c_r[:SUBLANES], axis=0), jnp.concatenate(c_i[:SUBLANES], axis=0)]

    def pass2(k, st):
        out = []
        for j in range(ncol):
            cr, ci = cols(j)
            dr, di = st[2 * j], st[2 * j + 1]
            nr = a_re[j] * dr - a_im[j] * di
            ni = a_re[j] * di + a_im[j] * dr
            s_ref[rows(k), cr] = s_ref[rows(k), cr] + nr
            s_ref[rows(k), ci] = s_ref[rows(k), ci] + ni
            out += [nr, ni]
        return tuple(out)

    lax.fori_loop(0, K, pass2, tuple(carries))

    y = jnp.dot(s_ref[...].astype(BF16), cblk_ref[...], preferred_element_type=F32)
    y = y + d_ref[...] * u
    y = jax.nn.gelu(y, approximate=True)
    gate = jnp.dot(y.astype(BF16), wglu_ref[...], preferred_element_type=F32) + bglu_ref[...]
    out = y * jax.nn.sigmoid(gate)
    for c in range(nio):
        for k in range(K):
            io_ref[c, sub_chunk_rows(k), :] = out[k * SUBLANES:(k + 1) * SUBLANES, c * LANES:(c + 1) * LANES]
    o_ref[...] = jnp.concatenate([io_ref[c] for c in range(nio)], axis=1).astype(BF16)


def s5_mixer(hp, u_col_block, batch, seq, bblk, lam, lamk, cblk, d_skip, w_glu, b_glu, tm=512):
    width = bblk.shape[0]
    n_state = bblk.shape[1] // 2
    nt = seq // tm
    return pl.pallas_call(
        functools.partial(_s5_kernel, n_state=n_state),
        grid=(batch, nt),
        in_specs=[
            pl.BlockSpec((tm, width), lambda b, t: (b * nt + t, u_col_block)),
            _const_spec(bblk.shape), _const_spec(lam.shape), _const_spec(lamk.shape),
            _const_spec(cblk.shape), _const_spec((1, width)), _const_spec(w_glu.shape),
            _const_spec((1, width)),
        ],
        out_specs=pl.BlockSpec((tm, width), lambda b, t: (b * nt + t, 0)),
        out_shape=jax.ShapeDtypeStruct((batch * seq, width), BF16),
        scratch_shapes=[pltpu.VMEM((width // LANES, tm, LANES), F32),
                        pltpu.VMEM((tm, 2 * n_state), F32),
                        pltpu.VMEM((SUBLANES, 2 * n_state), F32)],
        compiler_params=_params("parallel", "arbitrary"),
        name="s5_mixer",
    )(hp, bblk, lam, lamk, cblk, d_skip.reshape(1, width), w_glu, b_glu.reshape(1, width))


def s5_tables(lam_re, lam_im, log_dt, b_re, b_im, c_re, c_im, sub_chunk):
    G, P = lam_re.shape
    dt = jnp.exp(log_dt.astype(F32))[:, None]

    def lam_pow(n):
        mag = jnp.exp(lam_re * dt * n)
        return mag * jnp.cos(lam_im * dt * n), mag * jnp.sin(lam_im * dt * n)

    lr, li = lam_pow(1.0)
    kr, ki = lam_pow(float(sub_chunk))
    nr, ni = lr - 1.0, li
    den = lam_re * lam_re + lam_im * lam_im
    zr = (nr * lam_re + ni * lam_im) / den
    zi = (ni * lam_re - nr * lam_im) / den
    bbr = b_re * zr[..., None] - b_im * zi[..., None]
    bbi = b_re * zi[..., None] + b_im * zr[..., None]
    eye = jnp.eye(G, dtype=F32)
    Cg = b_re.shape[-1]

    def in_blk(m):
        return jnp.einsum('gpc,gh->gchp', m, eye).reshape(G * Cg, G * P)

    def out_blk(m):
        return jnp.einsum('gcp,gh->gphc', m, eye).reshape(G * P, G * Cg)

    bblk = jnp.concatenate([in_blk(bbr), in_blk(bbi)], axis=1).astype(BF16)
    cblk = jnp.concatenate([out_blk(c_re), -out_blk(c_im)], axis=0).astype(BF16)
    lam = jnp.stack([lr.reshape(-1), li.reshape(-1)])
    lamk = jnp.stack([kr.reshape(-1), ki.reshape(-1)])
    return bblk, lam, lamk, cblk


def _log_sigmoid_pair(z):
    soft = jnp.log(1.0 + jnp.exp(-jnp.abs(z)))
    lp = jnp.minimum(z, 0.0) - soft
    return lp, lp - z


SB_EXIT = -110.0


def _sb_kernel(q_ref, k_ref, v_ref, tri_ref, o_ref, acc_ref, carry_ref, *, blk, group, nsub):
    i = pl.program_id(2)
    lane = lax.broadcasted_iota(jnp.int32, (1, LANES), 1)
    head_lanes = [lane < SB_HEAD_DIM, lane >= SB_HEAD_DIM]
    zero = jnp.zeros((), BF16)
    qh = []
    for s in range(nsub):
        q = q_ref[s * blk:(s + 1) * blk, :]
        qh.append([jnp.where(m, q, zero) for m in head_lanes])
    row = lax.broadcasted_iota(jnp.int32, (blk, blk), 0)
    col = lax.broadcasted_iota(jnp.int32, (blk, blk), 1)
    strictly_earlier = col < row

    acc_ref[...] = jnp.zeros_like(acc_ref)
    carry_ref[...] = jnp.zeros_like(carry_ref)

    def group_step(g, first):
        ks, vs = {}, {}
        for d in range(1 - group, nsub):
            j = i * nsub - g * group + d
            start = pl.multiple_of(jnp.maximum(j, 0) * blk, blk)
            ks[d] = k_ref[pl.ds(start, blk), :]
            vj = v_ref[pl.ds(start, blk), :]
            vs[d] = [jnp.where(m & (j >= 0), vj, zero) for m in head_lanes]
        chains = [(s, u, h) for s in range(nsub) for u in range(group) for h in range(2)]
        zs = {c: lax.dot_general(qh[c[0]][c[2]], ks[c[0] - c[1]], (((1,), (1,)), ((), ())),
                                 preferred_element_type=F32) for c in chains}
        lps, rs = {}, {}
        for c in chains:
            lp, l1m = _log_sigmoid_pair(zs[c])
            if first and c[1] == 0:
                l1m = jnp.where(strictly_earlier, l1m, 0.0)
            lps[c] = lp
            rs[c] = jnp.dot(l1m.astype(BF16), tri_ref[...], preferred_element_type=F32)
        ws = {}
        top = None
        for s in range(nsub):
            for h in range(2):
                cum = carry_ref[s, h]
                for u in range(group):
                    c = (s, u, h)
                    w = jnp.exp(lps[c] + rs[c][:, :blk] + cum)
                    if first and u == 0:
                        w = jnp.where(strictly_earlier, w, 0.0)
                    ws[c] = w.astype(BF16)
                    cum = cum + rs[c][:, blk:]
                carry_ref[s, h] = cum
                top = cum if top is None else jnp.maximum(top, cum)
        for s in range(nsub):
            acc = acc_ref[s]
            for u in range(group):
                for h in range(2):
                    acc = acc + jnp.dot(ws[(s, u, h)], vs[s - u][h], preferred_element_type=F32)
            acc_ref[s] = acc
        return jnp.max(top)

    def more(state):
        g, top = state
        return (g * group <= i * nsub + nsub - 1) & (top > SB_EXIT)

    def step(state):
        g, _ = state
        return g + 1, group_step(g, False)

    lax.while_loop(more, step, (1, group_step(0, True)))
    for s in range(nsub):
        o_ref[s * blk:(s + 1) * blk, :] = acc_ref[s].astype(BF16)


def stick_breaking(hp, q_blk0, k_blk0, v_blk0, batch, seq, blk=128, group=3, nsub=8):
    assert blk == LANES
    tile = nsub * blk
    nq = seq // tile
    n_pairs = SB_HEADS * SB_HEAD_DIM // LANES
    r = np.arange(blk)
    tri = np.concatenate([(r[:, None] > r[None, :]), np.ones((blk, blk), bool)], axis=1)
    tri = jnp.asarray(tri, BF16)
    return pl.pallas_call(
        functools.partial(_sb_kernel, blk=blk, group=group, nsub=nsub),
        grid=(batch, n_pairs, nq),
        in_specs=[
            pl.BlockSpec((tile, LANES), lambda b, p, i: (b * nq + i, q_blk0 + p)),
            pl.BlockSpec((seq, LANES), lambda b, p, i: (b, k_blk0 + p)),
            pl.BlockSpec((seq, LANES), lambda b, p, i: (b, v_blk0 + p)),
            _const_spec(tri.shape),
        ],
        out_specs=pl.BlockSpec((tile, LANES), lambda b, p, i: (b * nq + i, p)),
        out_shape=jax.ShapeDtypeStruct((batch * seq, n_pairs * LANES), BF16),
        scratch_shapes=[pltpu.VMEM((nsub, blk, LANES), F32), pltpu.VMEM((nsub, 2, blk, blk), F32)],
        compiler_params=_params("parallel", "parallel", "arbitrary"),
        name="stick_breaking",
    )(hp, hp, hp, tri)


def _ret_kernel(qk_ref, v_ref, g_ref, cos_ref, sin_ref, mask_ref, rdec_ref, cdec_ref, tdec_ref, gn_ref,
                o_ref, state_ref):
    @pl.when(pl.program_id(1) == 0)
    def _():
        state_ref[...] = jnp.zeros_like(state_ref)

    half = RET_HEADS * RET_DK // 2
    qk = qk_ref[...].astype(F32)
    cos, sin = cos_ref[...], sin_ref[...]

    def rope(t):
        t1, t2 = t[:, :half], t[:, half:]
        return jnp.concatenate([t1 * cos - t2 * sin, t1 * sin + t2 * cos], axis=1)

    qr = rope(qk[:, :2 * half])
    kr = rope(qk[:, 2 * half:])
    kb = kr.astype(BF16)
    kt = kr.T
    lane = lax.broadcasted_iota(jnp.int32, (1, 2 * half), 1)
    head_of_lane = (lane % half) // (RET_DK // 2)
    for h in range(RET_HEADS):
        qh = jnp.where(head_of_lane == h, qr, 0.0)
        vh = v_ref[:, h * RET_DV:(h + 1) * RET_DV]
        scores = lax.dot_general(qh.astype(BF16), kb, (((1,), (1,)), ((), ())), preferred_element_type=F32)
        scores = scores * mask_ref[h]
        o = jnp.dot(scores.astype(BF16), vh, preferred_element_type=F32)
        o = o + jnp.dot((qh * rdec_ref[h]).astype(BF16), state_ref[h].astype(BF16), preferred_element_type=F32)
        kv = jnp.dot((kt * cdec_ref[h]).astype(BF16), vh, preferred_element_type=F32)
        state_ref[h] = tdec_ref[h] * state_ref[h] + kv
        mu = jnp.mean(o, axis=-1, keepdims=True)
        oc = o - mu
        var = jnp.mean(oc * oc, axis=-1, keepdims=True)
        on = oc * lax.rsqrt(var + LN_EPS) * gn_ref[:, h * RET_DV:(h + 1) * RET_DV]
        g = g_ref[:, h * RET_DV:(h + 1) * RET_DV].astype(F32)
        o_ref[:, h * RET_DV:(h + 1) * RET_DV] = (g * jax.nn.sigmoid(g) * on).astype(BF16)


def retention_tables(seq, tile):
    halfdim = RET_DK // 2
    inv_freq = ROPE_BASE ** (-jnp.arange(halfdim, dtype=F32) / halfdim)
    ang = jnp.arange(seq, dtype=F32)[:, None] * inv_freq[None, :]
    cos = jnp.tile(jnp.cos(ang), (1, RET_HEADS))
    sin = jnp.tile(jnp.sin(ang), (1, RET_HEADS))
    log_gamma = jnp.log(1.0 - 2.0 ** (-5.0 - jnp.arange(RET_HEADS, dtype=F32)))
    t = jnp.arange(tile, dtype=F32)
    same_or_earlier_chunk = (t[None, :] // CHUNK) <= (t[:, None] // CHUNK)
    mask = jnp.exp(log_gamma[:, None, None] * jnp.abs(t[:, None] - t[None, :]))
    mask = jnp.where(same_or_earlier_chunk[None], mask, 0.0)
    rdec = jnp.exp(log_gamma[:, None, None] * (t[None, :, None] + 1.0))
    cdec = jnp.exp(log_gamma[:, None, None] * (tile - 1.0 - t[None, None, :]))
    tdec = jnp.broadcast_to(jnp.exp(log_gamma * tile)[:, None, None], (RET_HEADS, 1, RET_DV))
    return cos, sin, mask, rdec, cdec, tdec


def retention(hp, qk_blk, v_blk, g_blk, batch, seq, gn_g, tile=512):
    cos, sin, mask, rdec, cdec, tdec = retention_tables(seq, tile)
    nt = seq // tile
    qkw = 2 * RET_HEADS * RET_DK
    vw = RET_HEADS * RET_DV
    half = RET_HEADS * RET_DK // 2
    return pl.pallas_call(
        _ret_kernel,
        grid=(batch, nt),
        in_specs=[
            pl.BlockSpec((tile, qkw), lambda b, t: (b * nt + t, qk_blk)),
            pl.BlockSpec((tile, vw), lambda b, t: (b * nt + t, v_blk)),
            pl.BlockSpec((tile, vw), lambda b, t: (b * nt + t, g_blk)),
            pl.BlockSpec((tile, half), lambda b, t: (t, 0)),
            pl.BlockSpec((tile, half), lambda b, t: (t, 0)),
            _const_spec(mask.shape), _const_spec(rdec.shape), _const_spec(cdec.shape),
            _const_spec(tdec.shape), _const_spec((1, vw)),
        ],
        out_specs=pl.BlockSpec((tile, vw), lambda b, t: (b * nt + t, 0)),
        out_shape=jax.ShapeDtypeStruct((batch * seq, vw), BF16),
        scratch_shapes=[pltpu.VMEM((RET_HEADS, 2 * half, RET_DV), F32)],
        compiler_params=_params("parallel", "arbitrary"),
        name="retention",
    )(hp, hp, hp, cos, sin, mask, rdec, cdec, tdec, gn_g.reshape(1, vw))


def _first_max(vals, lane):
    m = jnp.max(vals, axis=-1, keepdims=True)
    idx = jnp.min(jnp.where(vals == m, lane, float(LANES)), axis=-1, keepdims=True)
    return m, idx


def _merge_kernel(ga_ref, gb_ref, gc_ref, ya_ref, yb_ref, yc_ref, h_ref, wa_ref, wb_ref, wc_ref, wo_ref,
                  g1_ref, b1_ref, rw_ref, rb_ref, h1_ref, info_ref, infot_ref, cnt_ref, run_ref, merged_ref,
                  hprev_ref, *, col_chunk):
    step = pl.program_id(0)

    @pl.when(step == 0)
    def _():
        run_ref[...] = jnp.zeros_like(run_ref)
        hprev_ref[...] = jnp.zeros_like(hprev_ref)

    _route(hprev_ref[...], rw_ref, rb_ref, info_ref, infot_ref, cnt_ref, run_ref, count=step > 0)

    d_model = h_ref.shape[1]
    for c in range(0, d_model, col_chunk):
        cs = slice(c, c + col_chunk)
        acc = ga_ref[:, cs].astype(F32) * jnp.dot(ya_ref[...], wa_ref[:, cs], preferred_element_type=F32)
        acc = acc + gb_ref[:, cs].astype(F32) * jnp.dot(yb_ref[...], wb_ref[:, cs], preferred_element_type=F32)
        acc = acc + gc_ref[:, cs].astype(F32) * jnp.dot(yc_ref[...], wc_ref[:, cs], preferred_element_type=F32)
        merged_ref[:, cs] = acc.astype(BF16)
    for c in range(0, d_model, col_chunk):
        cs = slice(c, c + col_chunk)
        h1_ref[:, cs] = ALPHA * h_ref[:, cs] + jnp.dot(merged_ref[...], wo_ref[:, cs], preferred_element_type=F32)
    h1 = _layer_norm(h1_ref[...], g1_ref[...], b1_ref[...])
    h1_ref[...] = h1
    hprev_ref[...] = h1


def _route(h1, rw_ref, rb_ref, info_ref, infot_ref, cnt_ref, run_ref, count):
    h_hi = h1.astype(BF16)
    h_lo = (h1 - h_hi.astype(F32)).astype(BF16)
    logits = (jnp.dot(h_hi, rw_ref[0], preferred_element_type=F32)
              + jnp.dot(h_lo, rw_ref[0], preferred_element_type=F32)
              + jnp.dot(h_hi, rw_ref[1], preferred_element_type=F32)) + rb_ref[...]
    tm = logits.shape[0]
    lane_i = lax.broadcasted_iota(jnp.int32, (tm, LANES), 1)
    lane = lane_i.astype(F32)
    ex = jnp.exp(logits - jnp.max(logits, axis=-1, keepdims=True))
    scores = ex / jnp.sum(ex, axis=-1, keepdims=True)
    group = (lane_i // EXPERTS_PER_GROUP).astype(F32)
    best = jnp.zeros((tm, 1), F32)
    best_score = jnp.full((tm, 1), -1.0, F32)
    for gi in range(N_EXPERTS // EXPERTS_PER_GROUP):
        sg = jnp.where(group == float(gi), scores, -1.0)
        m1, i1 = _first_max(sg, lane)
        m2 = jnp.max(jnp.where(lane == i1, -1.0, sg), axis=-1, keepdims=True)
        gs = m1 + m2
        better = gs > best_score
        best = jnp.where(better, float(gi), best)
        best_score = jnp.where(better, gs, best_score)
    masked = jnp.where(group == best, scores, -1.0)
    w1, e1 = _first_max(masked, lane)
    w2, e2 = _first_max(jnp.where(lane == e1, -2.0, masked), lane)
    den = w1 + w2
    gate1, gate2 = w1 / den, w2 / den

    onehot = jnp.where(((lane == e1) | (lane == e2)) & count, 1.0, 0.0)
    run_ref[0:1, :] = run_ref[0:1, :] + jnp.sum(onehot, axis=0, keepdims=True)
    cnt_ref[...] = run_ref[...]

    info = jnp.zeros((tm, LANES), F32)
    for k, val in enumerate((e1, e2, gate1, gate2)):
        info = jnp.where(lane_i == k, val, info)
    info_ref[...] = info
    infot_ref[...] = info.T[:SUBLANES]


def merge_norm_route(hp, ya, yb, yc, h, wa, wb, wc, wo, ln_g, ln_b, rw, rb, tm=512, col_chunk=512):
    T, D = h.shape
    nt = T // tm

    def tile(col):
        return lambda i: (jnp.minimum(i, nt - 1), col)

    routed = lambda i: jnp.maximum(i - 1, 0)
    return pl.pallas_call(
        functools.partial(_merge_kernel, col_chunk=col_chunk),
        grid=(nt + 1,),
        in_specs=[
            pl.BlockSpec((tm, D), tile(0)), pl.BlockSpec((tm, D), tile(1)), pl.BlockSpec((tm, D), tile(2)),
            pl.BlockSpec((tm, ya.shape[1]), tile(0)), pl.BlockSpec((tm, yb.shape[1]), tile(0)),
            pl.BlockSpec((tm, yc.shape[1]), tile(0)), pl.BlockSpec((tm, D), tile(0)),
            _const_spec(wa.shape), _const_spec(wb.shape), _const_spec(wc.shape), _const_spec(wo.shape),
            _const_spec((1, D)), _const_spec((1, D)), _const_spec(rw.shape), _const_spec(rb.shape),
        ],
        out_specs=[pl.BlockSpec((tm, D), tile(0)), pl.BlockSpec((tm, LANES), lambda i: (routed(i), 0)),
                   pl.BlockSpec((SUBLANES, tm), lambda i: (0, routed(i))),
                   pl.BlockSpec((SUBLANES, LANES), lambda i: (0, 0))],
        out_shape=[jax.ShapeDtypeStruct((T, D), F32), jax.ShapeDtypeStruct((T, LANES), F32),
                   jax.ShapeDtypeStruct((SUBLANES, T), F32), jax.ShapeDtypeStruct((SUBLANES, LANES), F32)],
        scratch_shapes=[pltpu.VMEM((SUBLANES, LANES), F32), pltpu.VMEM((tm, D), BF16), pltpu.VMEM((tm, D), F32)],
        compiler_params=_params("arbitrary"),
        name="merge_norm_route",
    )(hp, hp, hp, ya, yb, yc, h, wa, wb, wc, wo, ln_g.reshape(1, D), ln_b.reshape(1, D), rw, rb)


ROW_UNROLL = 8


def _expert_kernel(be_ref, nused_ref, base_ref, rows_ref, h_ref, w1_ref, w3_ref, w2_ref, ys_ref,
                   xa, xb, ya, yb, w1b, w3b, w2b, gsem, ssem, *, n_tok):
    j = pl.program_id(0)
    n_used = nused_ref[0]
    blk = xa.shape[0]
    n_blocks = base_ref.shape[0] - 1
    tok_mask = n_tok - 1
    assert n_tok & tok_mask == 0

    def gather_copy(base, r, x_dst, sem):
        tok = rows_ref[base + r] & tok_mask
        return pltpu.make_async_copy(h_ref.at[pl.ds(tok, 1)], x_dst.at[pl.ds(r, 1)], sem)

    def scatter_copy(base, r, y_src, sem):
        return pltpu.make_async_copy(y_src.at[pl.ds(r, 1)], ys_ref.at[pl.ds(rows_ref[base + r], 1)], sem)

    def wait_gathered(x_dst, sem):
        pltpu.make_async_copy(h_ref.at[pl.ds(0, blk)], x_dst, sem).wait()

    def wait_scattered(y_src, sem):
        pltpu.make_async_copy(y_src, ys_ref.at[pl.ds(0, blk)], sem).wait()

    def rolled(start_row_copy):
        def body(c, carry):
            for k in range(ROW_UNROLL):
                start_row_copy(c * ROW_UNROLL + k)
            return carry
        lax.fori_loop(0, blk // ROW_UNROLL, body, 0)

    def block_step(b, x_own, x_other, y_own, y_other, g_own, g_other, s_own, s_other, leads_pair):
        @pl.when(b < n_used)
        def _():
            if leads_pair:
                @pl.when(b == 0)
                def _():
                    rolled(lambda r: gather_copy(base_ref[0], r, x_own, g_own).start())
                    y_other[...] = jnp.zeros_like(y_other)

            wait_gathered(x_own, g_own)

            @pl.when(b >= 1)
            def _():
                wait_scattered(y_own, s_own)

            if leads_pair:
                @pl.when((b == 0) | (be_ref[b] != be_ref[jnp.maximum(b - 2, 0)]))
                def _():
                    w1b[...] = w1_ref[0, 0].astype(BF16)
                    w3b[...] = w3_ref[0, 0].astype(BF16)
                    w2b[...] = w2_ref[0, 0].astype(BF16)

            base_next = base_ref[jnp.minimum(b + 1, n_blocks)]
            base_prev = base_ref[jnp.where(b >= 1, b - 1, n_blocks)]
            for r in range(blk):
                gather_copy(base_next, r, x_other, g_other).start()
                scatter_copy(base_prev, r, y_other, s_other).start()
            x = x_own[...].astype(BF16)
            a = jnp.dot(x, w1b[...], preferred_element_type=F32)
            g = jnp.dot(x, w3b[...], preferred_element_type=F32)
            hb = (a * jax.nn.sigmoid(a) * g).astype(BF16)
            y_own[...] = jnp.dot(hb, w2b[...], preferred_element_type=F32)

            @pl.when(b == n_used - 1)
            def _():
                wait_scattered(y_other, s_other)
                rolled(lambda r: scatter_copy(base_ref[b], r, y_own, s_own).start())
                wait_scattered(y_own, s_own)
                wait_gathered(x_other, g_other)

    block_step(2 * j, xa, xb, ya, yb, gsem.at[0], gsem.at[1], ssem.at[0], ssem.at[1], True)
    block_step(2 * j + 1, xb, xa, yb, ya, gsem.at[1], gsem.at[0], ssem.at[1], ssem.at[0], False)


def moe_experts(h1, rows, base, block_e, n_used, w1, w3, w2, layer, blk):
    T, D = h1.shape
    Fd = w1.shape[-1]
    nb = block_e.shape[0]
    assert nb % 2 == 0
    wspec = lambda s: pl.BlockSpec((1, 1) + s, lambda j, be, *_: (layer, be[2 * j], 0, 0))
    return pl.pallas_call(
        functools.partial(_expert_kernel, n_tok=T),
        grid_spec=pltpu.PrefetchScalarGridSpec(
            num_scalar_prefetch=4,
            grid=(nb // 2,),
            in_specs=[pl.BlockSpec(memory_space=pl.ANY), wspec((D, Fd)), wspec((D, Fd)), wspec((Fd, D))],
            out_specs=pl.BlockSpec(memory_space=pl.ANY),
            scratch_shapes=[pltpu.VMEM((blk, D), F32), pltpu.VMEM((blk, D), F32),
                            pltpu.VMEM((blk, D), F32), pltpu.VMEM((blk, D), F32),
                            pltpu.VMEM((D, Fd), BF16), pltpu.VMEM((D, Fd), BF16), pltpu.VMEM((Fd, D), BF16),
                            pltpu.SemaphoreType.DMA((2,)), pltpu.SemaphoreType.DMA((2,))],
        ),
        out_shape=jax.ShapeDtypeStruct((2 * T + blk, D), F32),
        compiler_params=pltpu.CompilerParams(dimension_semantics=("arbitrary",), vmem_limit_bytes=VMEM_LIMIT,
                                             has_side_effects=True),
        name="moe_experts",
    )(block_e, n_used, base, rows, h1, w1, w3, w2)


def _combine_kernel(y1_ref, y2_ref, h_ref, info_ref, g_ref, b_ref, o_ref):
    o_ref[...] = _moe_post_norm(y1_ref[...], y2_ref[...], h_ref[...], info_ref[...], g_ref[...], b_ref[...])


def moe_combine(ys, h, info, ln_g, ln_b, tm=512):
    T, D = h.shape
    nt = T // tm
    return pl.pallas_call(
        _combine_kernel,
        grid=(nt,),
        in_specs=[pl.BlockSpec((tm, D), lambda i: (i, 0)),
                  pl.BlockSpec((tm, D), lambda i: (nt + i, 0)),
                  pl.BlockSpec((tm, D), lambda i: (i, 0)),
                  pl.BlockSpec((tm, LANES), lambda i: (i, 0)),
                  _const_spec((1, D)), _const_spec((1, D))],
        out_specs=pl.BlockSpec((tm, D), lambda i: (i, 0)),
        out_shape=jax.ShapeDtypeStruct((T, D), F32),
        compiler_params=_params("parallel"),
        name="moe_combine",
    )(ys, ys, h, info, ln_g.reshape(1, D), ln_b.reshape(1, D))


def route_tables(info_t, counts, blk, n_blocks):
    n_assign = info_t.shape[1] * 2
    pair = 2 * blk
    experts = info_t[0:2].reshape(-1).astype(jnp.int32)
    pad_experts = jnp.repeat(jnp.arange(N_EXPERTS, dtype=jnp.int32), pair)
    keys = jnp.concatenate([2 * experts, 2 * pad_experts + 1])
    index_bits = int(keys.shape[0] - 1).bit_length()
    packed = jnp.sort((keys << index_bits) | jnp.arange(keys.shape[0], dtype=jnp.int32))
    order = packed & ((1 << index_bits) - 1)
    rows = jnp.where(order < n_assign, order, n_assign + (order - n_assign) % blk)
    cnt = counts[0, :N_EXPERTS].astype(jnp.int32)
    padded = (cnt + pair - 1) // pair * pair
    pends = jnp.cumsum(padded)
    block_row = jnp.arange(n_blocks, dtype=jnp.int32) * blk
    block_e = jnp.minimum(jnp.sum(block_row[:, None] >= pends[None, :], axis=1), N_EXPERTS - 1).astype(jnp.int32)
    of_block = block_e[:, None] == jnp.arange(N_EXPERTS, dtype=jnp.int32)[None, :]

    def per_block(table):
        return jnp.sum(jnp.where(of_block, table[None, :], 0), axis=1)

    segment_start = jnp.cumsum(cnt + pair) - (cnt + pair)
    base = per_block(segment_start) + block_row - per_block(pends - padded)
    all_padding = rows.shape[0] - blk
    base = jnp.where(block_row < pends[-1], base, all_padding)
    base = jnp.concatenate([base, jnp.full((1,), all_padding, jnp.int32)]).astype(jnp.int32)
    n_used = (pends[-1:] // blk).astype(jnp.int32)
    return rows, base, block_e, n_used


def grouped_moe(h1, info_t, counts, w1, w3, w2, layer, blk=512):
    T, D = h1.shape
    n_blocks = (2 * T) // blk + 2 * N_EXPERTS
    rows, base, block_e, n_used = route_tables(info_t, counts, blk, n_blocks)
    return moe_experts(h1, rows, base, block_e, n_used, w1, w3, w2, layer, blk)


def _in_proj_layout(d_model):
    s5w = d_model // 4
    sbw = SB_HEADS * SB_HEAD_DIM
    rqk = RET_HEADS * RET_DK
    rv = RET_HEADS * RET_DV
    sizes = (s5w, sbw, sbw, sbw, rqk, rqk, rv, rv, d_model, d_model, d_model)
    off = np.concatenate([[0], np.cumsum(sizes)]).tolist()

    def permute(w):
        u_a, q_b, k_b, v_b, q_c, k_c, v_c, g_c, ga, gb, gc = [w[:, off[i]:off[i + 1]] for i in range(11)]

        def rope_perm(m):
            m = m.reshape(d_model, RET_HEADS, 2, RET_DK // 2)
            return jnp.swapaxes(m, 1, 2).reshape(d_model, rqk)

        parts = [ga, gb, gc, rope_perm(q_c), rope_perm(k_c) * RET_DK ** -0.5, v_c, g_c,
                 q_b * SB_HEAD_DIM ** -0.5, k_b, v_b, u_a]
        return jnp.concatenate([m.astype(BF16) for m in parts], axis=1)

    o_qk = 3 * d_model
    o_v = o_qk + 2 * rqk
    o_g = o_v + rv
    o_sb = o_g + rv
    o_u = o_sb + 3 * sbw
    blocks = dict(qk=o_qk // (2 * rqk), v=o_v // rv, g=o_g // rv, sb_q=o_sb // LANES,
                  sb_k=(o_sb + sbw) // LANES, sb_v=(o_sb + 2 * sbw) // LANES, u=o_u // s5w)
    assert o_qk % (2 * rqk) == 0 and o_v % rv == 0 and o_g % rv == 0 and o_sb % LANES == 0 and o_u % s5w == 0
    return permute, blocks


def _layer(source, batch, seq, p):
    D = p['w_in'].shape[0]
    permute, blk = _in_proj_layout(D)
    hp, h = in_projection(source, permute(p['w_in']), n_gate_cols=3 * D)
    s5_tile = 512
    bblk, lam, lamk, cblk = s5_tables(p['lam_re'], p['lam_im'], p['log_dt'], p['b_re'], p['b_im'],
                                      p['c_re'], p['c_im'], s5_tile // SUBLANES)
    ya = s5_mixer(hp, blk['u'], batch, seq, bblk, lam, lamk, cblk, p['d_skip'], p['w_glu'].astype(BF16),
                  p['b_glu'], tm=s5_tile)
    yb = stick_breaking(hp, blk['sb_q'], blk['sb_k'], blk['sb_v'], batch, seq)
    yc = retention(hp, blk['qk'], blk['v'], blk['g'], batch, seq, p['gn_g'])
    rw = jnp.zeros((D, LANES), F32).at[:, :N_EXPERTS].set(p['router_w'])
    rw_hi = rw.astype(BF16)
    rw = jnp.stack([rw_hi, (rw - rw_hi.astype(F32)).astype(BF16)])
    rb = jnp.full((1, LANES), NEG_BIG, F32).at[0, :N_EXPERTS].set(p['router_b'])
    h1, info, info_t, counts = merge_norm_route(
        hp, ya, yb, yc, h, p['w_up_a'].astype(BF16), p['w_up_b'].astype(BF16), p['w_up_c'].astype(BF16),
        p['w_out'].astype(BF16), p['ln1_g'], p['ln1_b'], rw, rb)
    ys = grouped_moe(h1, info_t, counts, p['moe_w1'], p['moe_w3'], p['moe_w2'], p['layer'])
    return ("moe", ys, h1, info, p['ln2_g'], p['ln2_b'])


def kernel(x, ln0_g, ln0_b, w_in, s5_lambda_re, s5_lambda_im, s5_log_dt, s5_b_re, s5_b_im, s5_c_re, s5_c_im,
           s5_d, s5_w_glu, s5_b_glu, ret_gn_g, w_up_a, w_up_b, w_up_c, w_out, ln1_g, ln1_b, router_w, router_b,
           moe_w1, moe_w3, moe_w2, ln2_g, ln2_b):
    batch, seq, D = x.shape
    source = ("raw", x.reshape(batch * seq, D), ln0_g, ln0_b)
    for l in range(w_in.shape[0]):
        p = dict(w_in=w_in[l], lam_re=s5_lambda_re[l], lam_im=s5_lambda_im[l], log_dt=s5_log_dt[l],
                 b_re=s5_b_re[l], b_im=s5_b_im[l], c_re=s5_c_re[l], c_im=s5_c_im[l], d_skip=s5_d[l],
                 w_glu=s5_w_glu[l], b_glu=s5_b_glu[l], gn_g=ret_gn_g[l], w_up_a=w_up_a[l], w_up_b=w_up_b[l],
                 w_up_c=w_up_c[l], w_out=w_out[l], ln1_g=ln1_g[l], ln1_b=ln1_b[l], router_w=router_w,
                 router_b=router_b, moe_w1=moe_w1, moe_w3=moe_w3, moe_w2=moe_w2, layer=l, ln2_g=ln2_g[l],
                 ln2_b=ln2_b[l])
        source = _layer(source, batch, seq, p)
    return moe_combine(*source[1:]).reshape(batch, seq, D)
```

```python
import functools

import numpy as np
import jax
import jax.numpy as jnp
from jax import lax
from jax.experimental import pallas as pl
from jax.experimental.pallas import tpu as pltpu

F32 = jnp.float32
BF16 = jnp.bfloat16

LANES = 128
SUBLANES = 8
VMEM_LIMIT = 56 * 1024 * 1024

DEPTH = 2
CHUNK = 64
SB_HEADS = 4
SB_HEAD_DIM = 64
RET_HEADS = 4
RET_DK = 64
RET_DV = 128
ROPE_BASE = 10000.0
N_EXPERTS = 16
EXPERTS_PER_GROUP = 4
ALPHA = (2 * DEPTH) ** 0.25
LN_EPS = 1e-5

NEG_BIG = -1e30


def _params(*sem):
    return pltpu.CompilerParams(dimension_semantics=sem, vmem_limit_bytes=VMEM_LIMIT)


def _const_spec(shape):
    zeros = (0,) * len(shape)
    return pl.BlockSpec(shape, lambda *_: zeros, pipeline_mode=pl.Buffered(1))


def _layer_norm(x, g, b):
    mu = jnp.mean(x, axis=-1, keepdims=True)
    xc = x - mu
    var = jnp.mean(xc * xc, axis=-1, keepdims=True)
    return xc * lax.rsqrt(var + LN_EPS) * g + b


def _moe_post_norm(y1, y2, h1, info, g, b):
    return _layer_norm(ALPHA * h1 + info[:, 2:3] * y1 + info[:, 3:4] * y2, g, b)


def _inproj_kernel(*refs, col_chunk, n_gate_cols, from_moe):
    if from_moe:
        y1_ref, y2_ref, h1_ref, info_ref, g_ref, b_ref, w_ref, o_ref, h_ref = refs
        x = _moe_post_norm(y1_ref[...], y2_ref[...], h1_ref[...], info_ref[...], g_ref[...], b_ref[...])
    else:
        x_ref, g_ref, b_ref, w_ref, o_ref, h_ref = refs
        x = _layer_norm(x_ref[...], g_ref[...], b_ref[...])
    h_ref[...] = x
    xb = x.astype(BF16)
    n = w_ref.shape[1]
    for c in range(0, n, col_chunk):
        acc = jnp.dot(xb, w_ref[:, c:c + col_chunk], preferred_element_type=F32)
        if c < n_gate_cols:
            acc = 0.5 * jnp.tanh(0.5 * acc) + 0.5
        o_ref[:, c:c + col_chunk] = acc.astype(BF16)


def in_projection(source, w_bf16, n_gate_cols, tm=512, col_chunk=512):
    kind, *operands = source
    gain, bias = operands[-2:]
    T, D = operands[1].shape if kind == "moe" else operands[0].shape
    N = w_bf16.shape[1]
    nt = T // tm
    assert n_gate_cols % col_chunk == 0
    rows = pl.BlockSpec((tm, D), lambda i: (i, 0))
    if kind == "moe":
        ys, h1, info = operands[:3]
        args = [ys, ys, h1, info]
        in_specs = [rows, pl.BlockSpec((tm, D), lambda i: (nt + i, 0)), rows,
                    pl.BlockSpec((tm, LANES), lambda i: (i, 0))]
    else:
        args, in_specs = [operands[0]], [rows]
    return pl.pallas_call(
        functools.partial(_inproj_kernel, col_chunk=col_chunk, n_gate_cols=n_gate_cols, from_moe=kind == "moe"),
        grid=(nt,),
        in_specs=in_specs + [_const_spec((1, D)), _const_spec((1, D)), _const_spec((D, N))],
        out_specs=[pl.BlockSpec((tm, N), lambda i: (i, 0)), rows],
        out_shape=[jax.ShapeDtypeStruct((T, N), BF16), jax.ShapeDtypeStruct((T, D), F32)],
        compiler_params=_params("parallel"),
        name="in_proj",
    )(*args, gain.reshape(1, D), bias.reshape(1, D), w_bf16)


def _s5_kernel(u_ref, bblk_ref, lam_ref, lamk_ref, cblk_ref, d_ref, wglu_ref, bglu_ref,
               o_ref, io_ref, s_ref, carry_ref, *, n_state):
    tm, width = u_ref.shape
    K = tm // SUBLANES
    ncol = n_state // LANES
    nio = width // LANES

    @pl.when(pl.program_id(1) == 0)
    def _():
        carry_ref[...] = jnp.zeros_like(carry_ref)

    def sub_chunk_rows(k):
        return pl.ds(k, SUBLANES, stride=K)

    for c in range(nio):
        io_ref[c] = u_ref[:, c * LANES:(c + 1) * LANES].astype(F32)
    u = jnp.concatenate(
        [jnp.concatenate([io_ref[c, sub_chunk_rows(k), :] for k in range(K)], axis=0) for c in range(nio)],
        axis=1)
    s_ref[...] = jnp.dot(u.astype(BF16), bblk_ref[...], preferred_element_type=F32)

    def cols(j):
        return pl.ds(j * LANES, LANES), pl.ds(n_state + j * LANES, LANES)

    def rows(k):
        return pl.ds(pl.multiple_of(k * SUBLANES, SUBLANES), SUBLANES)

    a_re = [jnp.broadcast_to(lam_ref[0:1, pl.ds(j * LANES, LANES)], (SUBLANES, LANES)) for j in range(ncol)]
    a_im = [jnp.broadcast_to(lam_ref[1:2, pl.ds(j * LANES, LANES)], (SUBLANES, LANES)) for j in range(ncol)]

    def pass1(k, st):
        out = []
        for j in range(ncol):
            cr, ci = cols(j)
            sr, si = st[2 * j], st[2 * j + 1]
            nr = a_re[j] * sr - a_im[j] * si + s_ref[rows(k), cr]
            ni = a_re[j] * si + a_im[j] * sr + s_ref[rows(k), ci]
            s_ref[rows(k), cr] = nr
            s_ref[rows(k), ci] = ni
            out += [nr, ni]
        return tuple(out)

    zero = jnp.zeros((SUBLANES, LANES), F32)
    ends = lax.fori_loop(0, K, pass1, (zero,) * (2 * ncol))

    carries = []
    for j in range(ncol):
        cr, ci = cols(j)
        kr = lamk_ref[0:1, pl.ds(j * LANES, LANES)]
        ki = lamk_ref[1:2, pl.ds(j * LANES, LANES)]
        er, ei = ends[2 * j], ends[2 * j + 1]
        c_r = [carry_ref[0:1, cr]]
        c_i = [carry_ref[0:1, ci]]
        for r in range(1, SUBLANES + 1):
            pr, pi = c_r[-1], c_i[-1]
            c_r.append(kr * pr - ki * pi + er[r - 1:r, :])
            c_i.append(kr * pi + ki * pr + ei[r - 1:r, :])
        carry_ref[0:1, cr] = c_r[SUBLANES]
        carry_ref[0:1, ci] = c_i[SUBLANES]
        carries += [jnp.concatenate(c_r[:SUBLANES], axis=0), jnp.concatenate(c_i[:SUBLANES], axis=0)]

    def pass2(k, st):
        out = []
        for j in range(ncol):
            cr, ci = cols(j)
            dr, di = st[2 * j], st[2 * j + 1]
            nr = a_re[j] * dr - a_im[j] * di
            ni = a_re[j] * di + a_im[j] * dr
            s_ref[rows(k), cr] = s_ref[rows(k), cr] + nr
            s_ref[rows(k), ci] = s_ref[rows(k), ci] + ni
            out += [nr, ni]
        return tuple(out)

    lax.fori_loop(0, K, pass2, tuple(carries))

    y = jnp.dot(s_ref[...].astype(BF16), cblk_ref[...], preferred_element_type=F32)
    y = y + d_ref[...] * u
    y = jax.nn.gelu(y, approximate=True)
    gate = jnp.dot(y.astype(BF16), wglu_ref[...], preferred_element_type=F32) + bglu_ref[...]
    out = y * jax.nn.sigmoid(gate)
    for c in range(nio):
        for k in range(K):
            io_ref[c, sub_chunk_rows(k), :] = out[k * SUBLANES:(k + 1) * SUBLANES, c * LANES:(c + 1) * LANES]
    o_ref[...] = jnp.concatenate([io_ref[c] for c in range(nio)], axis=1).astype(BF16)


def s5_mixer(hp, u_col_block, batch, seq, bblk, lam, lamk, cblk, d_skip, w_glu, b_glu, tm=512):
    width = bblk.shape[0]
    n_state = bblk.shape[1] // 2
    nt = seq // tm
    return pl.pallas_call(
        functools.partial(_s5_kernel, n_state=n_state),
        grid=(batch, nt),
        in_specs=[
            pl.BlockSpec((tm, width), lambda b, t: (b * nt + t, u_col_block)),
            _const_spec(bblk.shape), _const_spec(lam.shape), _const_spec(lamk.shape),
            _const_spec(cblk.shape), _const_spec((1, width)), _const_spec(w_glu.shape),
            _const_spec((1, width)),
        ],
        out_specs=pl.BlockSpec((tm, width), lambda b, t: (b * nt + t, 0)),
        out_shape=jax.ShapeDtypeStruct((batch * seq, width), BF16),
        scratch_shapes=[pltpu.VMEM((width // LANES, tm, LANES), F32),
                        pltpu.VMEM((tm, 2 * n_state), F32),
                        pltpu.VMEM((SUBLANES, 2 * n_state), F32)],
        compiler_params=_params("parallel", "arbitrary"),
        name="s5_mixer",
    )(hp, bblk, lam, lamk, cblk, d_skip.reshape(1, width), w_glu, b_glu.reshape(1, width))


def s5_tables(lam_re, lam_im, log_dt, b_re, b_im, c_re, c_im, sub_chunk):
    G, P = lam_re.shape
    dt = jnp.exp(log_dt.astype(F32))[:, None]

    def lam_pow(n):
        mag = jnp.exp(lam_re * dt * n)
        return mag * jnp.cos(lam_im * dt * n), mag * jnp.sin(lam_im * dt * n)

    lr, li = lam_pow(1.0)
    kr, ki = lam_pow(float(sub_chunk))
    nr, ni = lr - 1.0, li
    den = lam_re * lam_re + lam_im * lam_im
    zr = (nr * lam_re + ni * lam_im) / den
    zi = (ni * lam_re - nr * lam_im) / den
    bbr = b_re * zr[..., None] - b_im * zi[..., None]
    bbi = b_re * zi[..., None] + b_im * zr[..., None]
    eye = jnp.eye(G, dtype=F32)
    Cg = b_re.shape[-1]

    def in_blk(m):
        return jnp.einsum('gpc,gh->gchp', m, eye).reshape(G * Cg, G * P)

    def out_blk(m):
        return jnp.einsum('gcp,gh->gphc', m, eye).reshape(G * P, G * Cg)

    bblk = jnp.concatenate([in_blk(bbr), in_blk(bbi)], axis=1).astype(BF16)
    cblk = jnp.concatenate([out_blk(c_re), -out_blk(c_im)], axis=0).astype(BF16)
    lam = jnp.stack([lr.reshape(-1), li.reshape(-1)])
    lamk = jnp.stack([kr.reshape(-1), ki.reshape(-1)])
    return bblk, lam, lamk, cblk


def _log_sigmoid_pair(z):
    soft = jnp.log(1.0 + jnp.exp(-jnp.abs(z)))
    lp = jnp.minimum(z, 0.0) - soft
    return lp, lp - z


SB_EXIT = -110.0


def _sb_kernel(q_ref, k_ref, v_ref, tri_ref, o_ref, acc_ref, carry_ref, *, blk, group, nsub):
    i = pl.program_id(2)
    lane = lax.broadcasted_iota(jnp.int32, (1, LANES), 1)
    head_lanes = [lane < SB_HEAD_DIM, lane >= SB_HEAD_DIM]
    zero = jnp.zeros((), BF16)
    qh = []
    for s in range(nsub):
        q = q_ref[s * blk:(s + 1) * blk, :]
        qh.append([jnp.where(m, q, zero) for m in head_lanes])
    row = lax.broadcasted_iota(jnp.int32, (blk, blk), 0)
    col = lax.broadcasted_iota(jnp.int32, (blk, blk), 1)
    strictly_earlier = col < row

    acc_ref[...] = jnp.zeros_like(acc_ref)
    carry_ref[...] = jnp.zeros_like(carry_ref)

    def group_step(g, first):
        ks, vs = {}, {}
        for d in range(1 - group, nsub):
            j = i * nsub - g * group + d
            start = pl.multiple_of(jnp.maximum(j, 0) * blk, blk)
            ks[d] = k_ref[pl.ds(start, blk), :]
            vj = v_ref[pl.ds(start, blk), :]
            vs[d] = [jnp.where(m & (j >= 0), vj, zero) for m in head_lanes]
        chains = [(s, u, h) for s in range(nsub) for u in range(group) for h in range(2)]
        zs = {c: lax.dot_general(qh[c[0]][c[2]], ks[c[0] - c[1]], (((1,), (1,)), ((), ())),
                                 preferred_element_type=F32) for c in chains}
        lps, rs = {}, {}
        for c in chains:
            lp, l1m = _log_sigmoid_pair(zs[c])
            if first and c[1] == 0:
                l1m = jnp.where(strictly_earlier, l1m, 0.0)
            lps[c] = lp
            rs[c] = jnp.dot(l1m.astype(BF16), tri_ref[...], preferred_element_type=F32)
        ws = {}
        top = None
        for s in range(nsub):
            for h in range(2):
                cum = carry_ref[s, h]
                for u in range(group):
                    c = (s, u, h)
                    w = jnp.exp(lps[c] + rs[c][:, :blk] + cum)
                    if first and u == 0:
                        w = jnp.where(strictly_earlier, w, 0.0)
                    ws[c] = w.astype(BF16)
                    cum = cum + rs[c][:, blk:]
                carry_ref[s, h] = cum
                top = cum if top is None else jnp.maximum(top, cum)
        for s in range(nsub):
            acc = acc_ref[s]
            for u in range(group):
                for h in range(2):
                    acc = acc + jnp.dot(ws[(s, u, h)], vs[s - u][h], preferred_element_type=F32)
            acc_ref[s] = acc
        return jnp.max(top)

    def more(state):
        g, top = state
        return (g * group <= i * nsub + nsub - 1) & (top > SB_EXIT)

    def step(state):
        g, _ = state
        return g + 1, group_step(g, False)

    lax.while_loop(more, step, (1, group_step(0, True)))
    for s in range(nsub):
        o_ref[s * blk:(s + 1) * blk, :] = acc_ref[s].astype(BF16)


def stick_breaking(hp, q_blk0, k_blk0, v_blk0, batch, seq, blk=128, group=3, nsub=8):
    assert blk == LANES
    tile = nsub * blk
    nq = seq // tile
    n_pairs = SB_HEADS * SB_HEAD_DIM // LANES
    r = np.arange(blk)
    tri = np.concatenate([(r[:, None] > r[None, :]), np.ones((blk, blk), bool)], axis=1)
    tri = jnp.asarray(tri, BF16)
    return pl.pallas_call(
        functools.partial(_sb_kernel, blk=blk, group=group, nsub=nsub),
        grid=(batch, n_pairs, nq),
        in_specs=[
            pl.BlockSpec((tile, LANES), lambda b, p, i: (b * nq + i, q_blk0 + p)),
            pl.BlockSpec((seq, LANES), lambda b, p, i: (b, k_blk0 + p)),
            pl.BlockSpec((seq, LANES), lambda b, p, i: (b, v_blk0 + p)),
            _const_spec(tri.shape),
        ],
        out_specs=pl.BlockSpec((tile, LANES), lambda b, p, i: (b * nq + i, p)),
        out_shape=jax.ShapeDtypeStruct((batch * seq, n_pairs * LANES), BF16),
        scratch_shapes=[pltpu.VMEM((nsub, blk, LANES), F32), pltpu.VMEM((nsub, 2, blk, blk), F32)],
        compiler_params=_params("parallel", "parallel", "arbitrary"),
        name="stick_breaking",
    )(hp, hp, hp, tri)


def _ret_kernel(qk_ref, v_ref, g_ref, cos_ref, sin_ref, mask_ref, rdec_ref, cdec_ref, tdec_ref, gn_ref,
                o_ref, state_ref):
    @pl.when(pl.program_id(1) == 0)
    def _():
        state_ref[...] = jnp.zeros_like(state_ref)

    half = RET_HEADS * RET_DK // 2
    qk = qk_ref[...].astype(F32)
    cos, sin = cos_ref[...], sin_ref[...]

    def rope(t):
        t1, t2 = t[:, :half], t[:, half:]
        return jnp.concatenate([t1 * cos - t2 * sin, t1 * sin + t2 * cos], axis=1)

    qr = rope(qk[:, :2 * half])
    kr = rope(qk[:, 2 * half:])
    kb = kr.astype(BF16)
    kt = kr.T
    lane = lax.broadcasted_iota(jnp.int32, (1, 2 * half), 1)
    head_of_lane = (lane % half) // (RET_DK // 2)
    for h in range(RET_HEADS):
        qh = jnp.where(head_of_lane == h, qr, 0.0)
        vh = v_ref[:, h * RET_DV:(h + 1) * RET_DV]
        scores = lax.dot_general(qh.astype(BF16), kb, (((1,), (1,)), ((), ())), preferred_element_type=F32)
        scores = scores * mask_ref[h]
        o = jnp.dot(scores.astype(BF16), vh, preferred_element_type=F32)
        o = o + jnp.dot((qh * rdec_ref[h]).astype(BF16), state_ref[h].astype(BF16), preferred_element_type=F32)
        kv = jnp.dot((kt * cdec_ref[h]).astype(BF16), vh, preferred_element_type=F32)
        state_ref[h] = tdec_ref[h] * state_ref[h] + kv
        mu = jnp.mean(o, axis=-1, keepdims=True)
        oc = o - mu
        var = jnp.mean(oc * oc, axis=-1, keepdims=True)
        on = oc * lax.rsqrt(var + LN_EPS) * gn_ref[:, h * RET_DV:(h + 1) * RET_DV]
        g = g_ref[:, h * RET_DV:(h + 1) * RET_DV].astype(F32)
        o_ref[:, h * RET_DV:(h + 1) * RET_DV] = (g * jax.nn.sigmoid(g) * on).astype(BF16)


def retention_tables(seq, tile):
    halfdim = RET_DK // 2
    inv_freq = ROPE_BASE ** (-jnp.arange(halfdim, dtype=F32) / halfdim)
    ang = jnp.arange(seq, dtype=F32)[:, None] * inv_freq[None, :]
    cos = jnp.tile(jnp.cos(ang), (1, RET_HEADS))
    sin = jnp.tile(jnp.sin(ang), (1, RET_HEADS))
    log_gamma = jnp.log(1.0 - 2.0 ** (-5.0 - jnp.arange(RET_HEADS, dtype=F32)))
    t = jnp.arange(tile, dtype=F32)
    same_or_earlier_chunk = (t[None, :] // CHUNK) <= (t[:, None] // CHUNK)
    mask = jnp.exp(log_gamma[:, None, None] * jnp.abs(t[:, None] - t[None, :]))
    mask = jnp.where(same_or_earlier_chunk[None], mask, 0.0)
    rdec = jnp.exp(log_gamma[:, None, None] * (t[None, :, None] + 1.0))
    cdec = jnp.exp(log_gamma[:, None, None] * (tile - 1.0 - t[None, None, :]))
    tdec = jnp.broadcast_to(jnp.exp(log_gamma * tile)[:, None, None], (RET_HEADS, 1, RET_DV))
    return cos, sin, mask, rdec, cdec, tdec


def retention(hp, qk_blk, v_blk, g_blk, batch, seq, gn_g, tile=512):
    cos, sin, mask, rdec, cdec, tdec = retention_tables(seq, tile)
    nt = seq // tile
    qkw = 2 * RET_HEADS * RET_DK
    vw = RET_HEADS * RET_DV
    half = RET_HEADS * RET_DK // 2
    return pl.pallas_call(
        _ret_kernel,
        grid=(batch, nt),
        in_specs=[
            pl.BlockSpec((tile, qkw), lambda b, t: (b * nt + t, qk_blk)),
            pl.BlockSpec((tile, vw), lambda b, t: (b * nt + t, v_blk)),
            pl.BlockSpec((tile, vw), lambda b, t: (b * nt + t, g_blk)),
            pl.BlockSpec((tile, half), lambda b, t: (t, 0)),
            pl.BlockSpec((tile, half), lambda b, t: (t, 0)),
            _const_spec(mask.shape), _const_spec(rdec.shape), _const_spec(cdec.shape),
            _const_spec(tdec.shape), _const_spec((1, vw)),
        ],
        out_specs=pl.BlockSpec((tile, vw), lambda b, t: (b * nt + t, 0)),
        out_shape=jax.ShapeDtypeStruct((batch * seq, vw), BF16),
        scratch_shapes=[pltpu.VMEM((RET_HEADS, 2 * half, RET_DV), F32)],
        compiler_params=_params("parallel", "arbitrary"),
        name="retention",
    )(hp, hp, hp, cos, sin, mask, rdec, cdec, tdec, gn_g.reshape(1, vw))


def _first_max(vals, lane):
    m = jnp.max(vals, axis=-1, keepdims=True)
    idx = jnp.min(jnp.where(vals == m, lane, float(LANES)), axis=-1, keepdims=True)
    return m, idx


def _merge_kernel(ga_ref, gb_ref, gc_ref, ya_ref, yb_ref, yc_ref, h_ref, wa_ref, wb_ref, wc_ref, wo_ref,
                  g1_ref, b1_ref, rw_ref, rb_ref, h1_ref, info_ref, infot_ref, cnt_ref, run_ref, merged_ref,
                  hprev_ref, *, col_chunk):
    step = pl.program_id(0)

    @pl.when(step == 0)
    def _():
        run_ref[...] = jnp.zeros_like(run_ref)
        hprev_ref[...] = jnp.zeros_like(hprev_ref)

    _route(hprev_ref[...], rw_ref, rb_ref, info_ref, infot_ref, cnt_ref, run_ref, count=step > 0)

    d_model = h_ref.shape[1]
    for c in range(0, d_model, col_chunk):
        cs = slice(c, c + col_chunk)
        acc = ga_ref[:, cs].astype(F32) * jnp.dot(ya_ref[...], wa_ref[:, cs], preferred_element_type=F32)
        acc = acc + gb_ref[:, cs].astype(F32) * jnp.dot(yb_ref[...], wb_ref[:, cs], preferred_element_type=F32)
        acc = acc + gc_ref[:, cs].astype(F32) * jnp.dot(yc_ref[...], wc_ref[:, cs], preferred_element_type=F32)
        merged_ref[:, cs] = acc.astype(BF16)
    for c in range(0, d_model, col_chunk):
        cs = slice(c, c + col_chunk)
        h1_ref[:, cs] = ALPHA * h_ref[:, cs] + jnp.dot(merged_ref[...], wo_ref[:, cs], preferred_element_type=F32)
    h1 = _layer_norm(h1_ref[...], g1_ref[...], b1_ref[...])
    h1_ref[...] = h1
    hprev_ref[...] = h1


def _route(h1, rw_ref, rb_ref, info_ref, infot_ref, cnt_ref, run_ref, count):
    h_hi = h1.astype(BF16)
    h_lo = (h1 - h_hi.astype(F32)).astype(BF16)
    logits = (jnp.dot(h_hi, rw_ref[0], preferred_element_type=F32)
              + jnp.dot(h_lo, rw_ref[0], preferred_element_type=F32)
              + jnp.dot(h_hi, rw_ref[1], preferred_element_type=F32)) + rb_ref[...]
    tm = logits.shape[0]
    lane_i = lax.broadcasted_iota(jnp.int32, (tm, LANES), 1)
    lane = lane_i.astype(F32)
    ex = jnp.exp(logits - jnp.max(logits, axis=-1, keepdims=True))
    scores = ex / jnp.sum(ex, axis=-1, keepdims=True)
    group = (lane_i // EXPERTS_PER_GROUP).astype(F32)
    best = jnp.zeros((tm, 1), F32)
    best_score = jnp.full((tm, 1), -1.0, F32)
    for gi in range(N_EXPERTS // EXPERTS_PER_GROUP):
        sg = jnp.where(group == float(gi), scores, -1.0)
        m1, i1 = _first_max(sg, lane)
        m2 = jnp.max(jnp.where(lane == i1, -1.0, sg), axis=-1, keepdims=True)
        gs = m1 + m2
        better = gs > best_score
        best = jnp.where(better, float(gi), best)
        best_score = jnp.where(better, gs, best_score)
    masked = jnp.where(group == best, scores, -1.0)
    w1, e1 = _first_max(masked, lane)
    w2, e2 = _first_max(jnp.where(lane == e1, -2.0, masked), lane)
    den = w1 + w2
    gate1, gate2 = w1 / den, w2 / den

    onehot = jnp.where(((lane == e1) | (lane == e2)) & count, 1.0, 0.0)
    run_ref[0:1, :] = run_ref[0:1, :] + jnp.sum(onehot, axis=0, keepdims=True)
    cnt_ref[...] = run_ref[...]

    info = jnp.zeros((tm, LANES), F32)
    for k, val in enumerate((e1, e2, gate1, gate2)):
        info = jnp.where(lane_i == k, val, info)
    info_ref[...] = info
    infot_ref[...] = info.T[:SUBLANES]


def merge_norm_route(hp, ya, yb, yc, h, wa, wb, wc, wo, ln_g, ln_b, rw, rb, tm=512, col_chunk=512):
    T, D = h.shape
    nt = T // tm

    def tile(col):
        return lambda i: (jnp.minimum(i, nt - 1), col)

    routed = lambda i: jnp.maximum(i - 1, 0)
    return pl.pallas_call(
        functools.partial(_merge_kernel, col_chunk=col_chunk),
        grid=(nt + 1,),
        in_specs=[
            pl.BlockSpec((tm, D), tile(0)), pl.BlockSpec((tm, D), tile(1)), pl.BlockSpec((tm, D), tile(2)),
            pl.BlockSpec((tm, ya.shape[1]), tile(0)), pl.BlockSpec((tm, yb.shape[1]), tile(0)),
            pl.BlockSpec((tm, yc.shape[1]), tile(0)), pl.BlockSpec((tm, D), tile(0)),
            _const_spec(wa.shape), _const_spec(wb.shape), _const_spec(wc.shape), _const_spec(wo.shape),
            _const_spec((1, D)), _const_spec((1, D)), _const_spec(rw.shape), _const_spec(rb.shape),
        ],
        out_specs=[pl.BlockSpec((tm, D), tile(0)), pl.BlockSpec((tm, LANES), lambda i: (routed(i), 0)),
                   pl.BlockSpec((SUBLANES, tm), lambda i: (0, routed(i))),
                   pl.BlockSpec((SUBLANES, LANES), lambda i: (0, 0))],
        out_shape=[jax.ShapeDtypeStruct((T, D), F32), jax.ShapeDtypeStruct((T, LANES), F32),
                   jax.ShapeDtypeStruct((SUBLANES, T), F32), jax.ShapeDtypeStruct((SUBLANES, LANES), F32)],
        scratch_shapes=[pltpu.VMEM((SUBLANES, LANES), F32), pltpu.VMEM((tm, D), BF16), pltpu.VMEM((tm, D), F32)],
        compiler_params=_params("arbitrary"),
        name="merge_norm_route",
    )(hp, hp, hp, ya, yb, yc, h, wa, wb, wc, wo, ln_g.reshape(1, D), ln_b.reshape(1, D), rw, rb)


ROW_UNROLL = 8


def _expert_kernel(be_ref, nused_ref, base_ref, rows_ref, h_ref, w1_ref, w3_ref, w2_ref, ys_ref,
                   xa, xb, ya, yb, w1b, w3b, w2b, gsem, ssem, *, n_tok):
    j = pl.program_id(0)
    n_used = nused_ref[0]
    blk = xa.shape[0]
    n_blocks = base_ref.shape[0] - 1
    tok_mask = n_tok - 1
    assert n_tok & tok_mask == 0

    def gather_copy(base, r, x_dst, sem):
        tok = rows_ref[base + r] & tok_mask
        return pltpu.make_async_copy(h_ref.at[pl.ds(tok, 1)], x_dst.at[pl.ds(r, 1)], sem)

    def scatter_copy(base, r, y_src, sem):
        return pltpu.make_async_copy(y_src.at[pl.ds(r, 1)], ys_ref.at[pl.ds(rows_ref[base + r], 1)], sem)

    def wait_gathered(x_dst, sem):
        pltpu.make_async_copy(h_ref.at[pl.ds(0, blk)], x_dst, sem).wait()

    def wait_scattered(y_src, sem):
        pltpu.make_async_copy(y_src, ys_ref.at[pl.ds(0, blk)], sem).wait()

    def rolled(start_row_copy):
        def body(c, carry):
            for k in range(ROW_UNROLL):
                start_row_copy(c * ROW_UNROLL + k)
            return carry
        lax.fori_loop(0, blk // ROW_UNROLL, body, 0)

    def block_step(b, x_own, x_other, y_own, y_other, g_own, g_other, s_own, s_other, leads_pair):
        @pl.when(b < n_used)
        def _():
            if leads_pair:
                @pl.when(b == 0)
                def _():
                    rolled(lambda r: gather_copy(base_ref[0], r, x_own, g_own).start())
                    y_other[...] = jnp.zeros_like(y_other)

            wait_gathered(x_own, g_own)

            @pl.when(b >= 1)
            def _():
                wait_scattered(y_own, s_own)

            if leads_pair:
                @pl.when((b == 0) | (be_ref[b] != be_ref[jnp.maximum(b - 2, 0)]))
                def _():
                    w1b[...] = w1_ref[0, 0].astype(BF16)
                    w3b[...] = w3_ref[0, 0].astype(BF16)
                    w2b[...] = w2_ref[0, 0].astype(BF16)

            base_next = base_ref[jnp.minimum(b + 1, n_blocks)]
            base_prev = base_ref[jnp.where(b >= 1, b - 1, n_blocks)]
            for r in range(blk):
                gather_copy(base_next, r, x_other, g_other).start()
                scatter_copy(base_prev, r, y_other, s_other).start()
            x = x_own[...].astype(BF16)
            a = jnp.dot(x, w1b[...], preferred_element_type=F32)
            g = jnp.dot(x, w3b[...], preferred_element_type=F32)
            hb = (a * jax.nn.sigmoid(a) * g).astype(BF16)
            y_own[...] = jnp.dot(hb, w2b[...], preferred_element_type=F32)

            @pl.when(b == n_used - 1)
            def _():
                wait_scattered(y_other, s_other)
                rolled(lambda r: scatter_copy(base_ref[b], r, y_own, s_own).start())
                wait_scattered(y_own, s_own)
                wait_gathered(x_other, g_other)

    block_step(2 * j, xa, xb, ya, yb, gsem.at[0], gsem.at[1], ssem.at[0], ssem.at[1], True)
    block_step(2 * j + 1, xb, xa, yb, ya, gsem.at[1], gsem.at[0], ssem.at[1], ssem.at[0], False)


def moe_experts(h1, rows, base, block_e, n_used, w1, w3, w2, layer, blk):
    T, D = h1.shape
    Fd = w1.shape[-1]
    nb = block_e.shape[0]
    assert nb % 2 == 0
    wspec = lambda s: pl.BlockSpec((1, 1) + s, lambda j, be, *_: (layer, be[2 * j], 0, 0))
    return pl.pallas_call(
        functools.partial(_expert_kernel, n_tok=T),
        grid_spec=pltpu.PrefetchScalarGridSpec(
            num_scalar_prefetch=4,
            grid=(nb // 2,),
            in_specs=[pl.BlockSpec(memory_space=pl.ANY), wspec((D, Fd)), wspec((D, Fd)), wspec((Fd, D))],
            out_specs=pl.BlockSpec(memory_space=pl.ANY),
            scratch_shapes=[pltpu.VMEM((blk, D), F32), pltpu.VMEM((blk, D), F32),
                            pltpu.VMEM((blk, D), F32), pltpu.VMEM((blk, D), F32),
                            pltpu.VMEM((D, Fd), BF16), pltpu.VMEM((D, Fd), BF16), pltpu.VMEM((Fd, D), BF16),
                            pltpu.SemaphoreType.DMA((2,)), pltpu.SemaphoreType.DMA((2,))],
        ),
        out_shape=jax.ShapeDtypeStruct((2 * T + blk, D), F32),
        compiler_params=pltpu.CompilerParams(dimension_semantics=("arbitrary",), vmem_limit_bytes=VMEM_LIMIT,
                                             has_side_effects=True),
        name="moe_experts",
    )(block_e, n_used, base, rows, h1, w1, w3, w2)


def _combine_kernel(y1_ref, y2_ref, h_ref, info_ref, g_ref, b_ref, o_ref):
    o_ref[...] = _moe_post_norm(y1_ref[...], y2_ref[...], h_ref[...], info_ref[...], g_ref[...], b_ref[...])


def moe_combine(ys, h, info, ln_g, ln_b, tm=512):
    T, D = h.shape
    nt = T // tm
    return pl.pallas_call(
        _combine_kernel,
        grid=(nt,),
        in_specs=[pl.BlockSpec((tm, D), lambda i: (i, 0)),
                  pl.BlockSpec((tm, D), lambda i: (nt + i, 0)),
                  pl.BlockSpec((tm, D), lambda i: (i, 0)),
                  pl.BlockSpec((tm, LANES), lambda i: (i, 0)),
                  _const_spec((1, D)), _const_spec((1, D))],
        out_specs=pl.BlockSpec((tm, D), lambda i: (i, 0)),
        out_shape=jax.ShapeDtypeStruct((T, D), F32),
        compiler_params=_params("parallel"),
        name="moe_combine",
    )(ys, ys, h, info, ln_g.reshape(1, D), ln_b.reshape(1, D))


def route_tables(info_t, counts, blk, n_blocks):
    n_assign = info_t.shape[1] * 2
    pair = 2 * blk
    experts = info_t[0:2].reshape(-1).astype(jnp.int32)
    pad_experts = jnp.repeat(jnp.arange(N_EXPERTS, dtype=jnp.int32), pair)
    keys = jnp.concatenate([2 * experts, 2 * pad_experts + 1])
    index_bits = int(keys.shape[0] - 1).bit_length()
    packed = jnp.sort((keys << index_bits) | jnp.arange(keys.shape[0], dtype=jnp.int32))
    order = packed & ((1 << index_bits) - 1)
    rows = jnp.where(order < n_assign, order, n_assign + (order - n_assign) % blk)
    cnt = counts[0, :N_EXPERTS].astype(jnp.int32)
    padded = (cnt + pair - 1) // pair * pair
    pends = jnp.cumsum(padded)
    block_row = jnp.arange(n_blocks, dtype=jnp.int32) * blk
    block_e = jnp.minimum(jnp.sum(block_row[:, None] >= pends[None, :], axis=1), N_EXPERTS - 1).astype(jnp.int32)
    of_block = block_e[:, None] == jnp.arange(N_EXPERTS, dtype=jnp.int32)[None, :]

    def per_block(table):
        return jnp.sum(jnp.where(of_block, table[None, :], 0), axis=1)

    segment_start = jnp.cumsum(cnt + pair) - (cnt + pair)
    base = per_block(segment_start) + block_row - per_block(pends - padded)
    all_padding = rows.shape[0] - blk
    base = jnp.where(block_row < pends[-1], base, all_padding)
    base = jnp.concatenate([base, jnp.full((1,), all_padding, jnp.int32)]).astype(jnp.int32)
    n_used = (pends[-1:] // blk).astype(jnp.int32)
    return rows, base, block_e, n_used


def grouped_moe(h1, info_t, counts, w1, w3, w2, layer, blk=512):
    T, D = h1.shape
    n_blocks = (2 * T) // blk + 2 * N_EXPERTS
    rows, base, block_e, n_used = route_tables(info_t, counts, blk, n_blocks)
    return moe_experts(h1, rows, base, block_e, n_used, w1, w3, w2, layer, blk)


def _in_proj_layout(d_model):
    s5w = d_model // 4
    sbw = SB_HEADS * SB_HEAD_DIM
    rqk = RET_HEADS * RET_DK
    rv = RET_HEADS * RET_DV
    sizes = (s5w, sbw, sbw, sbw, rqk, rqk, rv, rv, d_model, d_model, d_model)
    off = np.concatenate([[0], np.cumsum(sizes)]).tolist()

    def permute(w):
        u_a, q_b, k_b, v_b, q_c, k_c, v_c, g_c, ga, gb, gc = [w[:, off[i]:off[i + 1]] for i in range(11)]

        def rope_perm(m):
            m = m.reshape(d_model, RET_HEADS, 2, RET_DK // 2)
            return jnp.swapaxes(m, 1, 2).reshape(d_model, rqk)

        parts = [ga, gb, gc, rope_perm(q_c), rope_perm(k_c) * RET_DK ** -0.5, v_c, g_c,
                 q_b * SB_HEAD_DIM ** -0.5, k_b, v_b, u_a]
        return jnp.concatenate([m.astype(BF16) for m in parts], axis=1)

    o_qk = 3 * d_model
    o_v = o_qk + 2 * rqk
    o_g = o_v + rv
    o_sb = o_g + rv
    o_u = o_sb + 3 * sbw
    blocks = dict(qk=o_qk // (2 * rqk), v=o_v // rv, g=o_g // rv, sb_q=o_sb // LANES,
                  sb_k=(o_sb + sbw) // LANES, sb_v=(o_sb + 2 * sbw) // LANES, u=o_u // s5w)
    assert o_qk % (2 * rqk) == 0 and o_v % rv == 0 and o_g % rv == 0 and o_sb % LANES == 0 and o_u % s5w == 0
    return permute, blocks


def _layer(source, batch, seq, p):
    D = p['w_in'].shape[0]
    permute, blk = _in_proj_layout(D)
    hp, h = in_projection(source, permute(p['w_in']), n_gate_cols=3 * D)
    s5_tile = 512
    bblk, lam, lamk, cblk = s5_tables(p['lam_re'], p['lam_im'], p['log_dt'], p['b_re'], p['b_im'],
                                      p['c_re'], p['c_im'], s5_tile // SUBLANES)
    ya = s5_mixer(hp, blk['u'], batch, seq, bblk, lam, lamk, cblk, p['d_skip'], p['w_glu'].astype(BF16),
                  p['b_glu'], tm=s5_tile)
    yb = stick_breaking(hp, blk['sb_q'], blk['sb_k'], blk['sb_v'], batch, seq)
    yc = retention(hp, blk['qk'], blk['v'], blk['g'], batch, seq, p['gn_g'])
    rw = jnp.zeros((D, LANES), F32).at[:, :N_EXPERTS].set(p['router_w'])
    rw_hi = rw.astype(BF16)
    rw = jnp.stack([rw_hi, (rw - rw_hi.astype(F32)).astype(BF16)])
    rb = jnp.full((1, LANES), NEG_BIG, F32).at[0, :N_EXPERTS].set(p['router_b'])
    h1, info, info_t, counts = merge_norm_route(
        hp, ya, yb, yc, h, p['w_up_a'].astype(BF16), p['w_up_b'].astype(BF16), p['w_up_c'].astype(BF16),
        p['w_out'].astype(BF16), p['ln1_g'], p['ln1_b'], rw, rb)
    ys = grouped_moe(h1, info_t, counts, p['moe_w1'], p['moe_w3'], p['moe_w2'], p['layer'])
    return ("moe", ys, h1, info, p['ln2_g'], p['ln2_b'])


def kernel(x, ln0_g, ln0_b, w_in, s5_lambda_re, s5_lambda_im, s5_log_dt, s5_b_re, s5_b_im, s5_c_re, s5_c_im,
           s5_d, s5_w_glu, s5_b_glu, ret_gn_g, w_up_a, w_up_b, w_up_c, w_out, ln1_g, ln1_b, router_w, router_b,
           moe_w1, moe_w3, moe_w2, ln2_g, ln2_b):
    batch, seq, D = x.shape
    source = ("raw", x.reshape(batch * seq, D), ln0_g, ln0_b)
    for l in range(w_in.shape[0]):
        p = dict(w_in=w_in[l], lam_re=s5_lambda_re[l], lam_im=s5_lambda_im[l], log_dt=s5_log_dt[l],
                 b_re=s5_b_re[l], b_im=s5_b_im[l], c_re=s5_c_re[l], c_im=s5_c_im[l], d_skip=s5_d[l],
                 w_glu=s5_w_glu[l], b_glu=s5_b_glu[l], gn_g=ret_gn_g[l], w_up_a=w_up_a[l], w_up_b=w_up_b[l],
                 w_up_c=w_up_c[l], w_out=w_out[l], ln1_g=ln1_g[l], ln1_b=ln1_b[l], router_w=router_w,
                 router_b=router_b, moe_w1=moe_w1, moe_w3=moe_w3, moe_w2=moe_w2, layer=l, ln2_g=ln2_g[l],
                 ln2_b=ln2_b[l])
        source = _layer(source, batch, seq, p)
    return moe_combine(*source[1:]).reshape(batch, seq, D)
```

```python
import functools

import numpy as np
import jax
import jax.numpy as jnp
from jax import lax
from jax.experimental import pallas as pl
from jax.experimental.pallas import tpu as pltpu

F32 = jnp.float32
BF16 = jnp.bfloat16

LANES = 128
SUBLANES = 8
VMEM_LIMIT = 56 * 1024 * 1024

DEPTH = 2
CHUNK = 64
SB_HEADS = 4
SB_HEAD_DIM = 64
RET_HEADS = 4
RET_DK = 64
RET_DV = 128
ROPE_BASE = 10000.0
N_EXPERTS = 16
EXPERTS_PER_GROUP = 4
ALPHA = (2 * DEPTH) ** 0.25
LN_EPS = 1e-5

NEG_BIG = -1e30


def _params(*sem):
    return pltpu.CompilerParams(dimension_semantics=sem, vmem_limit_bytes=VMEM_LIMIT)


def _const_spec(shape):
    zeros = (0,) * len(shape)
    return pl.BlockSpec(shape, lambda *_: zeros, pipeline_mode=pl.Buffered(1))


def _layer_norm(x, g, b):
    mu = jnp.mean(x, axis=-1, keepdims=True)
    xc = x - mu
    var = jnp.mean(xc * xc, axis=-1, keepdims=True)
    return xc * lax.rsqrt(var + LN_EPS) * g + b


def _moe_post_norm(y1, y2, h1, info, g, b):
    return _layer_norm(ALPHA * h1 + info[:, 2:3] * y1 + info[:, 3:4] * y2, g, b)


def _inproj_kernel(*refs, col_chunk, n_gate_cols, from_moe):
    if from_moe:
        y1_ref, y2_ref, h1_ref, info_ref, g_ref, b_ref, w_ref, o_ref, h_ref = refs
        x = _moe_post_norm(y1_ref[...], y2_ref[...], h1_ref[...], info_ref[...], g_ref[...], b_ref[...])
    else:
        x_ref, g_ref, b_ref, w_ref, o_ref, h_ref = refs
        x = _layer_norm(x_ref[...], g_ref[...], b_ref[...])
    h_ref[...] = x
    xb = x.astype(BF16)
    n = w_ref.shape[1]
    for c in range(0, n, col_chunk):
        acc = jnp.dot(xb, w_ref[:, c:c + col_chunk], preferred_element_type=F32)
        if c < n_gate_cols:
            acc = 0.5 * jnp.tanh(0.5 * acc) + 0.5
        o_ref[:, c:c + col_chunk] = acc.astype(BF16)


def in_projection(source, w_bf16, n_gate_cols, tm=512, col_chunk=512):
    kind, *operands = source
    gain, bias = operands[-2:]
    T, D = operands[1].shape if kind == "moe" else operands[0].shape
    N = w_bf16.shape[1]
    nt = T // tm
    assert n_gate_cols % col_chunk == 0
    rows = pl.BlockSpec((tm, D), lambda i: (i, 0))
    if kind == "moe":
        ys, h1, info = operands[:3]
        args = [ys, ys, h1, info]
        in_specs = [rows, pl.BlockSpec((tm, D), lambda i: (nt + i, 0)), rows,
                    pl.BlockSpec((tm, LANES), lambda i: (i, 0))]
    else:
        args, in_specs = [operands[0]], [rows]
    return pl.pallas_call(
        functools.partial(_inproj_kernel, col_chunk=col_chunk, n_gate_cols=n_gate_cols, from_moe=kind == "moe"),
        grid=(nt,),
        in_specs=in_specs + [_const_spec((1, D)), _const_spec((1, D)), _const_spec((D, N))],
        out_specs=[pl.BlockSpec((tm, N), lambda i: (i, 0)), rows],
        out_shape=[jax.ShapeDtypeStruct((T, N), BF16), jax.ShapeDtypeStruct((T, D), F32)],
        compiler_params=_params("parallel"),
        name="in_proj",
    )(*args, gain.reshape(1, D), bias.reshape(1, D), w_bf16)


def _s5_kernel(u_ref, bblk_ref, lam_ref, lamk_ref, cblk_ref, d_ref, wglu_ref, bglu_ref,
               o_ref, io_ref, s_ref, carry_ref, *, n_state):
    tm, width = u_ref.shape
    K = tm // SUBLANES
    ncol = n_state // LANES
    nio = width // LANES

    @pl.when(pl.program_id(1) == 0)
    def _():
        carry_ref[...] = jnp.zeros_like(carry_ref)

    def sub_chunk_rows(k):
        return pl.ds(k, SUBLANES, stride=K)

    for c in range(nio):
        io_ref[c] = u_ref[:, c * LANES:(c + 1) * LANES].astype(F32)
    u = jnp.concatenate(
        [jnp.concatenate([io_ref[c, sub_chunk_rows(k), :] for k in range(K)], axis=0) for c in range(nio)],
        axis=1)
    s_ref[...] = jnp.dot(u.astype(BF16), bblk_ref[...], preferred_element_type=F32)

    def cols(j):
        return pl.ds(j * LANES, LANES), pl.ds(n_state + j * LANES, LANES)

    def rows(k):
        return pl.ds(pl.multiple_of(k * SUBLANES, SUBLANES), SUBLANES)

    a_re = [jnp.broadcast_to(lam_ref[0:1, pl.ds(j * LANES, LANES)], (SUBLANES, LANES)) for j in range(ncol)]
    a_im = [jnp.broadcast_to(lam_ref[1:2, pl.ds(j * LANES, LANES)], (SUBLANES, LANES)) for j in range(ncol)]

    def pass1(k, st):
        out = []
        for j in range(ncol):
            cr, ci = cols(j)
            sr, si = st[2 * j], st[2 * j + 1]
            nr = a_re[j] * sr - a_im[j] * si + s_ref[rows(k), cr]
            ni = a_re[j] * si + a_im[j] * sr + s_ref[rows(k), ci]
            s_ref[rows(k), cr] = nr
            s_ref[rows(k), ci] = ni
            out += [nr, ni]
        return tuple(out)

    zero = jnp.zeros((SUBLANES, LANES), F32)
    ends = lax.fori_loop(0, K, pass1, (zero,) * (2 * ncol))

    carries = []
    for j in range(ncol):
        cr, ci = cols(j)
        kr = lamk_ref[0:1, pl.ds(j * LANES, LANES)]
        ki = lamk_ref[1:2, pl.ds(j * LANES, LANES)]
        er, ei = ends[2 * j], ends[2 * j + 1]
        c_r = [carry_ref[0:1, cr]]
        c_i = [carry_ref[0:1, ci]]
        for r in range(1, SUBLANES + 1):
            pr, pi = c_r[-1], c_i[-1]
            c_r.append(kr * pr - ki * pi + er[r - 1:r, :])
            c_i.append(kr * pi + ki * pr + ei[r - 1:r, :])
        carry_ref[0:1, cr] = c_r[SUBLANES]
        carry_ref[0:1, ci] = c_i[SUBLANES]
        carries += [jnp.concatenate(c_r[:SUBLANES], axis=0), jnp.concatenate(c_i[:SUBLANES], axis=0)]

    def pass2(k, st):
        out = []
        for j in range(ncol):
            cr, ci = cols(j)
            dr, di = st[2 * j], st[2 * j + 1]
            nr = a_re[j] * dr - a_im[j] * di
            ni = a_re[j] * di + a_im[j] * dr
            s_ref[rows(k), cr] = s_ref[rows(k), cr] + nr
            s_ref[rows(k), ci] = s_ref[rows(k), ci] + ni
            out += [nr, ni]
        return tuple(out)

    lax.fori_loop(0, K, pass2, tuple(carries))

    y = jnp.dot(s_ref[...].astype(BF16), cblk_ref[...], preferred_element_type=F32)
    y = y + d_ref[...] * u
    y = jax.nn.gelu(y, approximate=True)
    gate = jnp.dot(y.astype(BF16), wglu_ref[...], preferred_element_type=F32) + bglu_ref[...]
    out = y * jax.nn.sigmoid(gate)
    for c in range(nio):
        for k in range(K):
            io_ref[c, sub_chunk_rows(k), :] = out[k * SUBLANES:(k + 1) * SUBLANES, c * LANES:(c + 1) * LANES]
    o_ref[...] = jnp.concatenate([io_ref[c] for c in range(nio)], axis=1).astype(BF16)


def s5_mixer(hp, u_col_block, batch, seq, bblk, lam, lamk, cblk, d_skip, w_glu, b_glu, tm=512):
    width = bblk.shape[0]
    n_state = bblk.shape[1] // 2
    nt = seq // tm
    return pl.pallas_call(
        functools.partial(_s5_kernel, n_state=n_state),
        grid=(batch, nt),
        in_specs=[
            pl.BlockSpec((tm, width), lambda b, t: (b * nt + t, u_col_block)),
            _const_spec(bblk.shape), _const_spec(lam.shape), _const_spec(lamk.shape),
            _const_spec(cblk.shape), _const_spec((1, width)), _const_spec(w_glu.shape),
            _const_spec((1, width)),
        ],
        out_specs=pl.BlockSpec((tm, width), lambda b, t: (b * nt + t, 0)),
        out_shape=jax.ShapeDtypeStruct((batch * seq, width), BF16),
        scratch_shapes=[pltpu.VMEM((width // LANES, tm, LANES), F32),
                        pltpu.VMEM((tm, 2 * n_state), F32),
                        pltpu.VMEM((SUBLANES, 2 * n_state), F32)],
        compiler_params=_params("parallel", "arbitrary"),
        name="s5_mixer",
    )(hp, bblk, lam, lamk, cblk, d_skip.reshape(1, width), w_glu, b_glu.reshape(1, width))


def s5_tables(lam_re, lam_im, log_dt, b_re, b_im, c_re, c_im, sub_chunk):
    G, P = lam_re.shape
    dt = jnp.exp(log_dt.astype(F32))[:, None]

    def lam_pow(n):
        mag = jnp.exp(lam_re * dt * n)
        return mag * jnp.cos(lam_im * dt * n), mag * jnp.sin(lam_im * dt * n)

    lr, li = lam_pow(1.0)
    kr, ki = lam_pow(float(sub_chunk))
    nr, ni = lr - 1.0, li
    den = lam_re * lam_re + lam_im * lam_im
    zr = (nr * lam_re + ni * lam_im) / den
    zi = (ni * lam_re - nr * lam_im) / den
    bbr = b_re * zr[..., None] - b_im * zi[..., None]
    bbi = b_re * zi[..., None] + b_im * zr[..., None]
    eye = jnp.eye(G, dtype=F32)
    Cg = b_re.shape[-1]

    def in_blk(m):
        return jnp.einsum('gpc,gh->gchp', m, eye).reshape(G * Cg, G * P)

    def out_blk(m):
        return jnp.einsum('gcp,gh->gphc', m, eye).reshape(G * P, G * Cg)

    bblk = jnp.concatenate([in_blk(bbr), in_blk(bbi)], axis=1).astype(BF16)
    cblk = jnp.concatenate([out_blk(c_re), -out_blk(c_im)], axis=0).astype(BF16)
    lam = jnp.stack([lr.reshape(-1), li.reshape(-1)])
    lamk = jnp.stack([kr.reshape(-1), ki.reshape(-1)])
    return bblk, lam, lamk, cblk


def _log_sigmoid_pair(z):
    soft = jnp.log(1.0 + jnp.exp(-jnp.abs(z)))
    lp = jnp.minimum(z, 0.0) - soft
    return lp, lp - z


SB_EXIT = -110.0


def _sb_kernel(q_ref, k_ref, v_ref, tri_ref, o_ref, acc_ref, carry_ref, *, blk, group, nsub):
    i = pl.program_id(2)
    lane = lax.broadcasted_iota(jnp.int32, (1, LANES), 1)
    head_lanes = [lane < SB_HEAD_DIM, lane >= SB_HEAD_DIM]
    zero = jnp.zeros((), BF16)
    qh = []
    for s in range(nsub):
        q = q_ref[s * blk:(s + 1) * blk, :]
        qh.append([jnp.where(m, q, zero) for m in head_lanes])
    row = lax.broadcasted_iota(jnp.int32, (blk, blk), 0)
    col = lax.broadcasted_iota(jnp.int32, (blk, blk), 1)
    strictly_earlier = col < row

    acc_ref[...] = jnp.zeros_like(acc_ref)
    carry_ref[...] = jnp.zeros_like(carry_ref)

    def group_step(g, first):
        ks, vs = {}, {}
        for d in range(1 - group, nsub):
            j = i * nsub - g * group + d
            start = pl.multiple_of(jnp.maximum(j, 0) * blk, blk)
            ks[d] = k_ref[pl.ds(start, blk), :]
            vj = v_ref[pl.ds(start, blk), :]
            vs[d] = [jnp.where(m & (j >= 0), vj, zero) for m in head_lanes]
        chains = [(s, u, h) for s in range(nsub) for u in range(group) for h in range(2)]
        zs = {c: lax.dot_general(qh[c[0]][c[2]], ks[c[0] - c[1]], (((1,), (1,)), ((), ())),
                                 preferred_element_type=F32) for c in chains}
        lps, rs = {}, {}
        for c in chains:
            lp, l1m = _log_sigmoid_pair(zs[c])
            if first and c[1] == 0:
                l1m = jnp.where(strictly_earlier, l1m, 0.0)
            lps[c] = lp
            rs[c] = jnp.dot(l1m.astype(BF16), tri_ref[...], preferred_element_type=F32)
        ws = {}
        top = None
        for s in range(nsub):
            for h in range(2):
                cum = carry_ref[s, h]
                for u in range(group):
                    c = (s, u, h)
                    w = jnp.exp(lps[c] + rs[c][:, :blk] + cum)
                    if first and u == 0:
                        w = jnp.where(strictly_earlier, w, 0.0)
                    ws[c] = w.astype(BF16)
                    cum = cum + rs[c][:, blk:]
                carry_ref[s, h] = cum
                top = cum if top is None else jnp.maximum(top, cum)
        for s in range(nsub):
            acc = acc_ref[s]
            for u in range(group):
                for h in range(2):
                    acc = acc + jnp.dot(ws[(s, u, h)], vs[s - u][h], preferred_element_type=F32)
            acc_ref[s] = acc
        return jnp.max(top)

    def more(state):
        g, top = state
        return (g * group <= i * nsub + nsub - 1) & (top > SB_EXIT)

    def step(state):
        g, _ = state
        return g + 1, group_step(g, False)

    lax.while_loop(more, step, (1, group_step(0, True)))
    for s in range(nsub):
        o_ref[s * blk:(s + 1) * blk, :] = acc_ref[s].astype(BF16)


def stick_breaking(hp, q_blk0, k_blk0, v_blk0, batch, seq, blk=128, group=3, nsub=8):
    assert blk == LANES
    tile = nsub * blk
    nq = seq // tile
    n_pairs = SB_HEADS * SB_HEAD_DIM // LANES
    r = np.arange(blk)
    tri = np.concatenate([(r[:, None] > r[None, :]), np.ones((blk, blk), bool)], axis=1)
    tri = jnp.asarray(tri, BF16)
    return pl.pallas_call(
        functools.partial(_sb_kernel, blk=blk, group=group, nsub=nsub),
        grid=(batch, n_pairs, nq),
        in_specs=[
            pl.BlockSpec((tile, LANES), lambda b, p, i: (b * nq + i, q_blk0 + p)),
            pl.BlockSpec((seq, LANES), lambda b, p, i: (b, k_blk0 + p)),
            pl.BlockSpec((seq, LANES), lambda b, p, i: (b, v_blk0 + p)),
            _const_spec(tri.shape),
        ],
        out_specs=pl.BlockSpec((tile, LANES), lambda b, p, i: (b * nq + i, p)),
        out_shape=jax.ShapeDtypeStruct((batch * seq, n_pairs * LANES), BF16),
        scratch_shapes=[pltpu.VMEM((nsub, blk, LANES), F32), pltpu.VMEM((nsub, 2, blk, blk), F32)],
        compiler_params=_params("parallel", "parallel", "arbitrary"),
        name="stick_breaking",
    )(hp, hp, hp, tri)


def _ret_kernel(qk_ref, v_ref, g_ref, cos_ref, sin_ref, mask_ref, rdec_ref, cdec_ref, tdec_ref, gn_ref,
                o_ref, state_ref):
    @pl.when(pl.program_id(1) == 0)
    def _():
        state_ref[...] = jnp.zeros_like(state_ref)

    half = RET_HEADS * RET_DK // 2
    qk = qk_ref[...].astype(F32)
    cos, sin = cos_ref[...], sin_ref[...]

    def rope(t):
        t1, t2 = t[:, :half], t[:, half:]
        return jnp.concatenate([t1 * cos - t2 * sin, t1 * sin + t2 * cos], axis=1)

    qr = rope(qk[:, :2 * half])
    kr = rope(qk[:, 2 * half:])
    kb = kr.astype(BF16)
    kt = kr.T
    lane = lax.broadcasted_iota(jnp.int32, (1, 2 * half), 1)
    head_of_lane = (lane % half) // (RET_DK // 2)
    for h in range(RET_HEADS):
        qh = jnp.where(head_of_lane == h, qr, 0.0)
        vh = v_ref[:, h * RET_DV:(h + 1) * RET_DV]
        scores = lax.dot_general(qh.astype(BF16), kb, (((1,), (1,)), ((), ())), preferred_element_type=F32)
        scores = scores * mask_ref[h]
        o = jnp.dot(scores.astype(BF16), vh, preferred_element_type=F32)
        o = o + jnp.dot((qh * rdec_ref[h]).astype(BF16), state_ref[h].astype(BF16), preferred_element_type=F32)
        kv = jnp.dot((kt * cdec_ref[h]).astype(BF16), vh, preferred_element_type=F32)
        state_ref[h] = tdec_ref[h] * state_ref[h] + kv
        mu = jnp.mean(o, axis=-1, keepdims=True)
        oc = o - mu
        var = jnp.mean(oc * oc, axis=-1, keepdims=True)
        on = oc * lax.rsqrt(var + LN_EPS) * gn_ref[:, h * RET_DV:(h + 1) * RET_DV]
        g = g_ref[:, h * RET_DV:(h + 1) * RET_DV].astype(F32)
        o_ref[:, h * RET_DV:(h + 1) * RET_DV] = (g * jax.nn.sigmoid(g) * on).astype(BF16)


def retention_tables(seq, tile):
    halfdim = RET_DK // 2
    inv_freq = ROPE_BASE ** (-jnp.arange(halfdim, dtype=F32) / halfdim)
    ang = jnp.arange(seq, dtype=F32)[:, None] * inv_freq[None, :]
    cos = jnp.tile(jnp.cos(ang), (1, RET_HEADS))
    sin = jnp.tile(jnp.sin(ang), (1, RET_HEADS))
    log_gamma = jnp.log(1.0 - 2.0 ** (-5.0 - jnp.arange(RET_HEADS, dtype=F32)))
    t = jnp.arange(tile, dtype=F32)
    same_or_earlier_chunk = (t[None, :] // CHUNK) <= (t[:, None] // CHUNK)
    mask = jnp.exp(log_gamma[:, None, None] * jnp.abs(t[:, None] - t[None, :]))
    mask = jnp.where(same_or_earlier_chunk[None], mask, 0.0)
    rdec = jnp.exp(log_gamma[:, None, None] * (t[None, :, None] + 1.0))
    cdec = jnp.exp(log_gamma[:, None, None] * (tile - 1.0 - t[None, None, :]))
    tdec = jnp.broadcast_to(jnp.exp(log_gamma * tile)[:, None, None], (RET_HEADS, 1, RET_DV))
    return cos, sin, mask, rdec, cdec, tdec


def retention(hp, qk_blk, v_blk, g_blk, batch, seq, gn_g, tile=512):
    cos, sin, mask, rdec, cdec, tdec = retention_tables(seq, tile)
    nt = seq // tile
    qkw = 2 * RET_HEADS * RET_DK
    vw = RET_HEADS * RET_DV
    half = RET_HEADS * RET_DK // 2
    return pl.pallas_call(
        _ret_kernel,
        grid=(batch, nt),
        in_specs=[
            pl.BlockSpec((tile, qkw), lambda b, t: (b * nt + t, qk_blk)),
            pl.BlockSpec((tile, vw), lambda b, t: (b * nt + t, v_blk)),
            pl.BlockSpec((tile, vw), lambda b, t: (b * nt + t, g_blk)),
            pl.BlockSpec((tile, half), lambda b, t: (t, 0)),
            pl.BlockSpec((tile, half), lambda b, t: (t, 0)),
            _const_spec(mask.shape), _const_spec(rdec.shape), _const_spec(cdec.shape),
            _const_spec(tdec.shape), _const_spec((1, vw)),
        ],
        out_specs=pl.BlockSpec((tile, vw), lambda b, t: (b * nt + t, 0)),
        out_shape=jax.ShapeDtypeStruct((batch * seq, vw), BF16),
        scratch_shapes=[pltpu.VMEM((RET_HEADS, 2 * half, RET_DV), F32)],
        compiler_params=_params("parallel", "arbitrary"),
        name="retention",
    )(hp, hp, hp, cos, sin, mask, rdec, cdec, tdec, gn_g.reshape(1, vw))


def _first_max(vals, lane):
    m = jnp.max(vals, axis=-1, keepdims=True)
    idx = jnp.min(jnp.where(vals == m, lane, float(LANES)), axis=-1, keepdims=True)
    return m, idx


def _merge_kernel(ga_ref, gb_ref, gc_ref, ya_ref, yb_ref, yc_ref, h_ref, wa_ref, wb_ref, wc_ref, wo_ref,
                  g1_ref, b1_ref, rw_ref, rb_ref, h1_ref, info_ref, infot_ref, cnt_ref, run_ref, merged_ref,
                  hprev_ref, *, col_chunk):
    step = pl.program_id(0)

    @pl.when(step == 0)
    def _():
        run_ref[...] = jnp.zeros_like(run_ref)
        hprev_ref[...] = jnp.zeros_like(hprev_ref)

    _route(hprev_ref[...], rw_ref, rb_ref, info_ref, infot_ref, cnt_ref, run_ref, count=step > 0)

    d_model = h_ref.shape[1]
    for c in range(0, d_model, col_chunk):
        cs = slice(c, c + col_chunk)
        acc = ga_ref[:, cs].astype(F32) * jnp.dot(ya_ref[...], wa_ref[:, cs], preferred_element_type=F32)
        acc = acc + gb_ref[:, cs].astype(F32) * jnp.dot(yb_ref[...], wb_ref[:, cs], preferred_element_type=F32)
        acc = acc + gc_ref[:, cs].astype(F32) * jnp.dot(yc_ref[...], wc_ref[:, cs], preferred_element_type=F32)
        merged_ref[:, cs] = acc.astype(BF16)
    for c in range(0, d_model, col_chunk):
        cs = slice(c, c + col_chunk)
        h1_ref[:, cs] = ALPHA * h_ref[:, cs] + jnp.dot(merged_ref[...], wo_ref[:, cs], preferred_element_type=F32)
    h1 = _layer_norm(h1_ref[...], g1_ref[...], b1_ref[...])
    h1_ref[...] = h1
    hprev_ref[...] = h1


def _route(h1, rw_ref, rb_ref, info_ref, infot_ref, cnt_ref, run_ref, count):
    h_hi = h1.astype(BF16)
    h_lo = (h1 - h_hi.astype(F32)).astype(BF16)
    logits = (jnp.dot(h_hi, rw_ref[0], preferred_element_type=F32)
              + jnp.dot(h_lo, rw_ref[0], preferred_element_type=F32)
              + jnp.dot(h_hi, rw_ref[1], preferred_element_type=F32)) + rb_ref[...]
    tm = logits.shape[0]
    lane_i = lax.broadcasted_iota(jnp.int32, (tm, LANES), 1)
    lane = lane_i.astype(F32)
    ex = jnp.exp(logits - jnp.max(logits, axis=-1, keepdims=True))
    scores = ex / jnp.sum(ex, axis=-1, keepdims=True)
    group = (lane_i // EXPERTS_PER_GROUP).astype(F32)
    best = jnp.zeros((tm, 1), F32)
    best_score = jnp.full((tm, 1), -1.0, F32)
    for gi in range(N_EXPERTS // EXPERTS_PER_GROUP):
        sg = jnp.where(group == float(gi), scores, -1.0)
        m1, i1 = _first_max(sg, lane)
        m2 = jnp.max(jnp.where(lane == i1, -1.0, sg), axis=-1, keepdims=True)
        gs = m1 + m2
        better = gs > best_score
        best = jnp.where(better, float(gi), best)
        best_score = jnp.where(better, gs, best_score)
    masked = jnp.where(group == best, scores, -1.0)
    w1, e1 = _first_max(masked, lane)
    w2, e2 = _first_max(jnp.where(lane == e1, -2.0, masked), lane)
    den = w1 + w2
    gate1, gate2 = w1 / den, w2 / den

    onehot = jnp.where(((lane == e1) | (lane == e2)) & count, 1.0, 0.0)
    run_ref[0:1, :] = run_ref[0:1, :] + jnp.sum(onehot, axis=0, keepdims=True)
    cnt_ref[...] = run_ref[...]

    info = jnp.zeros((tm, LANES), F32)
    for k, val in enumerate((e1, e2, gate1, gate2)):
        info = jnp.where(lane_i == k, val, info)
    info_ref[...] = info
    infot_ref[...] = info.T[:SUBLANES]


def merge_norm_route(hp, ya, yb, yc, h, wa, wb, wc, wo, ln_g, ln_b, rw, rb, tm=512, col_chunk=512):
    T, D = h.shape
    nt = T // tm

    def tile(col):
        return lambda i: (jnp.minimum(i, nt - 1), col)

    routed = lambda i: jnp.maximum(i - 1, 0)
    return pl.pallas_call(
        functools.partial(_merge_kernel, col_chunk=col_chunk),
        grid=(nt + 1,),
        in_specs=[
            pl.BlockSpec((tm, D), tile(0)), pl.BlockSpec((tm, D), tile(1)), pl.BlockSpec((tm, D), tile(2)),
            pl.BlockSpec((tm, ya.shape[1]), tile(0)), pl.BlockSpec((tm, yb.shape[1]), tile(0)),
            pl.BlockSpec((tm, yc.shape[1]), tile(0)), pl.BlockSpec((tm, D), tile(0)),
            _const_spec(wa.shape), _const_spec(wb.shape), _const_spec(wc.shape), _const_spec(wo.shape),
            _const_spec((1, D)), _const_spec((1, D)), _const_spec(rw.shape), _const_spec(rb.shape),
        ],
        out_specs=[pl.BlockSpec((tm, D), tile(0)), pl.BlockSpec((tm, LANES), lambda i: (routed(i), 0)),
                   pl.BlockSpec((SUBLANES, tm), lambda i: (0, routed(i))),
                   pl.BlockSpec((SUBLANES, LANES), lambda i: (0, 0))],
        out_shape=[jax.ShapeDtypeStruct((T, D), F32), jax.ShapeDtypeStruct((T, LANES), F32),
                   jax.ShapeDtypeStruct((SUBLANES, T), F32), jax.ShapeDtypeStruct((SUBLANES, LANES), F32)],
        scratch_shapes=[pltpu.VMEM((SUBLANES, LANES), F32), pltpu.VMEM((tm, D), BF16), pltpu.VMEM((tm, D), F32)],
        compiler_params=_params("arbitrary"),
        name="merge_norm_route",
    )(hp, hp, hp, ya, yb, yc, h, wa, wb, wc, wo, ln_g.reshape(1, D), ln_b.reshape(1, D), rw, rb)


ROW_UNROLL = 8


def _expert_kernel(be_ref, nused_ref, base_ref, rows_ref, h_ref, w1_ref, w3_ref, w2_ref, ys_ref,
                   xa, xb, ya, yb, w1b, w3b, w2b, gsem, ssem, *, n_tok):
    j = pl.program_id(0)
    n_used = nused_ref[0]
    blk = xa.shape[0]
    n_blocks = base_ref.shape[0] - 1
    tok_mask = n_tok - 1
    assert n_tok & tok_mask == 0

    def gather_copy(base, r, x_dst, sem):
        tok = rows_ref[base + r] & tok_mask
        return pltpu.make_async_copy(h_ref.at[pl.ds(tok, 1)], x_dst.at[pl.ds(r, 1)], sem)

    def scatter_copy(base, r, y_src, sem):
        return pltpu.make_async_copy(y_src.at[pl.ds(r, 1)], ys_ref.at[pl.ds(rows_ref[base + r], 1)], sem)

    def wait_gathered(x_dst, sem):
        pltpu.make_async_copy(h_ref.at[pl.ds(0, blk)], x_dst, sem).wait()

    def wait_scattered(y_src, sem):
        pltpu.make_async_copy(y_src, ys_ref.at[pl.ds(0, blk)], sem).wait()

    def rolled(start_row_copy):
        def body(c, carry):
            for k in range(ROW_UNROLL):
                start_row_copy(c * ROW_UNROLL + k)
            return carry
        lax.fori_loop(0, blk // ROW_UNROLL, body, 0)

    def block_step(b, x_own, x_other, y_own, y_other, g_own, g_other, s_own, s_other, leads_pair):
        @pl.when(b < n_used)
        def _():
            if leads_pair:
                @pl.when(b == 0)
                def _():
                    rolled(lambda r: gather_copy(base_ref[0], r, x_own, g_own).start())
                    y_other[...] = jnp.zeros_like(y_other)

            wait_gathered(x_own, g_own)

            @pl.when(b >= 1)
            def _():
                wait_scattered(y_own, s_own)

            if leads_pair:
                @pl.when((b == 0) | (be_ref[b] != be_ref[jnp.maximum(b - 2, 0)]))
                def _():
                    w1b[...] = w1_ref[0, 0].astype(BF16)
                    w3b[...] = w3_ref[0, 0].astype(BF16)
                    w2b[...] = w2_ref[0, 0].astype(BF16)

            base_next = base_ref[jnp.minimum(b + 1, n_blocks)]
            base_prev = base_ref[jnp.where(b >= 1, b - 1, n_blocks)]
            for r in range(blk):
                gather_copy(base_next, r, x_other, g_other).start(priority=0)
                scatter_copy(base_prev, r, y_other, s_other).start(priority=1)
            x = x_own[...].astype(BF16)
            a = jnp.dot(x, w1b[...], preferred_element_type=F32)
            g = jnp.dot(x, w3b[...], preferred_element_type=F32)
            hb = (a * jax.nn.sigmoid(a) * g).astype(BF16)
            y_own[...] = jnp.dot(hb, w2b[...], preferred_element_type=F32)

            @pl.when(b == n_used - 1)
            def _():
                wait_scattered(y_other, s_other)
                rolled(lambda r: scatter_copy(base_ref[b], r, y_own, s_own).start())
                wait_scattered(y_own, s_own)
                wait_gathered(x_other, g_other)

    block_step(2 * j, xa, xb, ya, yb, gsem.at[0], gsem.at[1], ssem.at[0], ssem.at[1], True)
    block_step(2 * j + 1, xb, xa, yb, ya, gsem.at[1], gsem.at[0], ssem.at[1], ssem.at[0], False)


def moe_experts(h1, rows, base, block_e, n_used, w1, w3, w2, layer, blk):
    T, D = h1.shape
    Fd = w1.shape[-1]
    nb = block_e.shape[0]
    assert nb % 2 == 0
    wspec = lambda s: pl.BlockSpec((1, 1) + s, lambda j, be, *_: (layer, be[2 * j], 0, 0))
    return pl.pallas_call(
        functools.partial(_expert_kernel, n_tok=T),
        grid_spec=pltpu.PrefetchScalarGridSpec(
            num_scalar_prefetch=4,
            grid=(nb // 2,),
            in_specs=[pl.BlockSpec(memory_space=pl.ANY), wspec((D, Fd)), wspec((D, Fd)), wspec((Fd, D))],
            out_specs=pl.BlockSpec(memory_space=pl.ANY),
            scratch_shapes=[pltpu.VMEM((blk, D), F32), pltpu.VMEM((blk, D), F32),
                            pltpu.VMEM((blk, D), F32), pltpu.VMEM((blk, D), F32),
                            pltpu.VMEM((D, Fd), BF16), pltpu.VMEM((D, Fd), BF16), pltpu.VMEM((Fd, D), BF16),
                            pltpu.SemaphoreType.DMA((2,)), pltpu.SemaphoreType.DMA((2,))],
        ),
        out_shape=jax.ShapeDtypeStruct((2 * T + blk, D), F32),
        compiler_params=pltpu.CompilerParams(dimension_semantics=("arbitrary",), vmem_limit_bytes=VMEM_LIMIT,
                                             has_side_effects=True),
        name="moe_experts",
    )(block_e, n_used, base, rows, h1, w1, w3, w2)


def _combine_kernel(y1_ref, y2_ref, h_ref, info_ref, g_ref, b_ref, o_ref):
    o_ref[...] = _moe_post_norm(y1_ref[...], y2_ref[...], h_ref[...], info_ref[...], g_ref[...], b_ref[...])


def moe_combine(ys, h, info, ln_g, ln_b, tm=512):
    T, D = h.shape
    nt = T // tm
    return pl.pallas_call(
        _combine_kernel,
        grid=(nt,),
        in_specs=[pl.BlockSpec((tm, D), lambda i: (i, 0)),
                  pl.BlockSpec((tm, D), lambda i: (nt + i, 0)),
                  pl.BlockSpec((tm, D), lambda i: (i, 0)),
                  pl.BlockSpec((tm, LANES), lambda i: (i, 0)),
                  _const_spec((1, D)), _const_spec((1, D))],
        out_specs=pl.BlockSpec((tm, D), lambda i: (i, 0)),
        out_shape=jax.ShapeDtypeStruct((T, D), F32),
        compiler_params=_params("parallel"),
        name="moe_combine",
    )(ys, ys, h, info, ln_g.reshape(1, D), ln_b.reshape(1, D))


def route_tables(info_t, counts, blk, n_blocks):
    n_assign = info_t.shape[1] * 2
    pair = 2 * blk
    experts = info_t[0:2].reshape(-1).astype(jnp.int32)
    pad_experts = jnp.repeat(jnp.arange(N_EXPERTS, dtype=jnp.int32), pair)
    keys = jnp.concatenate([2 * experts, 2 * pad_experts + 1])
    index_bits = int(keys.shape[0] - 1).bit_length()
    packed = jnp.sort((keys << index_bits) | jnp.arange(keys.shape[0], dtype=jnp.int32))
    order = packed & ((1 << index_bits) - 1)
    rows = jnp.where(order < n_assign, order, n_assign + (order - n_assign) % blk)
    cnt = counts[0, :N_EXPERTS].astype(jnp.int32)
    padded = (cnt + pair - 1) // pair * pair
    pends = jnp.cumsum(padded)
    block_row = jnp.arange(n_blocks, dtype=jnp.int32) * blk
    block_e = jnp.minimum(jnp.sum(block_row[:, None] >= pends[None, :], axis=1), N_EXPERTS - 1).astype(jnp.int32)
    of_block = block_e[:, None] == jnp.arange(N_EXPERTS, dtype=jnp.int32)[None, :]

    def per_block(table):
        return jnp.sum(jnp.where(of_block, table[None, :], 0), axis=1)

    segment_start = jnp.cumsum(cnt + pair) - (cnt + pair)
    base = per_block(segment_start) + block_row - per_block(pends - padded)
    all_padding = rows.shape[0] - blk
    base = jnp.where(block_row < pends[-1], base, all_padding)
    base = jnp.concatenate([base, jnp.full((1,), all_padding, jnp.int32)]).astype(jnp.int32)
    n_used = (pends[-1:] // blk).astype(jnp.int32)
    return rows, base, block_e, n_used


def grouped_moe(h1, info_t, counts, w1, w3, w2, layer, blk=512):
    T, D = h1.shape
    n_blocks = (2 * T) // blk + 2 * N_EXPERTS
    rows, base, block_e, n_used = route_tables(info_t, counts, blk, n_blocks)
    return moe_experts(h1, rows, base, block_e, n_used, w1, w3, w2, layer, blk)


def _in_proj_layout(d_model):
    s5w = d_model // 4
    sbw = SB_HEADS * SB_HEAD_DIM
    rqk = RET_HEADS * RET_DK
    rv = RET_HEADS * RET_DV
    sizes = (s5w, sbw, sbw, sbw, rqk, rqk, rv, rv, d_model, d_model, d_model)
    off = np.concatenate([[0], np.cumsum(sizes)]).tolist()

    def permute(w):
        u_a, q_b, k_b, v_b, q_c, k_c, v_c, g_c, ga, gb, gc = [w[:, off[i]:off[i + 1]] for i in range(11)]

        def rope_perm(m):
            m = m.reshape(d_model, RET_HEADS, 2, RET_DK // 2)
            return jnp.swapaxes(m, 1, 2).reshape(d_model, rqk)

        parts = [ga, gb, gc, rope_perm(q_c), rope_perm(k_c) * RET_DK ** -0.5, v_c, g_c,
                 q_b * SB_HEAD_DIM ** -0.5, k_b, v_b, u_a]
        return jnp.concatenate([m.astype(BF16) for m in parts], axis=1)

    o_qk = 3 * d_model
    o_v = o_qk + 2 * rqk
    o_g = o_v + rv
    o_sb = o_g + rv
    o_u = o_sb + 3 * sbw
    blocks = dict(qk=o_qk // (2 * rqk), v=o_v // rv, g=o_g // rv, sb_q=o_sb // LANES,
                  sb_k=(o_sb + sbw) // LANES, sb_v=(o_sb + 2 * sbw) // LANES, u=o_u // s5w)
    assert o_qk % (2 * rqk) == 0 and o_v % rv == 0 and o_g % rv == 0 and o_sb % LANES == 0 and o_u % s5w == 0
    return permute, blocks


def _layer(source, batch, seq, p):
    D = p['w_in'].shape[0]
    permute, blk = _in_proj_layout(D)
    hp, h = in_projection(source, permute(p['w_in']), n_gate_cols=3 * D)
    s5_tile = 512
    bblk, lam, lamk, cblk = s5_tables(p['lam_re'], p['lam_im'], p['log_dt'], p['b_re'], p['b_im'],
                                      p['c_re'], p['c_im'], s5_tile // SUBLANES)
    ya = s5_mixer(hp, blk['u'], batch, seq, bblk, lam, lamk, cblk, p['d_skip'], p['w_glu'].astype(BF16),
                  p['b_glu'], tm=s5_tile)
    yb = stick_breaking(hp, blk['sb_q'], blk['sb_k'], blk['sb_v'], batch, seq)
    yc = retention(hp, blk['qk'], blk['v'], blk['g'], batch, seq, p['gn_g'])
    rw = jnp.zeros((D, LANES), F32).at[:, :N_EXPERTS].set(p['router_w'])
    rw_hi = rw.astype(BF16)
    rw = jnp.stack([rw_hi, (rw - rw_hi.astype(F32)).astype(BF16)])
    rb = jnp.full((1, LANES), NEG_BIG, F32).at[0, :N_EXPERTS].set(p['router_b'])
    h1, info, info_t, counts = merge_norm_route(
        hp, ya, yb, yc, h, p['w_up_a'].astype(BF16), p['w_up_b'].astype(BF16), p['w_up_c'].astype(BF16),
        p['w_out'].astype(BF16), p['ln1_g'], p['ln1_b'], rw, rb)
    ys = grouped_moe(h1, info_t, counts, p['moe_w1'], p['moe_w3'], p['moe_w2'], p['layer'])
    return ("moe", ys, h1, info, p['ln2_g'], p['ln2_b'])


def kernel(x, ln0_g, ln0_b, w_in, s5_lambda_re, s5_lambda_im, s5_log_dt, s5_b_re, s5_b_im, s5_c_re, s5_c_im,
           s5_d, s5_w_glu, s5_b_glu, ret_gn_g, w_up_a, w_up_b, w_up_c, w_out, ln1_g, ln1_b, router_w, router_b,
           moe_w1, moe_w3, moe_w2, ln2_g, ln2_b):
    batch, seq, D = x.shape
    source = ("raw", x.reshape(batch * seq, D), ln0_g, ln0_b)
    for l in range(w_in.shape[0]):
        p = dict(w_in=w_in[l], lam_re=s5_lambda_re[l], lam_im=s5_lambda_im[l], log_dt=s5_log_dt[l],
                 b_re=s5_b_re[l], b_im=s5_b_im[l], c_re=s5_c_re[l], c_im=s5_c_im[l], d_skip=s5_d[l],
                 w_glu=s5_w_glu[l], b_glu=s5_b_glu[l], gn_g=ret_gn_g[l], w_up_a=w_up_a[l], w_up_b=w_up_b[l],
                 w_up_c=w_up_c[l], w_out=w_out[l], ln1_g=ln1_g[l], ln1_b=ln1_b[l], router_w=router_w,
                 router_b=router_b, moe_w1=moe_w1, moe_w3=moe_w3, moe_w2=moe_w2, layer=l, ln2_g=ln2_g[l],
                 ln2_b=ln2_b[l])
        source = _layer(source, batch, seq, p)
    return moe_combine(*source[1:]).reshape(batch, seq, D)
```
